```python
import math
import numpy as np
import jax
import jax.numpy as jnp
from jax import lax

D_MODEL = 1024
BATCH = 1
SEQ = 16384
DEPTH = 4

GRID_W = 64
CTX_LEN = 256
EPS = 1e-6
N_MOD = 9
D_FF = 2816
H_RET = 4
DK_RET = 128
DV_RET = 256
CHUNK_RET = 128
ROPE_BASE = 10000.0
H_GLA = 4
DK_GLA = 128
DV_GLA = 256
GLA_RANK = 16
GLA_NORMALIZER = 16.0
CHUNK_GLA = 64
D_INNER = 2 * D_MODEL
SSD_HEADDIM = 64
H_SSD = D_INNER // SSD_HEADDIM
SSD_GROUPS = 4
D_STATE = 128
CONV_K = 5
CHUNK_SSD = 128
D_XBC = D_INNER + 2 * SSD_GROUPS * D_STATE

IN_LAYOUT = (
    ('ret_q', H_RET * DK_RET), ('ret_k', H_RET * DK_RET),
    ('ret_v', H_RET * DV_RET), ('ret_g', H_RET * DV_RET),
    ('gla_q', H_GLA * DK_GLA), ('gla_k', H_GLA * DK_GLA),
    ('gla_v', H_GLA * DV_GLA), ('gla_r', H_GLA * DV_GLA),
    ('gla_af', GLA_RANK), ('gla_ab', GLA_RANK),
    ('ssd_z', D_INNER), ('ssd_xbc', D_XBC), ('ssd_dtf', H_SSD), ('ssd_dtb', H_SSD),
    ('merge', 3 * D_MODEL),
)
D_IN_PROJ = sum(size for _, size in IN_LAYOUT)

kernel_name = 'hybrid_retention_gla_ssd_prefix_dit'


def rmsnorm(t, g=None):
    tf = t.astype(jnp.float32)
    y = tf * lax.rsqrt(jnp.mean(tf * tf, axis=-1, keepdims=True) + EPS)
    if g is not None:
        y = y * g.astype(jnp.float32)
    return y.astype(t.dtype)


def adaln(cond, w, b):
    m = jax.nn.silu(cond) @ w + b
    return m.reshape(cond.shape[0], N_MOD, 1, D_MODEL)


def pre(t, g, shift, scale):
    return rmsnorm(t, g) * (1 + scale) + shift


def swiglu(h, wi, wo):
    a, u = jnp.split(h @ wi, 2, axis=-1)
    return (jax.nn.silu(a) * u) @ wo


def split_cols(t):
    out, start = {}, 0
    for name, size in IN_LAYOUT:
        out[name] = t[..., start:start + size]
        start += size
    return out


def flip(t):
    return jnp.flip(t, axis=1)


def rope_2d(t):
    B, L, H, dk = t.shape
    rows = L // GRID_W
    row = jnp.broadcast_to(jnp.arange(rows, dtype=jnp.float32)[:, None], (rows, GRID_W)).reshape(L)
    col = jnp.broadcast_to(jnp.arange(GRID_W, dtype=jnp.float32)[None, :], (rows, GRID_W)).reshape(L)
    nf = dk // 4
    freq = ROPE_BASE ** (-jnp.arange(nf, dtype=jnp.float32) / nf)
    ang = jnp.concatenate([row[:, None] * freq, col[:, None] * freq], axis=-1)[None, :, None, :]
    cos, sin = jnp.cos(ang), jnp.sin(ang)
    t1 = t[..., :dk // 2].astype(jnp.float32)
    t2 = t[..., dk // 2:].astype(jnp.float32)
    return jnp.concatenate([t1 * cos - t2 * sin, t1 * sin + t2 * cos], axis=-1).astype(t.dtype)


def dwconv(t, w, b):
    y = lax.conv_general_dilated(
        t, w[:, None, :].astype(t.dtype), window_strides=(1,),
        padding=[(CONV_K // 2, CONV_K // 2)], dimension_numbers=('NWC', 'WIO', 'NWC'),
        feature_group_count=t.shape[-1])
    return y + b


def chunk_scan(q, k, v, a, s0, chunk):
    B, L, H, dv = v.shape
    rep = H // q.shape[2]
    n = L // chunk

    def blocks(t):
        return t.astype(jnp.float32).reshape(B, n, chunk, t.shape[2], t.shape[3]).transpose(1, 0, 3, 2, 4)

    lower = jnp.tril(jnp.ones((chunk, chunk), dtype=bool))[:, :, None]

    def step(s, inp):
        qc, kc, vc, ac = inp
        if rep > 1:
            qc = jnp.repeat(qc, rep, axis=1)
            kc = jnp.repeat(kc, rep, axis=1)
        g = jnp.cumsum(ac, axis=2)
        decay = jnp.exp(jnp.where(lower, g[:, :, :, None, :] - g[:, :, None, :, :], -jnp.inf))
        if ac.shape[-1] == 1:
            att = jnp.einsum('bhid,bhjd->bhij', qc, kc) * decay[..., 0]
        else:
            att = jnp.einsum('bhid,bhjd,bhijd->bhij', qc, kc, decay)
        y = jnp.einsum('bhcd,bhdv->bhcv', qc * jnp.exp(g), s) + jnp.einsum('bhij,bhjv->bhiv', att, vc)
        g_end = g[:, :, -1:, :]
        s = jnp.exp(g_end[:, :, 0, :, None]) * s + jnp.einsum('bhcd,bhcv->bhdv', kc * jnp.exp(g_end - g), vc)
        return s, y

    s, y = lax.scan(step, s0, (blocks(q), blocks(k), blocks(v), blocks(a)))
    return y.transpose(1, 0, 3, 2, 4).reshape(B, L, H, dv).astype(v.dtype), s


def scan_state(k, v, a):
    rep = v.shape[2] // k.shape[2]
    kf = jnp.repeat(k.astype(jnp.float32), rep, axis=2)
    g = jnp.cumsum(a.astype(jnp.float32), axis=1)
    return jnp.einsum('blhd,blhv->bhdv', kf * jnp.exp(g[:, -1:] - g), v.astype(jnp.float32))


def bidir_scan(q, k, v_f, v_b, a_f, a_b, s0_f, s0_b, chunk):
    y_f, s_f = chunk_scan(q, k, v_f, a_f, s0_f, chunk)
    y_b, s_b = chunk_scan(flip(q), flip(k), flip(v_b), flip(a_b), s0_b, chunk)
    return y_f + flip(y_b), s_f, s_b


def prefix_bidir_scan(ctx_in, lat_in, chunk, need_ctx):
    q_c, k_c, vf_c, vb_c, af_c, ab_c = ctx_in
    if need_ctx:
        B, _, H, dv = vf_c.shape
        zeros = jnp.zeros((B, H, k_c.shape[-1], dv), jnp.float32)
        y_ctx, s_f, s_b = bidir_scan(q_c, k_c, vf_c, vb_c, af_c, ab_c, zeros, zeros, chunk)
    else:
        y_ctx = None
        s_f = scan_state(k_c, vf_c, af_c)
        s_b = scan_state(flip(k_c), flip(vb_c), flip(ab_c))
    y_lat, _, _ = bidir_scan(*lat_in, s_f, s_b, chunk)
    return y_ctx, y_lat


def retention_inputs(p, logit, positional):
    B, L, _ = p['ret_q'].shape
    q = p['ret_q'].reshape(B, L, H_RET, DK_RET)
    k = p['ret_k'].reshape(B, L, H_RET, DK_RET) * (DK_RET ** -0.5)
    v = p['ret_v'].reshape(B, L, H_RET, DV_RET)
    if positional:
        q, k = rope_2d(q), rope_2d(k)
    log_gamma = jax.nn.log_sigmoid(logit.astype(jnp.float32))
    a_f = jnp.broadcast_to(log_gamma[0][:, None], (B, L, H_RET, 1))
    a_b = jnp.broadcast_to(log_gamma[1][:, None], (B, L, H_RET, 1))
    return (q, k, v, v, a_f, a_b)


def retention_out(y, gate):
    B, L = y.shape[:2]
    return rmsnorm(y).reshape(B, L, H_RET * DV_RET) * jax.nn.silu(gate)


def gla_inputs(p, wa2, ba):
    B, L, _ = p['gla_q'].shape
    q = p['gla_q'].reshape(B, L, H_GLA, DK_GLA) * (DK_GLA ** -0.5)
    k = p['gla_k'].reshape(B, L, H_GLA, DK_GLA)
    v = p['gla_v'].reshape(B, L, H_GLA, DV_GLA)

    def log_gate(code, w, b):
        z = (code @ w + b).astype(jnp.float32)
        return (jax.nn.log_sigmoid(z) / GLA_NORMALIZER).reshape(B, L, H_GLA, DK_GLA)

    return (q, k, v, v, log_gate(p['gla_af'], wa2[0], ba[0]), log_gate(p['gla_ab'], wa2[1], ba[1]))


def gla_out(y, r, g):
    B, L = y.shape[:2]
    return rmsnorm(y, g).reshape(B, L, H_GLA * DV_GLA) * jax.nn.silu(r)


def ssd_inputs(p, conv_w, conv_b, dt_bias, a_log):
    B, L, _ = p['ssd_z'].shape
    xbc = jax.nn.silu(dwconv(p['ssd_xbc'], conv_w, conv_b))
    nbc = SSD_GROUPS * D_STATE
    xs = xbc[..., :D_INNER].reshape(B, L, H_SSD, SSD_HEADDIM)
    bm = xbc[..., D_INNER:D_INNER + nbc].reshape(B, L, SSD_GROUPS, D_STATE)
    cm = xbc[..., D_INNER + nbc:].reshape(B, L, SSD_GROUPS, D_STATE)
    a = -jnp.exp(a_log.astype(jnp.float32))

    def direction(dt_raw, i):
        dt = jax.nn.softplus(dt_raw.astype(jnp.float32) + dt_bias[i].astype(jnp.float32))
        return (xs * dt[..., None]).astype(xs.dtype), (dt * a[i])[..., None]

    v_f, a_f = direction(p['ssd_dtf'], 0)
    v_b, a_b = direction(p['ssd_dtb'], 1)
    return (cm, bm, v_f, v_b, a_f, a_b), xs


def ssd_out(y, xs, z, d, g):
    B, L = y.shape[:2]
    y = (y + d[:, None] * xs).reshape(B, L, D_INNER) * jax.nn.silu(z)
    y = rmsnorm(y.reshape(B, L, SSD_GROUPS, D_INNER // SSD_GROUPS)).reshape(B, L, D_INNER)
    return y * g


def token_mixing(h_ctx, h_lat, w_in, ret_logit, gla_wa2, gla_ba, gla_norm_g, conv_w, conv_b,
                 dt_bias, a_log, ssd_d, ssd_norm_g, wb_ret, wb_gla, wb_ssd, w_out, need_ctx):
    pc = split_cols(h_ctx @ w_in)
    pl = split_cols(h_lat @ w_in)
    yr_c, yr_l = prefix_bidir_scan(retention_inputs(pc, ret_logit, False),
                                   retention_inputs(pl, ret_logit, True), CHUNK_RET, need_ctx)
    yg_c, yg_l = prefix_bidir_scan(gla_inputs(pc, gla_wa2, gla_ba),
                                   gla_inputs(pl, gla_wa2, gla_ba), CHUNK_GLA, need_ctx)
    sc_in, xs_c = ssd_inputs(pc, conv_w, conv_b, dt_bias, a_log)
    sl_in, xs_l = ssd_inputs(pl, conv_w, conv_b, dt_bias, a_log)
    ys_c, ys_l = prefix_bidir_scan(sc_in, sl_in, CHUNK_SSD, need_ctx)

    def merge(p, yr, yg, ys, xs):
        b_ret = retention_out(yr, p['ret_g']) @ wb_ret
        b_gla = gla_out(yg, p['gla_r'], gla_norm_g) @ wb_gla
        b_ssd = ssd_out(ys, xs, p['ssd_z'], ssd_d, ssd_norm_g) @ wb_ssd
        g_ret, g_gla, g_ssd = jnp.split(jax.nn.sigmoid(p['merge']), 3, axis=-1)
        return (g_ret * b_ret + g_gla * b_gla + g_ssd * b_ssd) @ w_out

    out_lat = merge(pl, yr_l, yg_l, ys_l, xs_l)
    out_ctx = merge(pc, yr_c, yg_c, ys_c, xs_c) if need_ctx else None
    return out_ctx, out_lat


def setup_inputs(seed: int = 0) -> dict:
    key = jax.random.key(seed)
    ks = iter(jax.random.split(key, 32))

    def nrm(shape, scale):
        return jax.random.normal(next(ks), shape, jnp.float32) * scale

    x = nrm((BATCH, SEQ, D_MODEL), 1.0)
    c = nrm((BATCH, D_MODEL), 1.0)
    ctx = nrm((BATCH, CTX_LEN, D_MODEL), 1.0)
    c_ctx = nrm((D_MODEL,), 1.0)
    ada_w = nrm((DEPTH, D_MODEL, N_MOD * D_MODEL), 0.5 * D_MODEL ** -0.5)
    ada_b = nrm((DEPTH, N_MOD * D_MODEL), 0.02)
    norm_g = 1.0 + nrm((DEPTH, 3, D_MODEL), 0.02)
    final_norm_g = 1.0 + nrm((D_MODEL,), 0.02)
    ffn1_wi = nrm((DEPTH, D_MODEL, 2 * D_FF), D_MODEL ** -0.5)
    ffn1_wo = nrm((DEPTH, D_FF, D_MODEL), D_FF ** -0.5)
    ffn2_wi = nrm((DEPTH, D_MODEL, 2 * D_FF), D_MODEL ** -0.5)
    ffn2_wo = nrm((DEPTH, D_FF, D_MODEL), D_FF ** -0.5)
    w_in = nrm((DEPTH, D_MODEL, D_IN_PROJ), D_MODEL ** -0.5)
    ret_base = jnp.log(2.0 ** (5.0 + jnp.arange(H_RET, dtype=jnp.float32)) - 1.0)
    ret_logit = ret_base + nrm((DEPTH, 2, H_RET), 0.1)
    gla_wa2 = nrm((DEPTH, 2, GLA_RANK, H_GLA * DK_GLA), GLA_RANK ** -0.5)
    gla_ba = nrm((DEPTH, 2, H_GLA * DK_GLA), 0.1)
    gla_norm_g = 1.0 + nrm((DEPTH, DV_GLA), 0.02)
    conv_w = nrm((DEPTH, CONV_K, D_XBC), CONV_K ** -0.5)
    conv_b = nrm((DEPTH, D_XBC), 0.02)
    dt = jnp.exp(jax.random.uniform(next(ks), (DEPTH, 2, H_SSD), jnp.float32, math.log(1e-3), math.log(1e-1)))
    dt_bias = dt + jnp.log(-jnp.expm1(-dt))
    a_log = jnp.log(jax.random.uniform(next(ks), (DEPTH, 2, H_SSD), jnp.float32, 1.0, 16.0))
    ssd_d = 1.0 + nrm((DEPTH, H_SSD), 0.02)
    ssd_norm_g = 1.0 + nrm((DEPTH, D_INNER), 0.02)
    wb_ret = nrm((DEPTH, H_RET * DV_RET, D_MODEL), (H_RET * DV_RET) ** -0.5)
    wb_gla = nrm((DEPTH, H_GLA * DV_GLA, D_MODEL), (H_GLA * DV_GLA) ** -0.5)
    wb_ssd = nrm((DEPTH, D_INNER, D_MODEL), D_INNER ** -0.5)
    w_out = nrm((DEPTH, D_MODEL, D_MODEL), D_MODEL ** -0.5)
    return {'x': x, 'c': c, 'ctx': ctx, 'c_ctx': c_ctx, 'ada_w': ada_w, 'ada_b': ada_b,
            'norm_g': norm_g, 'final_norm_g': final_norm_g, 'ffn1_wi': ffn1_wi, 'ffn1_wo': ffn1_wo,
            'ffn2_wi': ffn2_wi, 'ffn2_wo': ffn2_wo, 'w_in': w_in, 'ret_logit': ret_logit,
            'gla_wa2': gla_wa2, 'gla_ba': gla_ba, 'gla_norm_g': gla_norm_g, 'conv_w': conv_w,
            'conv_b': conv_b, 'dt_bias': dt_bias, 'a_log': a_log, 'ssd_d': ssd_d,
            'ssd_norm_g': ssd_norm_g, 'wb_ret': wb_ret, 'wb_gla': wb_gla, 'wb_ssd': wb_ssd,
            'w_out': w_out}


def reference(x, c, ctx, c_ctx, ada_w, ada_b, norm_g, final_norm_g, ffn1_wi, ffn1_wo, ffn2_wi,
              ffn2_wo, w_in, ret_logit, gla_wa2, gla_ba, gla_norm_g, conv_w, conv_b, dt_bias,
              a_log, ssd_d, ssd_norm_g, wb_ret, wb_gla, wb_ssd, w_out):
    for l in range(DEPTH):
        last = l == DEPTH - 1
        m = adaln(c, ada_w[l], ada_b[l])
        mc = adaln(c_ctx[None], ada_w[l], ada_b[l])
        x = x + 0.5 * m[:, 2] * swiglu(pre(x, norm_g[l, 0], m[:, 0], m[:, 1]), ffn1_wi[l], ffn1_wo[l])
        ctx = ctx + 0.5 * mc[:, 2] * swiglu(pre(ctx, norm_g[l, 0], mc[:, 0], mc[:, 1]), ffn1_wi[l], ffn1_wo[l])
        mix_ctx, mix_lat = token_mixing(
            pre(ctx, norm_g[l, 1], mc[:, 3], mc[:, 4]), pre(x, norm_g[l, 1], m[:, 3], m[:, 4]),
            w_in[l], ret_logit[l], gla_wa2[l], gla_ba[l], gla_norm_g[l], conv_w[l], conv_b[l],
            dt_bias[l], a_log[l], ssd_d[l], ssd_norm_g[l], wb_ret[l], wb_gla[l], wb_ssd[l], w_out[l],
            need_ctx=not last)
        x = x + m[:, 5] * mix_lat
        x = x + 0.5 * m[:, 8] * swiglu(pre(x, norm_g[l, 2], m[:, 6], m[:, 7]), ffn2_wi[l], ffn2_wo[l])
        if not last:
            ctx = ctx + mc[:, 5] * mix_ctx
            ctx = ctx + 0.5 * mc[:, 8] * swiglu(pre(ctx, norm_g[l, 2], mc[:, 6], mc[:, 7]), ffn2_wi[l], ffn2_wo[l])
    return rmsnorm(x, final_norm_g)
```

```python
import functools
import math

import jax
import jax.numpy as jnp
from jax import lax
from jax.experimental import pallas as pl
from jax.experimental.pallas import tpu as pltpu

F32 = jnp.float32
BF16 = jnp.bfloat16
HIGHEST = lax.Precision.HIGHEST

D_MODEL = 1024
GRID_W = 64
EPS = 1e-6
N_MOD = 9
D_FF = 2816
H_RET, DK_RET, DV_RET, CHUNK_RET = 4, 128, 256, 128
ROPE_BASE = 10000.0
H_GLA, DK_GLA, DV_GLA, GLA_RANK, CHUNK_GLA = 4, 128, 256, 16, 64
GLA_NORMALIZER = 16.0
GLA_SUB = 16
D_INNER = 2 * D_MODEL
SSD_HEADDIM = 64
H_SSD = D_INNER // SSD_HEADDIM
SSD_GROUPS = 4
D_STATE = 128
CONV_K = 5
CHUNK_SSD = 128
N_BC = SSD_GROUPS * D_STATE
D_XBC = D_INNER + 2 * N_BC

COL_RET = 0
COL_GLA = 3072
COL_XBC = 6144
COL_MERGE = 9216
COL_Z = 12288
D_BIG = 14336
SM_AF, SM_AB, SM_DTF, SM_DTB, D_SMALL = 0, 16, 32, 64, 128

TM = 256
CW = 1024
FF_CHUNK = 1408
VMEM_LIMIT = 56 * 1024 * 1024


def _cparams(sem):
    return pltpu.CompilerParams(dimension_semantics=sem, vmem_limit_bytes=VMEM_LIMIT)


def _resident(shape, index_map):
    return pl.BlockSpec(shape, index_map, pipeline_mode=pl.Buffered(1))


def _sigmoid(x):
    return jax.nn.sigmoid(x)


def _silu(x):
    return x * _sigmoid(x)


def _softplus(x):
    return jnp.maximum(x, 0.0) + jnp.log1p(jnp.exp(-jnp.abs(x)))


def _log_sigmoid(x):
    return jnp.minimum(x, 0.0) - jnp.log1p(jnp.exp(-jnp.abs(x)))


def _rms(x):
    return x * lax.rsqrt(jnp.mean(x * x, axis=-1, keepdims=True) + EPS)


def _pre(x, g, shift, scale):
    return _rms(x) * g * (1.0 + scale) + shift


def _dot(a, b):
    return jnp.dot(a, b, preferred_element_type=F32)


def _dot_exact(a, b):
    return jnp.dot(a, b, precision=HIGHEST, preferred_element_type=F32)


def _dot_nt(a, b):
    return lax.dot_general(a, b, (((1,), (1,)), ((), ())), preferred_element_type=F32)


def _dot_tn(a, b):
    return lax.dot_general(a, b, (((0,), (0,)), ((), ())), preferred_element_type=F32)


def _adaln_kernel(cond_ref, w_ref, b_ref, o_ref):
    cnd = cond_ref[...]
    o_ref[...] = _dot_exact(_silu(cnd), w_ref[...]) + b_ref[...]


def _adaln(cond8, ada_w, ada_b):
    depth = ada_w.shape[0]
    nblk = N_MOD
    return pl.pallas_call(
        _adaln_kernel,
        grid=(depth, nblk),
        in_specs=[
            pl.BlockSpec((8, D_MODEL), lambda l, j: (0, 0)),
            pl.BlockSpec((None, D_MODEL, D_MODEL), lambda l, j: (l, 0, j)),
            pl.BlockSpec((None, 1, D_MODEL), lambda l, j: (l, 0, j)),
        ],
        out_specs=pl.BlockSpec((None, 8, D_MODEL), lambda l, j: (l, 0, j)),
        out_shape=jax.ShapeDtypeStruct((depth, 8, N_MOD * D_MODEL), F32),
        compiler_params=_cparams(("arbitrary", "arbitrary")),
        name="adaln",
    )(cond8, ada_w, ada_b.reshape(depth, 1, N_MOD * D_MODEL))


def _ffn_kernel(x_ref, mod_ref, g_ref, wa_ref, wu_ref, wo_ref, o_ref, *, k0, gi):
    x = x_ref[...]
    h = _pre(x, g_ref[gi:gi + 1, :], mod_ref[0, k0:k0 + 1, :], mod_ref[0, k0 + 1:k0 + 2, :])
    hb = h.astype(BF16)
    out = None
    for c in range(D_FF // FF_CHUNK):
        sl = slice(c * FF_CHUNK, (c + 1) * FF_CHUNK)
        a = _dot(hb, wa_ref[:, sl])
        u = _dot(hb, wu_ref[:, sl])
        t = (_silu(a) * u).astype(BF16)
        part = _dot(t, wo_ref[sl, :])
        out = part if out is None else out + part
    o_ref[...] = x + 0.5 * mod_ref[0, k0 + 2:k0 + 3, :] * out


def _ffn(xa, mod, norm_g, wi, wo, l, k0, gi, nct):
    n = xa.shape[0]
    return pl.pallas_call(
        functools.partial(_ffn_kernel, k0=k0, gi=gi),
        grid=(n // TM,),
        in_specs=[
            pl.BlockSpec((TM, D_MODEL), lambda i: (i, 0)),
            pl.BlockSpec((None, 1, N_MOD, D_MODEL), lambda i: (l, (i >= nct).astype(jnp.int32), 0, 0)),
            _resident((None, 3, D_MODEL), lambda i: (l, 0, 0)),
            _resident((None, D_MODEL, D_FF), lambda i: (l, 0, 0)),
            _resident((None, D_MODEL, D_FF), lambda i: (l, 0, 1)),
            _resident((None, D_FF, D_MODEL), lambda i: (l, 0, 0)),
        ],
        out_specs=pl.BlockSpec((TM, D_MODEL), lambda i: (i, 0)),
        out_shape=jax.ShapeDtypeStruct((n, D_MODEL), F32),
        compiler_params=_cparams(("arbitrary",)),
        name="ffn",
    )(xa, mod, norm_g, wi, wi, wo)


def _inproj_kernel(x_ref, mod_ref, g_ref, cos_ref, sin_ref, w_ref, cs_ref, ws_ref,
                   p_ref, ps_ref, h_ref):
    j = pl.program_id(1)

    @pl.when(j == 0)
    def _():
        h = _pre(x_ref[...], g_ref[1:2, :], mod_ref[0, 3:4, :], mod_ref[0, 4:5, :])
        h_ref[...] = h.astype(BF16)
        ps_ref[...] = _dot_exact(h, ws_ref[...])

    acc = _dot(h_ref[...], w_ref[...]) * cs_ref[...]

    @pl.when(j == 0)
    def _():
        cos = cos_ref[...]
        sin = sin_ref[...]
        for b in range(CW // DK_RET):
            t = acc[:, b * DK_RET:(b + 1) * DK_RET]
            p_ref[:, b * DK_RET:(b + 1) * DK_RET] = (
                t * cos + pltpu.roll(t, DK_RET // 2, 1) * sin).astype(BF16)

    @pl.when(j != 0)
    def _():
        p_ref[...] = acc.astype(BF16)


def _inproj(xa, mod, norm_g, cos_t, sin_t, w_big, colscale, w_small, l, nct):
    n = xa.shape[0]
    return pl.pallas_call(
        _inproj_kernel,
        grid=(n // TM, D_BIG // CW),
        in_specs=[
            pl.BlockSpec((TM, D_MODEL), lambda i, j: (i, 0)),
            pl.BlockSpec((None, 1, N_MOD, D_MODEL), lambda i, j: (l, (i >= nct).astype(jnp.int32), 0, 0)),
            pl.BlockSpec((None, 3, D_MODEL), lambda i, j: (l, 0, 0)),
            pl.BlockSpec((TM, DK_RET), lambda i, j: (i, 0)),
            pl.BlockSpec((TM, DK_RET), lambda i, j: (i, 0)),
            pl.BlockSpec((None, D_MODEL, CW), lambda i, j: (l, 0, j)),
            pl.BlockSpec((1, CW), lambda i, j: (0, j)),
            pl.BlockSpec((None, D_MODEL, D_SMALL), lambda i, j: (l, 0, 0)),
        ],
        out_specs=[
            pl.BlockSpec((TM, CW), lambda i, j: (i, j)),
            pl.BlockSpec((TM, D_SMALL), lambda i, j: (i, 0)),
        ],
        out_shape=[
            jax.ShapeDtypeStruct((n, D_BIG), BF16),
            jax.ShapeDtypeStruct((n, D_SMALL), F32),
        ],
        scratch_shapes=[pltpu.VMEM((TM, D_MODEL), BF16)],
        compiler_params=_cparams(("arbitrary", "arbitrary")),
        name="inproj",
    )(xa, mod, norm_g, cos_t, sin_t, w_big, colscale, w_small)


CONV_STRIP = 512


def _conv_kernel(cur_ref, prev_ref, next_ref, w_ref, b_ref, o_ref, ext_ref, *, nct, ntiles):
    i = pl.program_id(0)
    has_prev = jnp.logical_and(i != 0, i != nct)
    has_next = jnp.logical_and(i != nct - 1, i != ntiles - 1)
    ext_ref[0:8, :] = jnp.where(has_prev, prev_ref[...].astype(F32), 0.0)
    ext_ref[8:8 + TM, :] = cur_ref[...].astype(F32)
    ext_ref[8 + TM:16 + TM, :] = jnp.where(has_next, next_ref[...].astype(F32), 0.0)
    for s in range(D_XBC // CONV_STRIP):
        cs = slice(s * CONV_STRIP, (s + 1) * CONV_STRIP)
        acc = b_ref[:, cs] + w_ref[0:1, cs] * ext_ref[pl.ds(8 - CONV_K // 2, TM), cs]
        for k in range(1, CONV_K):
            acc = acc + w_ref[k:k + 1, cs] * ext_ref[pl.ds(8 - CONV_K // 2 + k, TM), cs]
        o_ref[:, cs] = _silu(acc).astype(BF16)


def _conv(p, conv_w, conv_b, l, nct):
    n = p.shape[0]
    ntiles = n // TM
    r8 = TM // 8
    nblk8 = n // 8
    cb = COL_XBC // D_XBC
    return pl.pallas_call(
        functools.partial(_conv_kernel, nct=nct, ntiles=ntiles),
        grid=(ntiles,),
        in_specs=[
            pl.BlockSpec((TM, D_XBC), lambda i: (i, cb)),
            pl.BlockSpec((8, D_XBC), lambda i: (jnp.maximum(i * r8 - 1, 0), cb)),
            pl.BlockSpec((8, D_XBC), lambda i: (jnp.minimum((i + 1) * r8, nblk8 - 1), cb)),
            pl.BlockSpec((None, 8, D_XBC), lambda i: (l, 0, 0)),
            pl.BlockSpec((None, 1, D_XBC), lambda i: (l, 0, 0)),
        ],
        out_specs=pl.BlockSpec((TM, D_XBC), lambda i: (i, 0)),
        out_shape=jax.ShapeDtypeStruct((n, D_XBC), BF16),
        scratch_shapes=[pltpu.VMEM((TM + 16, D_XBC), F32)],
        compiler_params=_cparams(("arbitrary",)),
        name="conv",
    )(p, p, p, conv_w, conv_b)


def _bwd_chunk(t, ncc, n):
    return jnp.where(t < ncc, ncc - 1 - t, n + ncc - 1 - t)


def _ret_kernel(lg_ref, qkf_ref, vf_ref, qkb_ref, vb_ref, yf_ref, yb_ref,
                s_ref, dm_ref, rs_ref, ks_ref, cd_ref):
    c = CHUNK_RET
    t = pl.program_id(0)

    @pl.when(t == 0)
    def _():
        s_ref[...] = jnp.zeros(s_ref.shape, F32)
        la = _log_sigmoid(lg_ref[...])
        ii = lax.broadcasted_iota(jnp.int32, (c, c), 0)
        jj = lax.broadcasted_iota(jnp.int32, (c, c), 1)
        ir = lax.broadcasted_iota(jnp.int32, (c, 1), 0)
        for d in range(2):
            for h in range(H_RET):
                a = la[d * H_RET + h:d * H_RET + h + 1, :]
                a2 = jnp.concatenate([a, a], axis=1)
                if d == 0:
                    dist, keep = ii - jj, ii >= jj
                    rpow = ir + 1
                    kpow = c - 1 - ir
                else:
                    dist, keep = jj - ii, jj >= ii
                    rpow = c - ir
                    kpow = ir
                dm_ref[d, h] = jnp.where(keep, jnp.exp(dist.astype(F32) * a), 0.0)
                rs_ref[d, h] = jnp.exp(rpow.astype(F32) * a2)
                ks_ref[d, h] = jnp.exp(kpow.astype(F32) * a)
                cd_ref[d, h] = jnp.exp(float(c) * jnp.broadcast_to(a2, (8, DV_RET)))

    for d, (qk_ref, v_ref, y_ref) in enumerate(((qkf_ref, vf_ref, yf_ref), (qkb_ref, vb_ref, yb_ref))):
        for h in range(H_RET):
            q = qk_ref[:, h * DK_RET:(h + 1) * DK_RET]
            k = qk_ref[:, H_RET * DK_RET + h * DK_RET:H_RET * DK_RET + (h + 1) * DK_RET]
            v = v_ref[:, h * DV_RET:(h + 1) * DV_RET]
            s = s_ref[d, h]
            att = _dot_nt(q, k) * dm_ref[d, h]
            y = rs_ref[d, h] * _dot(q, s.astype(BF16)) + _dot(att.astype(BF16), v)
            y_ref[:, h * DV_RET:(h + 1) * DV_RET] = y.astype(y_ref.dtype)
            kt = (k.astype(F32) * ks_ref[d, h]).astype(BF16)
            s_ref[d, h] = cd_ref[d, h][0:1, :] * s + _dot_tn(kt, v)


def _ret_scan(p, logit8, ncc_rows):
    n = p.shape[0]
    c = CHUNK_RET
    nch = n // c
    ncc = ncc_rows // c
    bwd = lambda t: _bwd_chunk(t, ncc, nch)
    w = H_RET * DV_RET
    return pl.pallas_call(
        _ret_kernel,
        grid=(nch,),
        in_specs=[
            pl.BlockSpec((8, 128), lambda t: (0, 0)),
            pl.BlockSpec((c, w), lambda t: (t, 0)),
            pl.BlockSpec((c, w), lambda t: (t, 1)),
            pl.BlockSpec((c, w), lambda t: (bwd(t), 0)),
            pl.BlockSpec((c, w), lambda t: (bwd(t), 1)),
        ],
        out_specs=[
            pl.BlockSpec((c, w), lambda t: (t, 0)),
            pl.BlockSpec((c, w), lambda t: (bwd(t), 0)),
        ],
        out_shape=[jax.ShapeDtypeStruct((n, w), BF16)] * 2,
        scratch_shapes=[
            pltpu.VMEM((2, H_RET, DK_RET, DV_RET), F32),
            pltpu.VMEM((2, H_RET, c, c), F32),
            pltpu.VMEM((2, H_RET, c, DV_RET), F32),
            pltpu.VMEM((2, H_RET, c, DK_RET), F32),
            pltpu.VMEM((2, H_RET, 8, DV_RET), F32),
        ],
        compiler_params=_cparams(("arbitrary",)),
        name="ret_scan",
    )(logit8, p, p, p, p)


def _gla_dir(d, qk_ref, v_ref, ps_ref, wa_ref, ba_ref, y_ref, st_ref):
    c = CHUNK_GLA
    nsub = c // GLA_SUB
    ii = lax.broadcasted_iota(jnp.int32, (c, c), 0)
    jj = lax.broadcasted_iota(jnp.int32, (c, c), 1)
    blk_start = jnp.bitwise_and(ii, -GLA_SUB)
    if d == 0:
        causal = jj <= ii
        cum = causal
        ref_m = jj < blk_start
        end_row = c - 1
    else:
        causal = jj >= ii
        cum = causal
        ref_m = jj >= blk_start + GLA_SUB
        end_row = 0
    tri = jnp.concatenate([cum.astype(F32), ref_m.astype(F32)], axis=0)
    z = _dot_exact(ps_ref[...], wa_ref[d]) + ba_ref[d]
    a = _log_sigmoid(z) * (1.0 / GLA_NORMALIZER)
    gr = _dot_exact(tri, a)
    g_all = gr[0:c]
    r_all = gr[c:2 * c]
    col_blk = jnp.right_shift(jj, int(math.log2(GLA_SUB)))
    for h in range(H_GLA):
        ks = slice(h * DK_GLA, (h + 1) * DK_GLA)
        q = qk_ref[:, h * DK_GLA:(h + 1) * DK_GLA].astype(F32)
        k = qk_ref[:, H_GLA * DK_GLA + h * DK_GLA:H_GLA * DK_GLA + (h + 1) * DK_GLA].astype(F32)
        v = v_ref[:, h * DV_GLA:(h + 1) * DV_GLA]
        g = g_all[:, ks]
        r = r_all[:, ks]
        g_end = g[end_row:end_row + 1, :]
        st = st_ref[d, h]
        y = _dot_nt((q * jnp.exp(g)).astype(BF16), st.astype(BF16))
        kt = (k * jnp.exp(r - g)).astype(BF16)
        qs = []
        for jb in range(nsub):
            rj = r[jb * GLA_SUB:jb * GLA_SUB + 1, :]
            qs.append((q * jnp.exp(jnp.minimum(g - rj, 0.0))).astype(BF16))
        pm = _dot_nt(jnp.concatenate(qs, axis=0), kt)
        att = jnp.zeros((c, c), F32)
        for jb in range(nsub):
            att = jnp.where(col_blk == jb, pm[jb * c:(jb + 1) * c], att)
        att = jnp.where(causal, att, 0.0)
        y = y + _dot(att.astype(BF16), v)
        y_ref[:, h * DV_GLA:(h + 1) * DV_GLA] = y.astype(y_ref.dtype)
        ke = (k * jnp.exp(g_end - g)).astype(BF16)
        st_ref[d, h] = jnp.exp(g_end) * st + _dot_tn(v, ke)


def _gla_kernel(qkf_ref, vf_ref, psf_ref, qkb_ref, vb_ref, psb_ref, wa_ref, ba_ref,
                yf_ref, yb_ref, st_ref):
    @pl.when(pl.program_id(0) == 0)
    def _():
        st_ref[...] = jnp.zeros(st_ref.shape, F32)

    _gla_dir(0, qkf_ref, vf_ref, psf_ref, wa_ref, ba_ref, yf_ref, st_ref)
    _gla_dir(1, qkb_ref, vb_ref, psb_ref, wa_ref, ba_ref, yb_ref, st_ref)


def _gla_scan(p, psm, wa_pad, ba, ncc_rows):
    n = p.shape[0]
    c = CHUNK_GLA
    nch = n // c
    ncc = ncc_rows // c
    bwd = lambda t: _bwd_chunk(t, ncc, nch)
    w = H_GLA * DV_GLA
    qk_blk = COL_GLA // w
    return pl.pallas_call(
        _gla_kernel,
        grid=(nch,),
        in_specs=[
            pl.BlockSpec((c, w), lambda t: (t, qk_blk)),
            pl.BlockSpec((c, w), lambda t: (t, qk_blk + 1)),
            pl.BlockSpec((c, D_SMALL), lambda t: (t, 0)),
            pl.BlockSpec((c, w), lambda t: (bwd(t), qk_blk)),
            pl.BlockSpec((c, w), lambda t: (bwd(t), qk_blk + 1)),
            pl.BlockSpec((c, D_SMALL), lambda t: (bwd(t), 0)),
            pl.BlockSpec((2, D_SMALL, H_GLA * DK_GLA), lambda t: (0, 0, 0)),
            pl.BlockSpec((2, 1, H_GLA * DK_GLA), lambda t: (0, 0, 0)),
        ],
        out_specs=[
            pl.BlockSpec((c, w), lambda t: (t, 0)),
            pl.BlockSpec((c, w), lambda t: (bwd(t), 0)),
        ],
        out_shape=[jax.ShapeDtypeStruct((n, w), BF16)] * 2,
        scratch_shapes=[pltpu.VMEM((2, H_GLA, DV_GLA, DK_GLA), F32)],
        compiler_params=_cparams(("arbitrary",)),
        name="gla_scan",
    )(p, p, psm, p, p, psm, wa_pad, ba)


def _ssd_dir(d, xbc_ref, ps_ref, dtb_ref, alog_ref, y_ref, s_ref, e_ref):
    c = CHUNK_SSD
    base = SM_DTF if d == 0 else SM_DTB
    ii = lax.broadcasted_iota(jnp.int32, (c, c), 0)
    jj = lax.broadcasted_iota(jnp.int32, (c, c), 1)
    keep = (jj <= ii) if d == 0 else (jj >= ii)
    end_row = c - 1 if d == 0 else 0
    lane = lax.broadcasted_iota(jnp.int32, (1, D_SMALL), 1)
    in_dir = jnp.logical_and(lane >= base, lane < base + H_SSD)
    dt = jnp.where(in_dir, _softplus(ps_ref[...] + dtb_ref[...]), 0.0)
    a = dt * jnp.where(in_dir, -jnp.exp(alog_ref[...]), 0.0)
    g = _dot_exact(keep.astype(F32), a)
    gt = g.T
    e = e_ref[d]
    gx = _dot_exact(g, e)
    dtx = _dot_exact(dt, e)
    gex = gx[end_row:end_row + 1, :]
    xs = xbc_ref[:, 0:D_INNER].astype(F32)
    vf = xs * dtx
    vd = vf.astype(BF16)
    wv = (vf * jnp.exp(gex - gx)).astype(BF16)
    egx = jnp.exp(gx)
    lane2 = lax.broadcasted_iota(jnp.int32, (c, 2 * SSD_HEADDIM), 1)
    hpg = H_SSD // SSD_GROUPS
    gw = hpg * SSD_HEADDIM
    for grp in range(SSD_GROUPS):
        bg = xbc_ref[:, D_INNER + grp * D_STATE:D_INNER + (grp + 1) * D_STATE]
        cg = xbc_ref[:, D_INNER + N_BC + grp * D_STATE:D_INNER + N_BC + (grp + 1) * D_STATE]
        cb = _dot_nt(cg, bg)
        sg = s_ref[d, grp]
        yi = _dot(cg, sg.astype(BF16))
        for pr in range(hpg // 2):
            h0 = grp * hpg + 2 * pr
            cols = slice(h0 * SSD_HEADDIM, (h0 + 2) * SSD_HEADDIM)
            atts = []
            for hh in (h0, h0 + 1):
                col = base + hh
                diff = g[:, col:col + 1] - gt[col:col + 1, :]
                dec = jnp.where(keep, jnp.exp(jnp.minimum(diff, 0.0)), 0.0)
                atts.append((cb * dec).astype(BF16))
            v2 = vd[:, cols]
            y2 = (_dot(atts[0], jnp.where(lane2 < SSD_HEADDIM, v2, jnp.zeros_like(v2)))
                  + _dot(atts[1], jnp.where(lane2 >= SSD_HEADDIM, v2, jnp.zeros_like(v2))))
            yo = yi[:, pr * 2 * SSD_HEADDIM:(pr + 1) * 2 * SSD_HEADDIM] * egx[:, cols] + y2
            y_ref[:, cols] = yo.astype(y_ref.dtype)
        gcols = slice(grp * gw, (grp + 1) * gw)
        s_ref[d, grp] = jnp.exp(gex[:, gcols]) * sg + _dot_tn(bg, wv[:, gcols])


def _ssd_kernel(xf_ref, psf_ref, xb_ref, psb_ref, dtb_ref, alog_ref, yf_ref, yb_ref, s_ref, e_ref):
    @pl.when(pl.program_id(0) == 0)
    def _():
        s_ref[...] = jnp.zeros(s_ref.shape, F32)
        row = lax.broadcasted_iota(jnp.int32, (D_SMALL, D_INNER), 0)
        head = lax.broadcasted_iota(jnp.int32, (D_SMALL, D_INNER), 1) // SSD_HEADDIM
        e_ref[0] = (row - SM_DTF == head).astype(F32)
        e_ref[1] = (row - SM_DTB == head).astype(F32)

    _ssd_dir(0, xf_ref, psf_ref, dtb_ref, alog_ref, yf_ref, s_ref, e_ref)
    _ssd_dir(1, xb_ref, psb_ref, dtb_ref, alog_ref, yb_ref, s_ref, e_ref)


def _ssd_scan(xbc, psm, dtb_full, alog_full, ncc_rows):
    n = xbc.shape[0]
    c = CHUNK_SSD
    nch = n // c
    ncc = ncc_rows // c
    bwd = lambda t: _bwd_chunk(t, ncc, nch)
    return pl.pallas_call(
        _ssd_kernel,
        grid=(nch,),
        in_specs=[
            pl.BlockSpec((c, D_XBC), lambda t: (t, 0)),
            pl.BlockSpec((c, D_SMALL), lambda t: (t, 0)),
            pl.BlockSpec((c, D_XBC), lambda t: (bwd(t), 0)),
            pl.BlockSpec((c, D_SMALL), lambda t: (bwd(t), 0)),
            pl.BlockSpec((1, D_SMALL), lambda t: (0, 0)),
            pl.BlockSpec((1, D_SMALL), lambda t: (0, 0)),
        ],
        out_specs=[
            pl.BlockSpec((c, D_INNER), lambda t: (t, 0)),
            pl.BlockSpec((c, D_INNER), lambda t: (bwd(t), 0)),
        ],
        out_shape=[jax.ShapeDtypeStruct((n, D_INNER), BF16)] * 2,
        scratch_shapes=[
            pltpu.VMEM((2, SSD_GROUPS, D_STATE, D_INNER // SSD_GROUPS), F32),
            pltpu.VMEM((2, D_SMALL, D_INNER), F32),
        ],
        compiler_params=_cparams(("arbitrary",)),
        name="ssd_scan",
    )(xbc, psm, xbc, psm, dtb_full, alog_full)


def _segnorm(y, width):
    outs = [_rms(y[:, s * width:(s + 1) * width]) for s in range(y.shape[1] // width)]
    return jnp.concatenate(outs, axis=1)


def _merge_kernel(x_ref, mod_ref, rg_ref, gr_ref, mg_ref, z_ref, xs_ref,
                  yrf_ref, yrb_ref, ygf_ref, ygb_ref, ysf_ref, ysb_ref,
                  gng_ref, dx_ref, sng_ref, wbr_ref, wbg_ref, wbs_ref, wo_ref, o_ref):
    yr = yrf_ref[...].astype(F32) + yrb_ref[...].astype(F32)
    br = _segnorm(yr, DV_RET) * _silu(rg_ref[...].astype(F32))
    b_ret = _dot(br.astype(BF16), wbr_ref[...])
    yg = ygf_ref[...].astype(F32) + ygb_ref[...].astype(F32)
    bg = _segnorm(yg, DV_GLA) * gng_ref[...] * _silu(gr_ref[...].astype(F32))
    b_gla = _dot(bg.astype(BF16), wbg_ref[...])
    ys = ysf_ref[...].astype(F32) + ysb_ref[...].astype(F32)
    ys = (ys + dx_ref[...] * xs_ref[...].astype(F32)) * _silu(z_ref[...].astype(F32))
    bs = _segnorm(ys, D_INNER // SSD_GROUPS) * sng_ref[...]
    b_ssd = _dot(bs.astype(BF16), wbs_ref[...])
    gates = _sigmoid(mg_ref[...].astype(F32))
    mix = (gates[:, 0:D_MODEL] * b_ret + gates[:, D_MODEL:2 * D_MODEL] * b_gla
           + gates[:, 2 * D_MODEL:3 * D_MODEL] * b_ssd)
    out = _dot(mix.astype(BF16), wo_ref[...])
    o_ref[...] = x_ref[...] + mod_ref[0, 5:6, :] * out


def _merge(xa, mod, p, xbc, yrf, yrb, ygf, ygb, ysf, ysb, gng, dx, sng, wbr, wbg, wbs, wout, l, nct):
    n = xa.shape[0]
    row = lambda w, cb: pl.BlockSpec((TM, w), lambda i: (i, cb))
    return pl.pallas_call(
        _merge_kernel,
        grid=(n // TM,),
        in_specs=[
            row(D_MODEL, 0),
            pl.BlockSpec((None, 1, N_MOD, D_MODEL), lambda i: (l, (i >= nct).astype(jnp.int32), 0, 0)),
            row(D_MODEL, (COL_RET + 2048) // D_MODEL),
            row(D_MODEL, (COL_GLA + 2048) // D_MODEL),
            row(3 * D_MODEL, COL_MERGE // (3 * D_MODEL)),
            row(D_INNER, COL_Z // D_INNER),
            row(D_INNER, 0),
            row(D_MODEL, 0), row(D_MODEL, 0), row(D_MODEL, 0), row(D_MODEL, 0),
            row(D_INNER, 0), row(D_INNER, 0),
            _resident((None, 1, D_MODEL), lambda i: (l, 0, 0)),
            _resident((None, 1, D_INNER), lambda i: (l, 0, 0)),
            _resident((None, 1, D_INNER), lambda i: (l, 0, 0)),
            _resident((None, D_MODEL, D_MODEL), lambda i: (l, 0, 0)),
            _resident((None, D_MODEL, D_MODEL), lambda i: (l, 0, 0)),
            _resident((None, D_INNER, D_MODEL), lambda i: (l, 0, 0)),
            _resident((None, D_MODEL, D_MODEL), lambda i: (l, 0, 0)),
        ],
        out_specs=row(D_MODEL, 0),
        out_shape=jax.ShapeDtypeStruct((n, D_MODEL), F32),
        compiler_params=_cparams(("arbitrary",)),
        name="merge",
    )(xa, mod, p, p, p, p, xbc, yrf, yrb, ygf, ygb, ysf, ysb, gng, dx, sng, wbr, wbg, wbs, wout)


def _final_kernel(x_ref, g_ref, o_ref):
    o_ref[...] = _rms(x_ref[...]) * g_ref[...]


def _final_norm(xa, g, nct):
    n = xa.shape[0] - nct * TM
    return pl.pallas_call(
        _final_kernel,
        grid=(n // TM,),
        in_specs=[
            pl.BlockSpec((TM, D_MODEL), lambda i: (i + nct, 0)),
            pl.BlockSpec((1, D_MODEL), lambda i: (0, 0)),
        ],
        out_specs=pl.BlockSpec((TM, D_MODEL), lambda i: (i, 0)),
        out_shape=jax.ShapeDtypeStruct((n, D_MODEL), F32),
        compiler_params=_cparams(("arbitrary",)),
        name="final_norm",
    )(xa, g)


def _rope_tables(n_ctx, n_lat):
    rows = n_lat // GRID_W
    row = jnp.broadcast_to(jnp.arange(rows, dtype=F32)[:, None], (rows, GRID_W)).reshape(n_lat)
    col = jnp.broadcast_to(jnp.arange(GRID_W, dtype=F32)[None, :], (rows, GRID_W)).reshape(n_lat)
    nf = DK_RET // 4
    freq = ROPE_BASE ** (-jnp.arange(nf, dtype=F32) / nf)
    ang = jnp.concatenate([row[:, None] * freq, col[:, None] * freq], axis=-1)
    cos, sin = jnp.cos(ang), jnp.sin(ang)
    cos_t = jnp.concatenate([cos, cos], axis=1)
    sin_t = jnp.concatenate([-sin, sin], axis=1)
    cos_t = jnp.concatenate([jnp.ones((n_ctx, DK_RET), F32), cos_t], axis=0)
    sin_t = jnp.concatenate([jnp.zeros((n_ctx, DK_RET), F32), sin_t], axis=0)
    return cos_t, sin_t


def _split_w_in(w_in):
    o = 0
    sizes = (512, 512, 1024, 1024, 512, 512, 1024, 1024, GLA_RANK, GLA_RANK,
             D_INNER, D_XBC, H_SSD, H_SSD, 3 * D_MODEL)
    parts = []
    for s in sizes:
        parts.append(w_in[..., o:o + s])
        o += s
    (rq, rk, rv, rg, gq, gk, gv, gr, af, ab, sz, xbc, dtf, dtb, mg) = parts
    big = jnp.concatenate([rq, rk, rv, rg, gq, gk, gv, gr, xbc, mg, sz], axis=-1).astype(BF16)
    pad = jnp.zeros(w_in.shape[:-1] + (D_SMALL - 2 * GLA_RANK - 2 * H_SSD,), F32)
    small = jnp.concatenate([af, ab, dtf, dtb, pad], axis=-1)
    return big, small


def kernel(x, c, ctx, c_ctx, ada_w, ada_b, norm_g, final_norm_g, ffn1_wi, ffn1_wo, ffn2_wi, ffn2_wo,
           w_in, ret_logit, gla_wa2, gla_ba, gla_norm_g, conv_w, conv_b, dt_bias, a_log, ssd_d,
           ssd_norm_g, wb_ret, wb_gla, wb_ssd, w_out):
    depth = ada_w.shape[0]
    n_ctx, n_lat = ctx.shape[1], x.shape[1]
    assert x.shape[0] == 1 and n_ctx % TM == 0 and n_lat % TM == 0
    nct = n_ctx // TM
    xa = jnp.concatenate([ctx[0], x[0]], axis=0)

    cond8 = jnp.zeros((8, D_MODEL), F32).at[0].set(c_ctx).at[1].set(c[0])
    mod = _adaln(cond8, ada_w, ada_b).reshape(depth, 8, N_MOD, D_MODEL)

    cos_t, sin_t = _rope_tables(n_ctx, n_lat)
    w_big, w_small = _split_w_in(w_in)
    colscale = jnp.ones((1, D_BIG), F32)
    colscale = colscale.at[:, COL_RET + 512:COL_RET + 1024].set(DK_RET ** -0.5)
    colscale = colscale.at[:, COL_GLA:COL_GLA + 512].set(DK_GLA ** -0.5)

    bf = lambda t: t.astype(BF16)
    ffn1_wi, ffn1_wo, ffn2_wi, ffn2_wo = bf(ffn1_wi), bf(ffn1_wo), bf(ffn2_wi), bf(ffn2_wo)
    wb_ret, wb_gla, wb_ssd, w_out = bf(wb_ret), bf(wb_gla), bf(wb_ssd), bf(w_out)

    logit8 = jnp.broadcast_to(ret_logit.reshape(depth, 2 * H_RET, 1), (depth, 2 * H_RET, 128))
    wa_pad = jnp.zeros((depth, 2, D_SMALL, H_GLA * DK_GLA), F32)
    wa_pad = wa_pad.at[:, 0, SM_AF:SM_AF + GLA_RANK].set(gla_wa2[:, 0])
    wa_pad = wa_pad.at[:, 1, SM_AB:SM_AB + GLA_RANK].set(gla_wa2[:, 1])
    ba = gla_ba.reshape(depth, 2, 1, H_GLA * DK_GLA)
    zpad = jnp.zeros((depth, SM_DTF), F32)
    dtb_full = jnp.concatenate([zpad, dt_bias[:, 0], dt_bias[:, 1], zpad], axis=1).reshape(depth, 1, D_SMALL)
    alog_full = jnp.concatenate([zpad, a_log[:, 0], a_log[:, 1], zpad], axis=1).reshape(depth, 1, D_SMALL)
    conv_w8 = jnp.concatenate([conv_w, jnp.zeros((depth, 8 - CONV_K, D_XBC), F32)], axis=1)
    conv_b3 = conv_b.reshape(depth, 1, D_XBC)
    gng = jnp.tile(gla_norm_g, (1, H_GLA)).reshape(depth, 1, H_GLA * DV_GLA)
    dx = jnp.repeat(ssd_d, SSD_HEADDIM, axis=1).reshape(depth, 1, D_INNER)
    sng = ssd_norm_g.reshape(depth, 1, D_INNER)

    for l in range(depth):
        xa = _ffn(xa, mod, norm_g, ffn1_wi, ffn1_wo, l, 0, 0, nct)
        p, psm = _inproj(xa, mod, norm_g, cos_t, sin_t, w_big, colscale, w_small, l, nct)
        xbc = _conv(p, conv_w8, conv_b3, l, nct)
        yrf, yrb = _ret_scan(p, logit8[l], n_ctx)
        ygf, ygb = _gla_scan(p, psm, wa_pad[l], ba[l], n_ctx)
        ysf, ysb = _ssd_scan(xbc, psm, dtb_full[l], alog_full[l], n_ctx)
        xa = _merge(xa, mod, p, xbc, yrf, yrb, ygf, ygb, ysf, ysb, gng, dx, sng,
                    wb_ret, wb_gla, wb_ssd, w_out, l, nct)
        xa = _ffn(xa, mod, norm_g, ffn2_wi, ffn2_wo, l, 6, 2, nct)

    out = _final_norm(xa, final_norm_g.reshape(1, D_MODEL), nct)
    return out[None]
```

```python
import functools
import math

import jax
import jax.numpy as jnp
from jax import lax
from jax.experimental import pallas as pl
from jax.experimental.pallas import tpu as pltpu

F32 = jnp.float32
BF16 = jnp.bfloat16
HIGHEST = lax.Precision.HIGHEST

D_MODEL = 1024
GRID_W = 64
EPS = 1e-6
N_MOD = 9
D_FF = 2816
H_RET, DK_RET, DV_RET, CHUNK_RET = 4, 128, 256, 128
ROPE_BASE = 10000.0
H_GLA, DK_GLA, DV_GLA, GLA_RANK, CHUNK_GLA = 4, 128, 256, 16, 64
GLA_NORMALIZER = 16.0
GLA_SUB = 16
D_INNER = 2 * D_MODEL
SSD_HEADDIM = 64
H_SSD = D_INNER // SSD_HEADDIM
SSD_GROUPS = 4
D_STATE = 128
CONV_K = 5
CHUNK_SSD = 128
N_BC = SSD_GROUPS * D_STATE
D_XBC = D_INNER + 2 * N_BC

COL_RET = 0
COL_GLA = 3072
COL_XBC = 6144
COL_MERGE = 9216
COL_Z = 12288
D_BIG = 14336
SM_AF, SM_AB, SM_DTF, SM_DTB, D_SMALL = 0, 16, 32, 64, 128

TM = 256
CW = 1024
FF_CHUNK = 1408
VMEM_LIMIT = 56 * 1024 * 1024


def _cparams(sem):
    return pltpu.CompilerParams(dimension_semantics=sem, vmem_limit_bytes=VMEM_LIMIT)


def _resident(shape, index_map):
    return pl.BlockSpec(shape, index_map, pipeline_mode=pl.Buffered(1))


def _sigmoid(x):
    return jax.nn.sigmoid(x)


def _silu(x):
    return x * _sigmoid(x)


def _softplus(x):
    return jnp.maximum(x, 0.0) + jnp.log1p(jnp.exp(-jnp.abs(x)))


def _log_sigmoid(x):
    return jnp.minimum(x, 0.0) - jnp.log1p(jnp.exp(-jnp.abs(x)))


def _rms(x):
    return x * lax.rsqrt(jnp.mean(x * x, axis=-1, keepdims=True) + EPS)


def _pre(x, g, shift, scale):
    return _rms(x) * g * (1.0 + scale) + shift


def _dot(a, b):
    return jnp.dot(a, b, preferred_element_type=F32)


def _dot_exact(a, b):
    return jnp.dot(a, b, precision=HIGHEST, preferred_element_type=F32)


def _dot_nt(a, b):
    return lax.dot_general(a, b, (((1,), (1,)), ((), ())), preferred_element_type=F32)


def _dot_tn(a, b):
    return lax.dot_general(a, b, (((0,), (0,)), ((), ())), preferred_element_type=F32)


def _adaln_kernel(cond_ref, w_ref, b_ref, o_ref):
    cnd = cond_ref[...]
    o_ref[...] = _dot_exact(_silu(cnd), w_ref[...]) + b_ref[...]


def _adaln(cond8, ada_w, ada_b):
    depth = ada_w.shape[0]
    nblk = N_MOD
    return pl.pallas_call(
        _adaln_kernel,
        grid=(depth, nblk),
        in_specs=[
            pl.BlockSpec((8, D_MODEL), lambda l, j: (0, 0)),
            pl.BlockSpec((None, D_MODEL, D_MODEL), lambda l, j: (l, 0, j)),
            pl.BlockSpec((None, 1, D_MODEL), lambda l, j: (l, 0, j)),
        ],
        out_specs=pl.BlockSpec((None, 8, D_MODEL), lambda l, j: (l, 0, j)),
        out_shape=jax.ShapeDtypeStruct((depth, 8, N_MOD * D_MODEL), F32),
        compiler_params=_cparams(("arbitrary", "arbitrary")),
        name="adaln",
    )(cond8, ada_w, ada_b.reshape(depth, 1, N_MOD * D_MODEL))


def _ffn_kernel(x_ref, mod_ref, g_ref, wa_ref, wu_ref, wo_ref, o_ref, *, k0, gi):
    x = x_ref[...]
    h = _pre(x, g_ref[gi:gi + 1, :], mod_ref[0, k0:k0 + 1, :], mod_ref[0, k0 + 1:k0 + 2, :])
    hb = h.astype(BF16)
    out = None
    for c in range(D_FF // FF_CHUNK):
        sl = slice(c * FF_CHUNK, (c + 1) * FF_CHUNK)
        a = _dot(hb, wa_ref[:, sl])
        u = _dot(hb, wu_ref[:, sl])
        t = (_silu(a) * u).astype(BF16)
        part = _dot(t, wo_ref[sl, :])
        out = part if out is None else out + part
    o_ref[...] = x + 0.5 * mod_ref[0, k0 + 2:k0 + 3, :] * out


def _ffn(xa, mod, norm_g, wi, wo, l, k0, gi, nct):
    n = xa.shape[0]
    return pl.pallas_call(
        functools.partial(_ffn_kernel, k0=k0, gi=gi),
        grid=(n // TM,),
        in_specs=[
            pl.BlockSpec((TM, D_MODEL), lambda i: (i, 0)),
            pl.BlockSpec((None, 1, N_MOD, D_MODEL), lambda i: (l, (i >= nct).astype(jnp.int32), 0, 0)),
            _resident((None, 3, D_MODEL), lambda i: (l, 0, 0)),
            _resident((None, D_MODEL, D_FF), lambda i: (l, 0, 0)),
            _resident((None, D_MODEL, D_FF), lambda i: (l, 0, 1)),
            _resident((None, D_FF, D_MODEL), lambda i: (l, 0, 0)),
        ],
        out_specs=pl.BlockSpec((TM, D_MODEL), lambda i: (i, 0)),
        out_shape=jax.ShapeDtypeStruct((n, D_MODEL), F32),
        compiler_params=_cparams(("arbitrary",)),
        name="ffn",
    )(xa, mod, norm_g, wi, wi, wo)


def _inproj_kernel(x_ref, mod_ref, g_ref, cos_ref, sin_ref, w_ref, cs_ref, ws_ref,
                   p_ref, ps_ref):
    h = _pre(x_ref[...], g_ref[1:2, :], mod_ref[0, 3:4, :], mod_ref[0, 4:5, :])
    hb = h.astype(BF16)
    ps_ref[...] = _dot_exact(h, ws_ref[...])
    for j in range(D_BIG // CW):
        cols = slice(j * CW, (j + 1) * CW)
        acc = _dot(hb, w_ref[:, cols]) * cs_ref[:, cols]
        if j == 0:
            cos = cos_ref[...]
            sin = sin_ref[...]
            for b in range(CW // DK_RET):
                t = acc[:, b * DK_RET:(b + 1) * DK_RET]
                p_ref[:, b * DK_RET:(b + 1) * DK_RET] = (
                    t * cos + pltpu.roll(t, DK_RET // 2, 1) * sin).astype(BF16)
        else:
            p_ref[:, cols] = acc.astype(BF16)


def _inproj(xa, mod, norm_g, cos_t, sin_t, w_big, colscale, w_small, l, nct):
    n = xa.shape[0]
    return pl.pallas_call(
        _inproj_kernel,
        grid=(n // TM,),
        in_specs=[
            pl.BlockSpec((TM, D_MODEL), lambda i: (i, 0)),
            pl.BlockSpec((None, 1, N_MOD, D_MODEL), lambda i: (l, (i >= nct).astype(jnp.int32), 0, 0)),
            _resident((None, 3, D_MODEL), lambda i: (l, 0, 0)),
            pl.BlockSpec((TM, DK_RET), lambda i: (i, 0)),
            pl.BlockSpec((TM, DK_RET), lambda i: (i, 0)),
            _resident((None, D_MODEL, D_BIG), lambda i: (l, 0, 0)),
            _resident((1, D_BIG), lambda i: (0, 0)),
            _resident((None, D_MODEL, D_SMALL), lambda i: (l, 0, 0)),
        ],
        out_specs=[
            pl.BlockSpec((TM, D_BIG), lambda i: (i, 0)),
            pl.BlockSpec((TM, D_SMALL), lambda i: (i, 0)),
        ],
        out_shape=[
            jax.ShapeDtypeStruct((n, D_BIG), BF16),
            jax.ShapeDtypeStruct((n, D_SMALL), F32),
        ],
        compiler_params=_cparams(("arbitrary",)),
        name="inproj",
    )(xa, mod, norm_g, cos_t, sin_t, w_big, colscale, w_small)


CONV_STRIP = 512


def _conv_kernel(cur_ref, prev_ref, next_ref, w_ref, b_ref, o_ref, ext_ref, *, nct, ntiles):
    i = pl.program_id(0)
    has_prev = jnp.logical_and(i != 0, i != nct)
    has_next = jnp.logical_and(i != nct - 1, i != ntiles - 1)
    ext_ref[0:8, :] = jnp.where(has_prev, prev_ref[...].astype(F32), 0.0)
    ext_ref[8:8 + TM, :] = cur_ref[...].astype(F32)
    ext_ref[8 + TM:16 + TM, :] = jnp.where(has_next, next_ref[...].astype(F32), 0.0)
    for s in range(D_XBC // CONV_STRIP):
        cs = slice(s * CONV_STRIP, (s + 1) * CONV_STRIP)
        acc = b_ref[:, cs] + w_ref[0:1, cs] * ext_ref[pl.ds(8 - CONV_K // 2, TM), cs]
        for k in range(1, CONV_K):
            acc = acc + w_ref[k:k + 1, cs] * ext_ref[pl.ds(8 - CONV_K // 2 + k, TM), cs]
        o_ref[:, cs] = _silu(acc).astype(BF16)


def _conv(p, conv_w, conv_b, l, nct):
    n = p.shape[0]
    ntiles = n // TM
    r8 = TM // 8
    nblk8 = n // 8
    cb = COL_XBC // D_XBC
    return pl.pallas_call(
        functools.partial(_conv_kernel, nct=nct, ntiles=ntiles),
        grid=(ntiles,),
        in_specs=[
            pl.BlockSpec((TM, D_XBC), lambda i: (i, cb)),
            pl.BlockSpec((8, D_XBC), lambda i: (jnp.maximum(i * r8 - 1, 0), cb)),
            pl.BlockSpec((8, D_XBC), lambda i: (jnp.minimum((i + 1) * r8, nblk8 - 1), cb)),
            pl.BlockSpec((None, 8, D_XBC), lambda i: (l, 0, 0)),
            pl.BlockSpec((None, 1, D_XBC), lambda i: (l, 0, 0)),
        ],
        out_specs=pl.BlockSpec((TM, D_XBC), lambda i: (i, 0)),
        out_shape=jax.ShapeDtypeStruct((n, D_XBC), BF16),
        scratch_shapes=[pltpu.VMEM((TM + 16, D_XBC), F32)],
        compiler_params=_cparams(("arbitrary",)),
        name="conv",
    )(p, p, p, conv_w, conv_b)


def _bwd_chunk(t, ncc, n):
    return jnp.where(t < ncc, ncc - 1 - t, n + ncc - 1 - t)


def _ret_kernel(lg_ref, qkf_ref, vf_ref, qkb_ref, vb_ref, yf_ref, yb_ref,
                s_ref, dm_ref, rs_ref, ks_ref, cd_ref):
    c = CHUNK_RET
    t = pl.program_id(0)

    @pl.when(t == 0)
    def _():
        s_ref[...] = jnp.zeros(s_ref.shape, F32)
        la = _log_sigmoid(lg_ref[...])
        ii = lax.broadcasted_iota(jnp.int32, (c, c), 0)
        jj = lax.broadcasted_iota(jnp.int32, (c, c), 1)
        ir = lax.broadcasted_iota(jnp.int32, (c, 1), 0)
        for d in range(2):
            for h in range(H_RET):
                a = la[d * H_RET + h:d * H_RET + h + 1, :]
                a2 = jnp.concatenate([a, a], axis=1)
                if d == 0:
                    dist, keep = ii - jj, ii >= jj
                    rpow = ir + 1
                    kpow = c - 1 - ir
                else:
                    dist, keep = jj - ii, jj >= ii
                    rpow = c - ir
                    kpow = ir
                dm_ref[d, h] = jnp.where(keep, jnp.exp(dist.astype(F32) * a), 0.0)
                rs_ref[d, h] = jnp.exp(rpow.astype(F32) * a2)
                ks_ref[d, h] = jnp.exp(kpow.astype(F32) * a)
                cd_ref[d, h] = jnp.exp(float(c) * jnp.broadcast_to(a2, (8, DV_RET)))

    for d, (qk_ref, v_ref, y_ref) in enumerate(((qkf_ref, vf_ref, yf_ref), (qkb_ref, vb_ref, yb_ref))):
        for h in range(H_RET):
            q = qk_ref[:, h * DK_RET:(h + 1) * DK_RET]
            k = qk_ref[:, H_RET * DK_RET + h * DK_RET:H_RET * DK_RET + (h + 1) * DK_RET]
            v = v_ref[:, h * DV_RET:(h + 1) * DV_RET]
            s = s_ref[d, h]
            att = _dot_nt(q, k) * dm_ref[d, h]
            y = rs_ref[d, h] * _dot(q, s.astype(BF16)) + _dot(att.astype(BF16), v)
            y_ref[:, h * DV_RET:(h + 1) * DV_RET] = y.astype(y_ref.dtype)
            kt = (k.astype(F32) * ks_ref[d, h]).astype(BF16)
            s_ref[d, h] = cd_ref[d, h][0:1, :] * s + _dot_tn(kt, v)


def _ret_scan(p, logit8, ncc_rows):
    n = p.shape[0]
    c = CHUNK_RET
    nch = n // c
    ncc = ncc_rows // c
    bwd = lambda t: _bwd_chunk(t, ncc, nch)
    w = H_RET * DV_RET
    return pl.pallas_call(
        _ret_kernel,
        grid=(nch,),
        in_specs=[
            pl.BlockSpec((8, 128), lambda t: (0, 0)),
            pl.BlockSpec((c, w), lambda t: (t, 0)),
            pl.BlockSpec((c, w), lambda t: (t, 1)),
            pl.BlockSpec((c, w), lambda t: (bwd(t), 0)),
            pl.BlockSpec((c, w), lambda t: (bwd(t), 1)),
        ],
        out_specs=[
            pl.BlockSpec((c, w), lambda t: (t, 0)),
            pl.BlockSpec((c, w), lambda t: (bwd(t), 0)),
        ],
        out_shape=[jax.ShapeDtypeStruct((n, w), BF16)] * 2,
        scratch_shapes=[
            pltpu.VMEM((2, H_RET, DK_RET, DV_RET), F32),
            pltpu.VMEM((2, H_RET, c, c), F32),
            pltpu.VMEM((2, H_RET, c, DV_RET), F32),
            pltpu.VMEM((2, H_RET, c, DK_RET), F32),
            pltpu.VMEM((2, H_RET, 8, DV_RET), F32),
        ],
        compiler_params=_cparams(("arbitrary",)),
        name="ret_scan",
    )(logit8, p, p, p, p)


def _gla_dir(d, qk_ref, v_ref, ps_ref, wa_ref, ba_ref, y_ref, st_ref):
    c = CHUNK_GLA
    nsub = c // GLA_SUB
    ii = lax.broadcasted_iota(jnp.int32, (c, c), 0)
    jj = lax.broadcasted_iota(jnp.int32, (c, c), 1)
    blk_start = jnp.bitwise_and(ii, -GLA_SUB)
    if d == 0:
        causal = jj <= ii
        cum = causal
        ref_m = jj < blk_start
        end_row = c - 1
    else:
        causal = jj >= ii
        cum = causal
        ref_m = jj >= blk_start + GLA_SUB
        end_row = 0
    tri = jnp.concatenate([cum.astype(F32), ref_m.astype(F32)], axis=0)
    z = _dot_exact(ps_ref[...], wa_ref[d]) + ba_ref[d]
    a = _log_sigmoid(z) * (1.0 / GLA_NORMALIZER)
    gr = _dot_exact(tri, a)
    g_all = gr[0:c]
    r_all = gr[c:2 * c]
    col_blk = jnp.right_shift(jj, int(math.log2(GLA_SUB)))
    for h in range(H_GLA):
        ks = slice(h * DK_GLA, (h + 1) * DK_GLA)
        q = qk_ref[:, h * DK_GLA:(h + 1) * DK_GLA].astype(F32)
        k = qk_ref[:, H_GLA * DK_GLA + h * DK_GLA:H_GLA * DK_GLA + (h + 1) * DK_GLA].astype(F32)
        v = v_ref[:, h * DV_GLA:(h + 1) * DV_GLA]
        g = g_all[:, ks]
        r = r_all[:, ks]
        g_end = g[end_row:end_row + 1, :]
        st = st_ref[d, h]
        y = _dot_nt((q * jnp.exp(g)).astype(BF16), st.astype(BF16))
        kt = (k * jnp.exp(r - g)).astype(BF16)
        qs = []
        for jb in range(nsub):
            rj = r[jb * GLA_SUB:jb * GLA_SUB + 1, :]
            qs.append((q * jnp.exp(jnp.minimum(g - rj, 0.0))).astype(BF16))
        pm = _dot_nt(jnp.concatenate(qs, axis=0), kt)
        att = jnp.zeros((c, c), F32)
        for jb in range(nsub):
            att = jnp.where(col_blk == jb, pm[jb * c:(jb + 1) * c], att)
        att = jnp.where(causal, att, 0.0)
        y = y + _dot(att.astype(BF16), v)
        y_ref[:, h * DV_GLA:(h + 1) * DV_GLA] = y.astype(y_ref.dtype)
        ke = (k * jnp.exp(g_end - g)).astype(BF16)
        st_ref[d, h] = jnp.exp(g_end) * st + _dot_tn(v, ke)


def _gla_kernel(qkf_ref, vf_ref, psf_ref, qkb_ref, vb_ref, psb_ref, wa_ref, ba_ref,
                yf_ref, yb_ref, st_ref):
    @pl.when(pl.program_id(0) == 0)
    def _():
        st_ref[...] = jnp.zeros(st_ref.shape, F32)

    _gla_dir(0, qkf_ref, vf_ref, psf_ref, wa_ref, ba_ref, yf_ref, st_ref)
    _gla_dir(1, qkb_ref, vb_ref, psb_ref, wa_ref, ba_ref, yb_ref, st_ref)


def _gla_scan(p, psm, wa_pad, ba, ncc_rows):
    n = p.shape[0]
    c = CHUNK_GLA
    nch = n // c
    ncc = ncc_rows // c
    bwd = lambda t: _bwd_chunk(t, ncc, nch)
    w = H_GLA * DV_GLA
    qk_blk = COL_GLA // w
    return pl.pallas_call(
        _gla_kernel,
        grid=(nch,),
        in_specs=[
            pl.BlockSpec((c, w), lambda t: (t, qk_blk)),
            pl.BlockSpec((c, w), lambda t: (t, qk_blk + 1)),
            pl.BlockSpec((c, D_SMALL), lambda t: (t, 0)),
            pl.BlockSpec((c, w), lambda t: (bwd(t), qk_blk)),
            pl.BlockSpec((c, w), lambda t: (bwd(t), qk_blk + 1)),
            pl.BlockSpec((c, D_SMALL), lambda t: (bwd(t), 0)),
            pl.BlockSpec((2, D_SMALL, H_GLA * DK_GLA), lambda t: (0, 0, 0)),
            pl.BlockSpec((2, 1, H_GLA * DK_GLA), lambda t: (0, 0, 0)),
        ],
        out_specs=[
            pl.BlockSpec((c, w), lambda t: (t, 0)),
            pl.BlockSpec((c, w), lambda t: (bwd(t), 0)),
        ],
        out_shape=[jax.ShapeDtypeStruct((n, w), BF16)] * 2,
        scratch_shapes=[pltpu.VMEM((2, H_GLA, DV_GLA, DK_GLA), F32)],
        compiler_params=_cparams(("arbitrary",)),
        name="gla_scan",
    )(p, p, psm, p, p, psm, wa_pad, ba)


def _ssd_dir(d, xbc_ref, ps_ref, dtb_ref, alog_ref, y_ref, s_ref, e_ref):
    c = CHUNK_SSD
    base = SM_DTF if d == 0 else SM_DTB
    ii = lax.broadcasted_iota(jnp.int32, (c, c), 0)
    jj = lax.broadcasted_iota(jnp.int32, (c, c), 1)
    keep = (jj <= ii) if d == 0 else (jj >= ii)
    end_row = c - 1 if d == 0 else 0
    lane = lax.broadcasted_iota(jnp.int32, (1, D_SMALL), 1)
    in_dir = jnp.logical_and(lane >= base, lane < base + H_SSD)
    dt = jnp.where(in_dir, _softplus(ps_ref[...] + dtb_ref[...]), 0.0)
    a = dt * jnp.where(in_dir, -jnp.exp(alog_ref[...]), 0.0)
    g = _dot_exact(keep.astype(F32), a)
    gt = g.T
    e = e_ref[d]
    gx = _dot_exact(g, e)
    dtx = _dot_exact(dt, e)
    gex = gx[end_row:end_row + 1, :]
    xs = xbc_ref[:, 0:D_INNER].astype(F32)
    vf = xs * dtx
    vd = vf.astype(BF16)
    wv = (vf * jnp.exp(gex - gx)).astype(BF16)
    egx = jnp.exp(gx)
    lane2 = lax.broadcasted_iota(jnp.int32, (c, 2 * SSD_HEADDIM), 1)
    hpg = H_SSD // SSD_GROUPS
    gw = hpg * SSD_HEADDIM
    for grp in range(SSD_GROUPS):
        bg = xbc_ref[:, D_INNER + grp * D_STATE:D_INNER + (grp + 1) * D_STATE]
        cg = xbc_ref[:, D_INNER + N_BC + grp * D_STATE:D_INNER + N_BC + (grp + 1) * D_STATE]
        cb = _dot_nt(cg, bg)
        sg = s_ref[d, grp]
        yi = _dot(cg, sg.astype(BF16))
        for pr in range(hpg // 2):
            h0 = grp * hpg + 2 * pr
            cols = slice(h0 * SSD_HEADDIM, (h0 + 2) * SSD_HEADDIM)
            atts = []
            for hh in (h0, h0 + 1):
                col = base + hh
                diff = g[:, col:col + 1] - gt[col:col + 1, :]
                dec = jnp.where(keep, jnp.exp(jnp.minimum(diff, 0.0)), 0.0)
                atts.append((cb * dec).astype(BF16))
            v2 = vd[:, cols]
            y2 = (_dot(atts[0], jnp.where(lane2 < SSD_HEADDIM, v2, jnp.zeros_like(v2)))
                  + _dot(atts[1], jnp.where(lane2 >= SSD_HEADDIM, v2, jnp.zeros_like(v2))))
            yo = yi[:, pr * 2 * SSD_HEADDIM:(pr + 1) * 2 * SSD_HEADDIM] * egx[:, cols] + y2
            y_ref[:, cols] = yo.astype(y_ref.dtype)
        gcols = slice(grp * gw, (grp + 1) * gw)
        s_ref[d, grp] = jnp.exp(gex[:, gcols]) * sg + _dot_tn(bg, wv[:, gcols])


def _ssd_kernel(xf_ref, psf_ref, xb_ref, psb_ref, dtb_ref, alog_ref, yf_ref, yb_ref, s_ref, e_ref):
    @pl.when(pl.program_id(0) == 0)
    def _():
        s_ref[...] = jnp.zeros(s_ref.shape, F32)
        row = lax.broadcasted_iota(jnp.int32, (D_SMALL, D_INNER), 0)
        head = lax.broadcasted_iota(jnp.int32, (D_SMALL, D_INNER), 1) // SSD_HEADDIM
        e_ref[0] = (row - SM_DTF == head).astype(F32)
        e_ref[1] = (row - SM_DTB == head).astype(F32)

    _ssd_dir(0, xf_ref, psf_ref, dtb_ref, alog_ref, yf_ref, s_ref, e_ref)
    _ssd_dir(1, xb_ref, psb_ref, dtb_ref, alog_ref, yb_ref, s_ref, e_ref)


def _ssd_scan(xbc, psm, dtb_full, alog_full, ncc_rows):
    n = xbc.shape[0]
    c = CHUNK_SSD
    nch = n // c
    ncc = ncc_rows // c
    bwd = lambda t: _bwd_chunk(t, ncc, nch)
    return pl.pallas_call(
        _ssd_kernel,
        grid=(nch,),
        in_specs=[
            pl.BlockSpec((c, D_XBC), lambda t: (t, 0)),
            pl.BlockSpec((c, D_SMALL), lambda t: (t, 0)),
            pl.BlockSpec((c, D_XBC), lambda t: (bwd(t), 0)),
            pl.BlockSpec((c, D_SMALL), lambda t: (bwd(t), 0)),
            pl.BlockSpec((1, D_SMALL), lambda t: (0, 0)),
            pl.BlockSpec((1, D_SMALL), lambda t: (0, 0)),
        ],
        out_specs=[
            pl.BlockSpec((c, D_INNER), lambda t: (t, 0)),
            pl.BlockSpec((c, D_INNER), lambda t: (bwd(t), 0)),
        ],
        out_shape=[jax.ShapeDtypeStruct((n, D_INNER), BF16)] * 2,
        scratch_shapes=[
            pltpu.VMEM((2, SSD_GROUPS, D_STATE, D_INNER // SSD_GROUPS), F32),
            pltpu.VMEM((2, D_SMALL, D_INNER), F32),
        ],
        compiler_params=_cparams(("arbitrary",)),
        name="ssd_scan",
    )(xbc, psm, xbc, psm, dtb_full, alog_full)


def _segnorm(y, width):
    outs = [_rms(y[:, s * width:(s + 1) * width]) for s in range(y.shape[1] // width)]
    return jnp.concatenate(outs, axis=1)


def _merge_kernel(x_ref, mod_ref, rg_ref, gr_ref, mg_ref, z_ref, xs_ref,
                  yrf_ref, yrb_ref, ygf_ref, ygb_ref, ysf_ref, ysb_ref,
                  gng_ref, dx_ref, sng_ref, wbr_ref, wbg_ref, wbs_ref, wo_ref, o_ref):
    yr = yrf_ref[...].astype(F32) + yrb_ref[...].astype(F32)
    br = _segnorm(yr, DV_RET) * _silu(rg_ref[...].astype(F32))
    b_ret = _dot(br.astype(BF16), wbr_ref[...])
    yg = ygf_ref[...].astype(F32) + ygb_ref[...].astype(F32)
    bg = _segnorm(yg, DV_GLA) * gng_ref[...] * _silu(gr_ref[...].astype(F32))
    b_gla = _dot(bg.astype(BF16), wbg_ref[...])
    ys = ysf_ref[...].astype(F32) + ysb_ref[...].astype(F32)
    ys = (ys + dx_ref[...] * xs_ref[...].astype(F32)) * _silu(z_ref[...].astype(F32))
    bs = _segnorm(ys, D_INNER // SSD_GROUPS) * sng_ref[...]
    b_ssd = _dot(bs.astype(BF16), wbs_ref[...])
    gates = _sigmoid(mg_ref[...].astype(F32))
    mix = (gates[:, 0:D_MODEL] * b_ret + gates[:, D_MODEL:2 * D_MODEL] * b_gla
           + gates[:, 2 * D_MODEL:3 * D_MODEL] * b_ssd)
    out = _dot(mix.astype(BF16), wo_ref[...])
    o_ref[...] = x_ref[...] + mod_ref[0, 5:6, :] * out


def _merge(xa, mod, p, xbc, yrf, yrb, ygf, ygb, ysf, ysb, gng, dx, sng, wbr, wbg, wbs, wout, l, nct):
    n = xa.shape[0]
    row = lambda w, cb: pl.BlockSpec((TM, w), lambda i: (i, cb))
    return pl.pallas_call(
        _merge_kernel,
        grid=(n // TM,),
        in_specs=[
            row(D_MODEL, 0),
            pl.BlockSpec((None, 1, N_MOD, D_MODEL), lambda i: (l, (i >= nct).astype(jnp.int32), 0, 0)),
            row(D_MODEL, (COL_RET + 2048) // D_MODEL),
            row(D_MODEL, (COL_GLA + 2048) // D_MODEL),
            row(3 * D_MODEL, COL_MERGE // (3 * D_MODEL)),
            row(D_INNER, COL_Z // D_INNER),
            row(D_INNER, 0),
            row(D_MODEL, 0), row(D_MODEL, 0), row(D_MODEL, 0), row(D_MODEL, 0),
            row(D_INNER, 0), row(D_INNER, 0),
            _resident((None, 1, D_MODEL), lambda i: (l, 0, 0)),
            _resident((None, 1, D_INNER), lambda i: (l, 0, 0)),
            _resident((None, 1, D_INNER), lambda i: (l, 0, 0)),
            _resident((None, D_MODEL, D_MODEL), lambda i: (l, 0, 0)),
            _resident((None, D_MODEL, D_MODEL), lambda i: (l, 0, 0)),
            _resident((None, D_INNER, D_MODEL), lambda i: (l, 0, 0)),
            _resident((None, D_MODEL, D_MODEL), lambda i: (l, 0, 0)),
        ],
        out_specs=row(D_MODEL, 0),
        out_shape=jax.ShapeDtypeStruct((n, D_MODEL), F32),
        compiler_params=_cparams(("arbitrary",)),
        name="merge",
    )(xa, mod, p, p, p, p, xbc, yrf, yrb, ygf, ygb, ysf, ysb, gng, dx, sng, wbr, wbg, wbs, wout)


def _final_kernel(x_ref, g_ref, o_ref):
    o_ref[...] = _rms(x_ref[...]) * g_ref[...]


def _final_norm(xa, g, nct):
    n = xa.shape[0] - nct * TM
    return pl.pallas_call(
        _final_kernel,
        grid=(n // TM,),
        in_specs=[
            pl.BlockSpec((TM, D_MODEL), lambda i: (i + nct, 0)),
            pl.BlockSpec((1, D_MODEL), lambda i: (0, 0)),
        ],
        out_specs=pl.BlockSpec((TM, D_MODEL), lambda i: (i, 0)),
        out_shape=jax.ShapeDtypeStruct((n, D_MODEL), F32),
        compiler_params=_cparams(("arbitrary",)),
        name="final_norm",
    )(xa, g)


def _rope_tables(n_ctx, n_lat):
    rows = n_lat // GRID_W
    row = jnp.broadcast_to(jnp.arange(rows, dtype=F32)[:, None], (rows, GRID_W)).reshape(n_lat)
    col = jnp.broadcast_to(jnp.arange(GRID_W, dtype=F32)[None, :], (rows, GRID_W)).reshape(n_lat)
    nf = DK_RET // 4
    freq = ROPE_BASE ** (-jnp.arange(nf, dtype=F32) / nf)
    ang = jnp.concatenate([row[:, None] * freq, col[:, None] * freq], axis=-1)
    cos, sin = jnp.cos(ang), jnp.sin(ang)
    cos_t = jnp.concatenate([cos, cos], axis=1)
    sin_t = jnp.concatenate([-sin, sin], axis=1)
    cos_t = jnp.concatenate([jnp.ones((n_ctx, DK_RET), F32), cos_t], axis=0)
    sin_t = jnp.concatenate([jnp.zeros((n_ctx, DK_RET), F32), sin_t], axis=0)
    return cos_t, sin_t


def _split_w_in(w_in):
    o = 0
    sizes = (512, 512, 1024, 1024, 512, 512, 1024, 1024, GLA_RANK, GLA_RANK,
             D_INNER, D_XBC, H_SSD, H_SSD, 3 * D_MODEL)
    parts = []
    for s in sizes:
        parts.append(w_in[..., o:o + s])
        o += s
    (rq, rk, rv, rg, gq, gk, gv, gr, af, ab, sz, xbc, dtf, dtb, mg) = parts
    big = jnp.concatenate([rq, rk, rv, rg, gq, gk, gv, gr, xbc, mg, sz], axis=-1).astype(BF16)
    pad = jnp.zeros(w_in.shape[:-1] + (D_SMALL - 2 * GLA_RANK - 2 * H_SSD,), F32)
    small = jnp.concatenate([af, ab, dtf, dtb, pad], axis=-1)
    return big, small


def kernel(x, c, ctx, c_ctx, ada_w, ada_b, norm_g, final_norm_g, ffn1_wi, ffn1_wo, ffn2_wi, ffn2_wo,
           w_in, ret_logit, gla_wa2, gla_ba, gla_norm_g, conv_w, conv_b, dt_bias, a_log, ssd_d,
           ssd_norm_g, wb_ret, wb_gla, wb_ssd, w_out):
    depth = ada_w.shape[0]
    n_ctx, n_lat = ctx.shape[1], x.shape[1]
    assert x.shape[0] == 1 and n_ctx % TM == 0 and n_lat % TM == 0
    nct = n_ctx // TM
    xa = jnp.concatenate([ctx[0], x[0]], axis=0)

    cond8 = jnp.zeros((8, D_MODEL), F32).at[0].set(c_ctx).at[1].set(c[0])
    mod = _adaln(cond8, ada_w, ada_b).reshape(depth, 8, N_MOD, D_MODEL)

    cos_t, sin_t = _rope_tables(n_ctx, n_lat)
    w_big, w_small = _split_w_in(w_in)
    colscale = jnp.ones((1, D_BIG), F32)
    colscale = colscale.at[:, COL_RET + 512:COL_RET + 1024].set(DK_RET ** -0.5)
    colscale = colscale.at[:, COL_GLA:COL_GLA + 512].set(DK_GLA ** -0.5)

    bf = lambda t: t.astype(BF16)
    ffn1_wi, ffn1_wo, ffn2_wi, ffn2_wo = bf(ffn1_wi), bf(ffn1_wo), bf(ffn2_wi), bf(ffn2_wo)
    wb_ret, wb_gla, wb_ssd, w_out = bf(wb_ret), bf(wb_gla), bf(wb_ssd), bf(w_out)

    logit8 = jnp.broadcast_to(ret_logit.reshape(depth, 2 * H_RET, 1), (depth, 2 * H_RET, 128))
    wa_pad = jnp.zeros((depth, 2, D_SMALL, H_GLA * DK_GLA), F32)
    wa_pad = wa_pad.at[:, 0, SM_AF:SM_AF + GLA_RANK].set(gla_wa2[:, 0])
    wa_pad = wa_pad.at[:, 1, SM_AB:SM_AB + GLA_RANK].set(gla_wa2[:, 1])
    ba = gla_ba.reshape(depth, 2, 1, H_GLA * DK_GLA)
    zpad = jnp.zeros((depth, SM_DTF), F32)
    dtb_full = jnp.concatenate([zpad, dt_bias[:, 0], dt_bias[:, 1], zpad], axis=1).reshape(depth, 1, D_SMALL)
    alog_full = jnp.concatenate([zpad, a_log[:, 0], a_log[:, 1], zpad], axis=1).reshape(depth, 1, D_SMALL)
    conv_w8 = jnp.concatenate([conv_w, jnp.zeros((depth, 8 - CONV_K, D_XBC), F32)], axis=1)
    conv_b3 = conv_b.reshape(depth, 1, D_XBC)
    gng = jnp.tile(gla_norm_g, (1, H_GLA)).reshape(depth, 1, H_GLA * DV_GLA)
    dx = jnp.repeat(ssd_d, SSD_HEADDIM, axis=1).reshape(depth, 1, D_INNER)
    sng = ssd_norm_g.reshape(depth, 1, D_INNER)

    for l in range(depth):
        xa = _ffn(xa, mod, norm_g, ffn1_wi, ffn1_wo, l, 0, 0, nct)
        p, psm = _inproj(xa, mod, norm_g, cos_t, sin_t, w_big, colscale, w_small, l, nct)
        xbc = _conv(p, conv_w8, conv_b3, l, nct)
        yrf, yrb = _ret_scan(p, logit8[l], n_ctx)
        ygf, ygb = _gla_scan(p, psm, wa_pad[l], ba[l], n_ctx)
        ysf, ysb = _ssd_scan(xbc, psm, dtb_full[l], alog_full[l], n_ctx)
        xa = _merge(xa, mod, p, xbc, yrf, yrb, ygf, ygb, ysf, ysb, gng, dx, sng,
                    wb_ret, wb_gla, wb_ssd, w_out, l, nct)
        xa = _ffn(xa, mod, norm_g, ffn2_wi, ffn2_wo, l, 6, 2, nct)

    out = _final_norm(xa, final_norm_g.reshape(1, D_MODEL), nct)
    return out[None]
```

```python
import functools
import math

import jax
import jax.numpy as jnp
from jax import lax
from jax.experimental import pallas as pl
from jax.experimental.pallas import tpu as pltpu

F32 = jnp.float32
BF16 = jnp.bfloat16
HIGHEST = lax.Precision.HIGHEST

D_MODEL = 1024
GRID_W = 64
EPS = 1e-6
N_MOD = 9
D_FF = 2816
H_RET, DK_RET, DV_RET, CHUNK_RET = 4, 128, 256, 128
ROPE_BASE = 10000.0
H_GLA, DK_GLA, DV_GLA, GLA_RANK, CHUNK_GLA = 4, 128, 256, 16, 64
GLA_NORMALIZER = 16.0
GLA_SUB = 16
D_INNER = 2 * D_MODEL
SSD_HEADDIM = 64
H_SSD = D_INNER // SSD_HEADDIM
SSD_GROUPS = 4
D_STATE = 128
CONV_K = 5
CHUNK_SSD = 128
N_BC = SSD_GROUPS * D_STATE
D_XBC = D_INNER + 2 * N_BC

COL_RET = 0
COL_GLA = 3072
COL_XBC = 6144
COL_MERGE = 9216
COL_Z = 12288
D_BIG = 14336
SM_AF, SM_AB, SM_DTF, SM_DTB, D_SMALL = 0, 16, 32, 64, 128

TM = 256
CW = 1024
FF_SPLITS = (0, 1536, D_FF)
VMEM_LIMIT = 56 * 1024 * 1024


def _cparams(sem):
    return pltpu.CompilerParams(dimension_semantics=sem, vmem_limit_bytes=VMEM_LIMIT)


def _resident(shape, index_map):
    return pl.BlockSpec(shape, index_map, pipeline_mode=pl.Buffered(1))


def _sigmoid(x):
    return 0.5 * jnp.tanh(0.5 * x) + 0.5


def _silu(x):
    return x * _sigmoid(x)


def _softplus(x):
    return jnp.maximum(x, 0.0) + jnp.log1p(jnp.exp(-jnp.abs(x)))


def _log_sigmoid(x):
    return jnp.minimum(x, 0.0) - jnp.log1p(jnp.exp(-jnp.abs(x)))


def _rms(x):
    return x * lax.rsqrt(jnp.mean(x * x, axis=-1, keepdims=True) + EPS)


def _pre(x, g, shift, scale):
    return _rms(x) * g * (1.0 + scale) + shift


def _dot(a, b):
    return jnp.dot(a, b, preferred_element_type=F32)


def _dot_exact(a, b):
    return jnp.dot(a, b, precision=HIGHEST, preferred_element_type=F32)


def _split2(x):
    hi = x.astype(BF16)
    return hi, (x - hi.astype(F32)).astype(BF16)


def _split3(x):
    hi = x.astype(BF16)
    r = x - hi.astype(F32)
    mid = r.astype(BF16)
    return hi, mid, (r - mid.astype(F32)).astype(BF16)


def _dot_sel(m, x):
    hi, mid, lo = _split3(x)
    return _dot(m, hi) + _dot(m, mid) + _dot(m, lo)


def _dot_split(x, w_hi, w_lo):
    x_hi, x_lo = _split2(x)
    return _dot(x_hi, w_hi) + _dot(x_lo, w_hi) + _dot(x_hi, w_lo)


def _dot_nt(a, b):
    return lax.dot_general(a, b, (((1,), (1,)), ((), ())), preferred_element_type=F32)


def _dot_tn(a, b):
    return lax.dot_general(a, b, (((0,), (0,)), ((), ())), preferred_element_type=F32)


def _adaln_kernel(cond_ref, w_ref, b_ref, o_ref):
    cnd = cond_ref[...]
    o_ref[...] = _dot_exact(_silu(cnd), w_ref[...]) + b_ref[...]


def _adaln(cond8, ada_w, ada_b):
    depth = ada_w.shape[0]
    nblk = N_MOD
    return pl.pallas_call(
        _adaln_kernel,
        grid=(depth, nblk),
        in_specs=[
            pl.BlockSpec((8, D_MODEL), lambda l, j: (0, 0)),
            pl.BlockSpec((None, D_MODEL, D_MODEL), lambda l, j: (l, 0, j)),
            pl.BlockSpec((None, 1, D_MODEL), lambda l, j: (l, 0, j)),
        ],
        out_specs=pl.BlockSpec((None, 8, D_MODEL), lambda l, j: (l, 0, j)),
        out_shape=jax.ShapeDtypeStruct((depth, 8, N_MOD * D_MODEL), F32),
        compiler_params=_cparams(("arbitrary", "arbitrary")),
        name="adaln",
    )(cond8, ada_w, ada_b.reshape(depth, 1, N_MOD * D_MODEL))


def _ffn_kernel(x_ref, mod_ref, g_ref, wa_ref, wu_ref, wo_ref, fg_ref, o_ref, *, k0, gi, final):
    x = x_ref[...]
    h = _pre(x, g_ref[gi:gi + 1, :], mod_ref[0, k0:k0 + 1, :], mod_ref[0, k0 + 1:k0 + 2, :])
    hb = h.astype(BF16)
    out = None
    for lo, hi in zip(FF_SPLITS[:-1], FF_SPLITS[1:]):
        sl = slice(lo, hi)
        a = _dot(hb, wa_ref[:, sl])
        u = _dot(hb, wu_ref[:, sl])
        t = (_silu(a) * u).astype(BF16)
        part = _dot(t, wo_ref[sl, :])
        out = part if out is None else out + part
    y = x + 0.5 * mod_ref[0, k0 + 2:k0 + 3, :] * out
    o_ref[...] = _rms(y) * fg_ref[...] if final else y


def _ffn(xa, mod, norm_g, wi, wo, final_g, l, k0, gi, nct, final=False):
    n = xa.shape[0]
    skip = nct if final else 0
    ntiles = n // TM - skip
    return pl.pallas_call(
        functools.partial(_ffn_kernel, k0=k0, gi=gi, final=final),
        grid=(ntiles,),
        in_specs=[
            pl.BlockSpec((TM, D_MODEL), lambda i: (i + skip, 0)),
            pl.BlockSpec((None, 1, N_MOD, D_MODEL),
                         lambda i: (l, (i + skip >= nct).astype(jnp.int32), 0, 0)),
            _resident((None, 3, D_MODEL), lambda i: (l, 0, 0)),
            _resident((None, D_MODEL, D_FF), lambda i: (l, 0, 0)),
            _resident((None, D_MODEL, D_FF), lambda i: (l, 0, 1)),
            _resident((None, D_FF, D_MODEL), lambda i: (l, 0, 0)),
            _resident((1, D_MODEL), lambda i: (0, 0)),
        ],
        out_specs=pl.BlockSpec((TM, D_MODEL), lambda i: (i, 0)),
        out_shape=jax.ShapeDtypeStruct((ntiles * TM, D_MODEL), F32),
        compiler_params=_cparams(("arbitrary",)),
        name="ffn",
    )(xa, mod, norm_g, wi, wi, wo, final_g)


def _inproj_kernel(x_ref, mod_ref, g_ref, cos_ref, sin_ref, w_ref, cs_ref, wsh_ref, wsl_ref,
                   p_ref, ps_ref):
    h = _pre(x_ref[...], g_ref[1:2, :], mod_ref[0, 3:4, :], mod_ref[0, 4:5, :])
    hb = h.astype(BF16)
    ps_ref[...] = _dot(hb, wsh_ref[...]) + _dot((h - hb.astype(F32)).astype(BF16), wsh_ref[...]) \
        + _dot(hb, wsl_ref[...])
    for j in range(D_BIG // CW):
        cols = slice(j * CW, (j + 1) * CW)
        acc = _dot(hb, w_ref[:, cols]) * cs_ref[:, cols]
        if j == 0:
            cos = cos_ref[...]
            sin = sin_ref[...]
            for b in range(CW // DK_RET):
                t = acc[:, b * DK_RET:(b + 1) * DK_RET]
                p_ref[:, b * DK_RET:(b + 1) * DK_RET] = (
                    t * cos + pltpu.roll(t, DK_RET // 2, 1) * sin).astype(BF16)
        else:
            p_ref[:, cols] = acc.astype(BF16)


def _inproj(xa, mod, norm_g, cos_t, sin_t, w_big, colscale, ws_hi, ws_lo, l, nct):
    n = xa.shape[0]
    return pl.pallas_call(
        _inproj_kernel,
        grid=(n // TM,),
        in_specs=[
            pl.BlockSpec((TM, D_MODEL), lambda i: (i, 0)),
            pl.BlockSpec((None, 1, N_MOD, D_MODEL), lambda i: (l, (i >= nct).astype(jnp.int32), 0, 0)),
            _resident((None, 3, D_MODEL), lambda i: (l, 0, 0)),
            pl.BlockSpec((TM, DK_RET), lambda i: (i, 0)),
            pl.BlockSpec((TM, DK_RET), lambda i: (i, 0)),
            _resident((None, D_MODEL, D_BIG), lambda i: (l, 0, 0)),
            _resident((1, D_BIG), lambda i: (0, 0)),
            _resident((None, D_MODEL, D_SMALL), lambda i: (l, 0, 0)),
            _resident((None, D_MODEL, D_SMALL), lambda i: (l, 0, 0)),
        ],
        out_specs=[
            pl.BlockSpec((TM, D_BIG), lambda i: (i, 0)),
            pl.BlockSpec((TM, D_SMALL), lambda i: (i, 0)),
        ],
        out_shape=[
            jax.ShapeDtypeStruct((n, D_BIG), BF16),
            jax.ShapeDtypeStruct((n, D_SMALL), F32),
        ],
        compiler_params=_cparams(("arbitrary",)),
        name="inproj",
    )(xa, mod, norm_g, cos_t, sin_t, w_big, colscale, ws_hi, ws_lo)


CONV_STRIP = 512
CONV_HALO = 16
CONV_BLK = 128


def _conv_kernel(cur_ref, prev_ref, next_ref, w_ref, b_ref, o_ref, ext_ref, *, nct, ntiles):
    i = pl.program_id(0)
    has_prev = jnp.logical_and(i != 0, i != nct)
    has_next = jnp.logical_and(i != nct - 1, i != ntiles - 1)
    halo0 = jnp.zeros((CONV_HALO, D_XBC), BF16)
    ext_ref[0:CONV_HALO, :] = jnp.where(has_prev, prev_ref[...], halo0)
    ext_ref[CONV_HALO:CONV_HALO + TM, :] = cur_ref[...]
    ext_ref[CONV_HALO + TM:2 * CONV_HALO + TM, :] = jnp.where(has_next, next_ref[...], halo0)
    kwin = CONV_BLK + 2 * CONV_HALO
    rr = lax.broadcasted_iota(jnp.int32, (CONV_BLK, kwin), 0)
    cc = lax.broadcasted_iota(jnp.int32, (CONV_BLK, kwin), 1)
    mid = CONV_K // 2
    shift = {k: (cc == rr + CONV_HALO + k - mid).astype(F32).astype(BF16)
             for k in range(CONV_K) if k != mid}
    for blk in range(TM // CONV_BLK):
        rows = slice(blk * CONV_BLK, (blk + 1) * CONV_BLK)
        for s in range(D_XBC // CONV_STRIP):
            cs = slice(s * CONV_STRIP, (s + 1) * CONV_STRIP)
            win = ext_ref[blk * CONV_BLK:blk * CONV_BLK + kwin, cs]
            acc = b_ref[:, cs] + w_ref[mid:mid + 1, cs] * cur_ref[rows, cs].astype(F32)
            for k in range(CONV_K):
                if k != mid:
                    acc = acc + w_ref[k:k + 1, cs] * _dot(shift[k], win)
            o_ref[rows, cs] = _silu(acc).astype(BF16)


def _conv(p, conv_w, conv_b, l, nct):
    n = p.shape[0]
    ntiles = n // TM
    rh = TM // CONV_HALO
    nblkh = n // CONV_HALO
    cb = COL_XBC // D_XBC
    return pl.pallas_call(
        functools.partial(_conv_kernel, nct=nct, ntiles=ntiles),
        grid=(ntiles,),
        in_specs=[
            pl.BlockSpec((TM, D_XBC), lambda i: (i, cb)),
            pl.BlockSpec((CONV_HALO, D_XBC), lambda i: (jnp.maximum(i * rh - 1, 0), cb)),
            pl.BlockSpec((CONV_HALO, D_XBC), lambda i: (jnp.minimum((i + 1) * rh, nblkh - 1), cb)),
            pl.BlockSpec((None, 8, D_XBC), lambda i: (l, 0, 0)),
            pl.BlockSpec((None, 1, D_XBC), lambda i: (l, 0, 0)),
        ],
        out_specs=pl.BlockSpec((TM, D_XBC), lambda i: (i, 0)),
        out_shape=jax.ShapeDtypeStruct((n, D_XBC), BF16),
        scratch_shapes=[pltpu.VMEM((TM + 2 * CONV_HALO, D_XBC), BF16)],
        compiler_params=_cparams(("arbitrary",)),
        name="conv",
    )(p, p, p, conv_w, conv_b)


def _bwd_chunk(t, ncc, n):
    return jnp.where(t < ncc, ncc - 1 - t, n + ncc - 1 - t)


def _ret_kernel(lg_ref, qkf_ref, vf_ref, qkb_ref, vb_ref, yf_ref, yb_ref,
                s_ref, dm_ref, rs_ref, ks_ref, cd_ref):
    c = CHUNK_RET
    t = pl.program_id(0)

    @pl.when(t == 0)
    def _():
        s_ref[...] = jnp.zeros(s_ref.shape, F32)
        la = _log_sigmoid(lg_ref[...])
        ii = lax.broadcasted_iota(jnp.int32, (c, c), 0)
        jj = lax.broadcasted_iota(jnp.int32, (c, c), 1)
        ir = lax.broadcasted_iota(jnp.int32, (c, 1), 0)
        for d in range(2):
            for h in range(H_RET):
                a = la[d * H_RET + h:d * H_RET + h + 1, :]
                a2 = jnp.concatenate([a, a], axis=1)
                if d == 0:
                    dist, keep = ii - jj, ii >= jj
                    rpow = ir + 1
                    kpow = c - 1 - ir
                else:
                    dist, keep = jj - ii, jj >= ii
                    rpow = c - ir
                    kpow = ir
                dm_ref[d, h] = jnp.where(keep, jnp.exp(dist.astype(F32) * a), 0.0)
                rs_ref[d, h] = jnp.exp(rpow.astype(F32) * a2)
                ks_ref[d, h] = jnp.exp(kpow.astype(F32) * a)
                cd_ref[d, h] = jnp.exp(float(c) * jnp.broadcast_to(a2, (8, DV_RET)))

    dirs = ((qkf_ref, vf_ref, yf_ref), (qkb_ref, vb_ref, yb_ref))

    def q_of(qk_ref, h):
        return qk_ref[:, h * DK_RET:(h + 1) * DK_RET]

    def k_of(qk_ref, h):
        return qk_ref[:, H_RET * DK_RET + h * DK_RET:H_RET * DK_RET + (h + 1) * DK_RET]

    qk = [[_dot_nt(q_of(qk_ref, h), k_of(qk_ref, h)) for h in range(H_RET)] for qk_ref, _, _ in dirs]
    qs = [[_dot(q_of(qk_ref, h), s_ref[d, h].astype(BF16)) for h in range(H_RET)]
          for d, (qk_ref, _, _) in enumerate(dirs)]
    for d, (qk_ref, v_ref, y_ref) in enumerate(dirs):
        for h in range(H_RET):
            v = v_ref[:, h * DV_RET:(h + 1) * DV_RET]
            att = (qk[d][h] * dm_ref[d, h]).astype(BF16)
            y = rs_ref[d, h] * qs[d][h] + _dot(att, v)
            y_ref[:, h * DV_RET:(h + 1) * DV_RET] = y.astype(y_ref.dtype)
    for d, (qk_ref, v_ref, y_ref) in enumerate(dirs):
        for h in range(H_RET):
            v = v_ref[:, h * DV_RET:(h + 1) * DV_RET]
            kt = (k_of(qk_ref, h).astype(F32) * ks_ref[d, h]).astype(BF16)
            s_ref[d, h] = cd_ref[d, h][0:1, :] * s_ref[d, h] + _dot_tn(kt, v)


def _ret_scan(p, logit8, ncc_rows):
    n = p.shape[0]
    c = CHUNK_RET
    nch = n // c
    ncc = ncc_rows // c
    bwd = lambda t: _bwd_chunk(t, ncc, nch)
    w = H_RET * DV_RET
    return pl.pallas_call(
        _ret_kernel,
        grid=(nch,),
        in_specs=[
            pl.BlockSpec((8, 128), lambda t: (0, 0)),
            pl.BlockSpec((c, w), lambda t: (t, 0)),
            pl.BlockSpec((c, w), lambda t: (t, 1)),
            pl.BlockSpec((c, w), lambda t: (bwd(t), 0)),
            pl.BlockSpec((c, w), lambda t: (bwd(t), 1)),
        ],
        out_specs=[
            pl.BlockSpec((c, w), lambda t: (t, 0)),
            pl.BlockSpec((c, w), lambda t: (bwd(t), 0)),
        ],
        out_shape=[jax.ShapeDtypeStruct((n, w), BF16)] * 2,
        scratch_shapes=[
            pltpu.VMEM((2, H_RET, DK_RET, DV_RET), F32),
            pltpu.VMEM((2, H_RET, c, c), F32),
            pltpu.VMEM((2, H_RET, c, DV_RET), F32),
            pltpu.VMEM((2, H_RET, c, DK_RET), F32),
            pltpu.VMEM((2, H_RET, 8, DV_RET), F32),
        ],
        compiler_params=_cparams(("arbitrary",)),
        name="ret_scan",
    )(logit8, p, p, p, p)


GLA_BLOCK = 128


def _gla_dir(d, qk_ref, v_ref, ps_ref, wah_ref, wal_ref, ba_ref, y_ref, st_ref):
    c = CHUNK_GLA
    nb = GLA_BLOCK
    nsub = c // GLA_SUB
    bi = lax.broadcasted_iota(jnp.int32, (nb, nb), 0)
    bj = lax.broadcasted_iota(jnp.int32, (nb, nb), 1)
    same = (bi // c) == (bj // c)
    sub_start = jnp.bitwise_and(bi, -GLA_SUB)
    if d == 0:
        cum = jnp.logical_and(same, bj <= bi)
        ref_m = jnp.logical_and(same, bj < sub_start)
    else:
        cum = jnp.logical_and(same, bj >= bi)
        ref_m = jnp.logical_and(same, bj >= sub_start + GLA_SUB)
    tri = jnp.concatenate([cum, ref_m], axis=0).astype(F32).astype(BF16)
    z = _dot_split(ps_ref[...], wah_ref[d], wal_ref[d]) + ba_ref[d]
    a = _log_sigmoid(z) * (1.0 / GLA_NORMALIZER)
    gr = _dot_sel(tri, a)
    ii = lax.broadcasted_iota(jnp.int32, (c, c), 0)
    jj = lax.broadcasted_iota(jnp.int32, (c, c), 1)
    causal = (jj <= ii) if d == 0 else (jj >= ii)
    end_row = c - 1 if d == 0 else 0
    col_blk = jnp.right_shift(jj, int(math.log2(GLA_SUB)))
    order = list(range(nb // c)) if d == 0 else list(range(nb // c - 1, -1, -1))
    pre = {}
    for ci in order:
        rows = slice(ci * c, (ci + 1) * c)
        g_all = gr[ci * c:(ci + 1) * c]
        r_all = gr[nb + ci * c:nb + (ci + 1) * c]
        for h in range(H_GLA):
            ks = slice(h * DK_GLA, (h + 1) * DK_GLA)
            q = qk_ref[rows, h * DK_GLA:(h + 1) * DK_GLA].astype(F32)
            k = qk_ref[rows, H_GLA * DK_GLA + h * DK_GLA:H_GLA * DK_GLA + (h + 1) * DK_GLA].astype(F32)
            g = g_all[:, ks]
            r = r_all[:, ks]
            g_end = g[end_row:end_row + 1, :]
            kt = (k * jnp.exp(r - g)).astype(BF16)
            qs = []
            for jb in range(nsub):
                rj = r[jb * GLA_SUB:jb * GLA_SUB + 1, :]
                qs.append((q * jnp.exp(jnp.minimum(g - rj, 0.0))).astype(BF16))
            pm = _dot_nt(jnp.concatenate(qs, axis=0), kt)
            qg = (q * jnp.exp(g)).astype(BF16)
            ke = (k * jnp.exp(g_end - g)).astype(BF16)
            pre[(ci, h)] = (pm, qg, ke, jnp.exp(g_end))

    def finish(ci):
        rows = slice(ci * c, (ci + 1) * c)
        for h in range(H_GLA):
            pm, qg, ke, e_end = pre[(ci, h)]
            v = v_ref[rows, h * DV_GLA:(h + 1) * DV_GLA]
            att = jnp.zeros((c, c), F32)
            for jb in range(nsub):
                att = jnp.where(col_blk == jb, pm[jb * c:(jb + 1) * c], att)
            att = jnp.where(causal, att, 0.0).astype(BF16)
            st = st_ref[d, h]
            y = _dot_nt(qg, st.astype(BF16)) + _dot(att, v)
            y_ref[rows, h * DV_GLA:(h + 1) * DV_GLA] = y.astype(y_ref.dtype)
            st_ref[d, h] = e_end * st + _dot_tn(v, ke)

    return [functools.partial(finish, ci) for ci in order]


def _gla_kernel(qkf_ref, vf_ref, psf_ref, qkb_ref, vb_ref, psb_ref, wah_ref, wal_ref, ba_ref,
                yf_ref, yb_ref, st_ref):
    @pl.when(pl.program_id(0) == 0)
    def _():
        st_ref[...] = jnp.zeros(st_ref.shape, F32)

    fin_f = _gla_dir(0, qkf_ref, vf_ref, psf_ref, wah_ref, wal_ref, ba_ref, yf_ref, st_ref)
    fin_b = _gla_dir(1, qkb_ref, vb_ref, psb_ref, wah_ref, wal_ref, ba_ref, yb_ref, st_ref)
    for ff, fb in zip(fin_f, fin_b):
        ff()
        fb()


def _gla_scan(p, psm, wa_hi, wa_lo, ba, ncc_rows):
    n = p.shape[0]
    c = GLA_BLOCK
    nch = n // c
    ncc = ncc_rows // c
    bwd = lambda t: _bwd_chunk(t, ncc, nch)
    w = H_GLA * DV_GLA
    qk_blk = COL_GLA // w
    return pl.pallas_call(
        _gla_kernel,
        grid=(nch,),
        in_specs=[
            pl.BlockSpec((c, w), lambda t: (t, qk_blk)),
            pl.BlockSpec((c, w), lambda t: (t, qk_blk + 1)),
            pl.BlockSpec((c, D_SMALL), lambda t: (t, 0)),
            pl.BlockSpec((c, w), lambda t: (bwd(t), qk_blk)),
            pl.BlockSpec((c, w), lambda t: (bwd(t), qk_blk + 1)),
            pl.BlockSpec((c, D_SMALL), lambda t: (bwd(t), 0)),
            pl.BlockSpec((2, D_SMALL, H_GLA * DK_GLA), lambda t: (0, 0, 0)),
            pl.BlockSpec((2, D_SMALL, H_GLA * DK_GLA), lambda t: (0, 0, 0)),
            pl.BlockSpec((2, 1, H_GLA * DK_GLA), lambda t: (0, 0, 0)),
        ],
        out_specs=[
            pl.BlockSpec((c, w), lambda t: (t, 0)),
            pl.BlockSpec((c, w), lambda t: (bwd(t), 0)),
        ],
        out_shape=[jax.ShapeDtypeStruct((n, w), BF16)] * 2,
        scratch_shapes=[pltpu.VMEM((2, H_GLA, DV_GLA, DK_GLA), F32)],
        compiler_params=_cparams(("arbitrary",)),
        name="gla_scan",
    )(p, p, psm, p, p, psm, wa_hi, wa_lo, ba)


def _ssd_dir(d, xbc_ref, ps_ref, dtb_ref, alog_ref, y_ref, s_ref, e_ref):
    c = CHUNK_SSD
    base = SM_DTF if d == 0 else SM_DTB
    ii = lax.broadcasted_iota(jnp.int32, (c, c), 0)
    jj = lax.broadcasted_iota(jnp.int32, (c, c), 1)
    keep = (jj <= ii) if d == 0 else (jj >= ii)
    end_row = c - 1 if d == 0 else 0
    lane = lax.broadcasted_iota(jnp.int32, (1, D_SMALL), 1)
    in_dir = jnp.logical_and(lane >= base, lane < base + H_SSD)
    dt = jnp.where(in_dir, _softplus(ps_ref[...] + dtb_ref[...]), 0.0)
    a = dt * jnp.where(in_dir, -jnp.exp(alog_ref[...]), 0.0)
    g = _dot_sel(keep.astype(F32).astype(BF16), a)
    gt = (g - jnp.where(in_dir, jnp.log(dt), 0.0)).T
    g_end = g[end_row:end_row + 1, :]
    f_state = dt * jnp.exp(g_end - g)
    f_y = jnp.exp(g)
    f_end = jnp.broadcast_to(jnp.exp(g_end), (8, D_SMALL))
    fs_hi, fs_lo = _split2(jnp.concatenate([f_state, f_y, f_end], axis=0))
    fx = _dot(jnp.concatenate([fs_hi, fs_lo], axis=1), e_ref[d])
    wv = (xbc_ref[:, 0:D_INNER].astype(F32) * fx[0:c]).astype(BF16)
    egx = fx[c:2 * c]
    eex = fx[2 * c:2 * c + 1]
    lane2 = lax.broadcasted_iota(jnp.int32, (c, 2 * SSD_HEADDIM), 1)
    hpg = H_SSD // SSD_GROUPS
    gw = hpg * SSD_HEADDIM
    def b_of(grp):
        return xbc_ref[:, D_INNER + grp * D_STATE:D_INNER + (grp + 1) * D_STATE]

    def c_of(grp):
        return xbc_ref[:, D_INNER + N_BC + grp * D_STATE:D_INNER + N_BC + (grp + 1) * D_STATE]

    cbs = [_dot_nt(c_of(grp), b_of(grp)) for grp in range(SSD_GROUPS)]
    yis = [_dot(c_of(grp), s_ref[d, grp].astype(BF16)) for grp in range(SSD_GROUPS)]
    for grp in range(SSD_GROUPS):
        bg = b_of(grp)
        cb = cbs[grp]
        sg = s_ref[d, grp]
        yi = yis[grp]
        for pr in range(hpg // 2):
            h0 = grp * hpg + 2 * pr
            cols = slice(h0 * SSD_HEADDIM, (h0 + 2) * SSD_HEADDIM)
            atts = []
            for hh in (h0, h0 + 1):
                col = base + hh
                diff = g[:, col:col + 1] - gt[col:col + 1, :]
                dec = jnp.exp(jnp.where(keep, diff, -jnp.inf))
                atts.append((cb * dec).astype(BF16))
            v2 = xbc_ref[:, cols]
            vv = jnp.concatenate([jnp.where(lane2 < SSD_HEADDIM, v2, jnp.zeros_like(v2)),
                                  jnp.where(lane2 >= SSD_HEADDIM, v2, jnp.zeros_like(v2))], axis=0)
            y2 = _dot(jnp.concatenate(atts, axis=1), vv)
            yo = yi[:, pr * 2 * SSD_HEADDIM:(pr + 1) * 2 * SSD_HEADDIM] * egx[:, cols] + y2
            y_ref[:, cols] = yo.astype(y_ref.dtype)
        gcols = slice(grp * gw, (grp + 1) * gw)
        s_ref[d, grp] = eex[:, gcols] * sg + _dot_tn(bg, wv[:, gcols])


def _ssd_kernel(xf_ref, psf_ref, xb_ref, psb_ref, dtb_ref, alog_ref, yf_ref, yb_ref, s_ref, e_ref):
    @pl.when(pl.program_id(0) == 0)
    def _():
        s_ref[...] = jnp.zeros(s_ref.shape, F32)
        row = lax.broadcasted_iota(jnp.int32, (2 * D_SMALL, D_INNER), 0) % D_SMALL
        head = lax.broadcasted_iota(jnp.int32, (2 * D_SMALL, D_INNER), 1) // SSD_HEADDIM
        e_ref[0] = (row - SM_DTF == head).astype(F32).astype(BF16)
        e_ref[1] = (row - SM_DTB == head).astype(F32).astype(BF16)

    _ssd_dir(0, xf_ref, psf_ref, dtb_ref, alog_ref, yf_ref, s_ref, e_ref)
    _ssd_dir(1, xb_ref, psb_ref, dtb_ref, alog_ref, yb_ref, s_ref, e_ref)


def _ssd_scan(xbc, psm, dtb_full, alog_full, ncc_rows):
    n = xbc.shape[0]
    c = CHUNK_SSD
    nch = n // c
    ncc = ncc_rows // c
    bwd = lambda t: _bwd_chunk(t, ncc, nch)
    return pl.pallas_call(
        _ssd_kernel,
        grid=(nch,),
        in_specs=[
            pl.BlockSpec((c, D_XBC), lambda t: (t, 0)),
            pl.BlockSpec((c, D_SMALL), lambda t: (t, 0)),
            pl.BlockSpec((c, D_XBC), lambda t: (bwd(t), 0)),
            pl.BlockSpec((c, D_SMALL), lambda t: (bwd(t), 0)),
            pl.BlockSpec((1, D_SMALL), lambda t: (0, 0)),
            pl.BlockSpec((1, D_SMALL), lambda t: (0, 0)),
        ],
        out_specs=[
            pl.BlockSpec((c, D_INNER), lambda t: (t, 0)),
            pl.BlockSpec((c, D_INNER), lambda t: (bwd(t), 0)),
        ],
        out_shape=[jax.ShapeDtypeStruct((n, D_INNER), BF16)] * 2,
        scratch_shapes=[
            pltpu.VMEM((2, SSD_GROUPS, D_STATE, D_INNER // SSD_GROUPS), F32),
            pltpu.VMEM((2, 2 * D_SMALL, D_INNER), BF16),
        ],
        compiler_params=_cparams(("arbitrary",)),
        name="ssd_scan",
    )(xbc, psm, xbc, psm, dtb_full, alog_full)


def _segnorm(y, width):
    outs = [_rms(y[:, s * width:(s + 1) * width]) for s in range(y.shape[1] // width)]
    return jnp.concatenate(outs, axis=1)


def _merge_kernel(x_ref, mod_ref, rg_ref, gr_ref, mg_ref, z_ref, xs_ref,
                  yrf_ref, yrb_ref, ygf_ref, ygb_ref, ysf_ref, ysb_ref,
                  gng_ref, dx_ref, sng_ref, wbr_ref, wbg_ref, wbs_ref, wo_ref, o_ref):
    yr = yrf_ref[...].astype(F32) + yrb_ref[...].astype(F32)
    br = _segnorm(yr, DV_RET) * _silu(rg_ref[...].astype(F32))
    b_ret = _dot(br.astype(BF16), wbr_ref[...])
    yg = ygf_ref[...].astype(F32) + ygb_ref[...].astype(F32)
    bg = _segnorm(yg, DV_GLA) * gng_ref[...] * _silu(gr_ref[...].astype(F32))
    b_gla = _dot(bg.astype(BF16), wbg_ref[...])
    ys = ysf_ref[...].astype(F32) + ysb_ref[...].astype(F32)
    ys = (ys + dx_ref[...] * xs_ref[...].astype(F32)) * _silu(z_ref[...].astype(F32))
    bs = _segnorm(ys, D_INNER // SSD_GROUPS) * sng_ref[...]
    b_ssd = _dot(bs.astype(BF16), wbs_ref[...])
    gates = _sigmoid(mg_ref[...].astype(F32))
    mix = (gates[:, 0:D_MODEL] * b_ret + gates[:, D_MODEL:2 * D_MODEL] * b_gla
           + gates[:, 2 * D_MODEL:3 * D_MODEL] * b_ssd)
    out = _dot(mix.astype(BF16), wo_ref[...])
    o_ref[...] = x_ref[...] + mod_ref[0, 5:6, :] * out


def _merge(xa, mod, p, xbc, yrf, yrb, ygf, ygb, ysf, ysb, gng, dx, sng, wbr, wbg, wbs, wout, l, nct):
    n = xa.shape[0]
    row = lambda w, cb: pl.BlockSpec((TM, w), lambda i: (i, cb))
    return pl.pallas_call(
        _merge_kernel,
        grid=(n // TM,),
        in_specs=[
            row(D_MODEL, 0),
            pl.BlockSpec((None, 1, N_MOD, D_MODEL), lambda i: (l, (i >= nct).astype(jnp.int32), 0, 0)),
            row(D_MODEL, (COL_RET + 2048) // D_MODEL),
            row(D_MODEL, (COL_GLA + 2048) // D_MODEL),
            row(3 * D_MODEL, COL_MERGE // (3 * D_MODEL)),
            row(D_INNER, COL_Z // D_INNER),
            row(D_INNER, 0),
            row(D_MODEL, 0), row(D_MODEL, 0), row(D_MODEL, 0), row(D_MODEL, 0),
            row(D_INNER, 0), row(D_INNER, 0),
            _resident((None, 1, D_MODEL), lambda i: (l, 0, 0)),
            _resident((None, 1, D_INNER), lambda i: (l, 0, 0)),
            _resident((None, 1, D_INNER), lambda i: (l, 0, 0)),
            _resident((None, D_MODEL, D_MODEL), lambda i: (l, 0, 0)),
            _resident((None, D_MODEL, D_MODEL), lambda i: (l, 0, 0)),
            _resident((None, D_INNER, D_MODEL), lambda i: (l, 0, 0)),
            _resident((None, D_MODEL, D_MODEL), lambda i: (l, 0, 0)),
        ],
        out_specs=row(D_MODEL, 0),
        out_shape=jax.ShapeDtypeStruct((n, D_MODEL), F32),
        compiler_params=_cparams(("arbitrary",)),
        name="merge",
    )(xa, mod, p, p, p, p, xbc, yrf, yrb, ygf, ygb, ysf, ysb, gng, dx, sng, wbr, wbg, wbs, wout)


def _rope_tables(n_ctx, n_lat):
    rows = n_lat // GRID_W
    row = jnp.broadcast_to(jnp.arange(rows, dtype=F32)[:, None], (rows, GRID_W)).reshape(n_lat)
    col = jnp.broadcast_to(jnp.arange(GRID_W, dtype=F32)[None, :], (rows, GRID_W)).reshape(n_lat)
    nf = DK_RET // 4
    freq = ROPE_BASE ** (-jnp.arange(nf, dtype=F32) / nf)
    ang = jnp.concatenate([row[:, None] * freq, col[:, None] * freq], axis=-1)
    cos, sin = jnp.cos(ang), jnp.sin(ang)
    cos_t = jnp.concatenate([cos, cos], axis=1)
    sin_t = jnp.concatenate([-sin, sin], axis=1)
    cos_t = jnp.concatenate([jnp.ones((n_ctx, DK_RET), F32), cos_t], axis=0)
    sin_t = jnp.concatenate([jnp.zeros((n_ctx, DK_RET), F32), sin_t], axis=0)
    return cos_t, sin_t


def _split_w_in(w_in):
    o = 0
    sizes = (512, 512, 1024, 1024, 512, 512, 1024, 1024, GLA_RANK, GLA_RANK,
             D_INNER, D_XBC, H_SSD, H_SSD, 3 * D_MODEL)
    parts = []
    for s in sizes:
        parts.append(w_in[..., o:o + s])
        o += s
    (rq, rk, rv, rg, gq, gk, gv, gr, af, ab, sz, xbc, dtf, dtb, mg) = parts
    big = jnp.concatenate([rq, rk, rv, rg, gq, gk, gv, gr, xbc, mg, sz], axis=-1).astype(BF16)
    pad = jnp.zeros(w_in.shape[:-1] + (D_SMALL - 2 * GLA_RANK - 2 * H_SSD,), F32)
    small = jnp.concatenate([af, ab, dtf, dtb, pad], axis=-1)
    return big, small


def kernel(x, c, ctx, c_ctx, ada_w, ada_b, norm_g, final_norm_g, ffn1_wi, ffn1_wo, ffn2_wi, ffn2_wo,
           w_in, ret_logit, gla_wa2, gla_ba, gla_norm_g, conv_w, conv_b, dt_bias, a_log, ssd_d,
           ssd_norm_g, wb_ret, wb_gla, wb_ssd, w_out):
    depth = ada_w.shape[0]
    n_ctx, n_lat = ctx.shape[1], x.shape[1]
    assert x.shape[0] == 1 and n_ctx % TM == 0 and n_lat % TM == 0
    nct = n_ctx // TM
    xa = jnp.concatenate([ctx[0], x[0]], axis=0)

    cond8 = jnp.zeros((8, D_MODEL), F32).at[0].set(c_ctx).at[1].set(c[0])
    mod = _adaln(cond8, ada_w, ada_b).reshape(depth, 8, N_MOD, D_MODEL)

    cos_t, sin_t = _rope_tables(n_ctx, n_lat)
    w_big, w_small = _split_w_in(w_in)
    colscale = jnp.ones((1, D_BIG), F32)
    colscale = colscale.at[:, COL_RET + 512:COL_RET + 1024].set(DK_RET ** -0.5)
    colscale = colscale.at[:, COL_GLA:COL_GLA + 512].set(DK_GLA ** -0.5)

    bf = lambda t: t.astype(BF16)
    ffn1_wi, ffn1_wo, ffn2_wi, ffn2_wo = bf(ffn1_wi), bf(ffn1_wo), bf(ffn2_wi), bf(ffn2_wo)
    wb_ret, wb_gla, wb_ssd, w_out = bf(wb_ret), bf(wb_gla), bf(wb_ssd), bf(w_out)

    logit8 = jnp.broadcast_to(ret_logit.reshape(depth, 2 * H_RET, 1), (depth, 2 * H_RET, 128))
    wa_pad = jnp.zeros((depth, 2, D_SMALL, H_GLA * DK_GLA), F32)
    wa_pad = wa_pad.at[:, 0, SM_AF:SM_AF + GLA_RANK].set(gla_wa2[:, 0])
    wa_pad = wa_pad.at[:, 1, SM_AB:SM_AB + GLA_RANK].set(gla_wa2[:, 1])
    wa_hi = wa_pad.astype(BF16)
    wa_lo = (wa_pad - wa_hi.astype(F32)).astype(BF16)
    ws_hi = w_small.astype(BF16)
    ws_lo = (w_small - ws_hi.astype(F32)).astype(BF16)
    ba = gla_ba.reshape(depth, 2, 1, H_GLA * DK_GLA)
    zpad = jnp.zeros((depth, SM_DTF), F32)
    dtb_full = jnp.concatenate([zpad, dt_bias[:, 0], dt_bias[:, 1], zpad], axis=1).reshape(depth, 1, D_SMALL)
    alog_full = jnp.concatenate([zpad, a_log[:, 0], a_log[:, 1], zpad], axis=1).reshape(depth, 1, D_SMALL)
    conv_w8 = jnp.concatenate([conv_w, jnp.zeros((depth, 8 - CONV_K, D_XBC), F32)], axis=1)
    conv_b3 = conv_b.reshape(depth, 1, D_XBC)
    gng = jnp.tile(gla_norm_g, (1, H_GLA)).reshape(depth, 1, H_GLA * DV_GLA)
    dx = jnp.repeat(ssd_d, SSD_HEADDIM, axis=1).reshape(depth, 1, D_INNER)
    sng = ssd_norm_g.reshape(depth, 1, D_INNER)

    fg = final_norm_g.reshape(1, D_MODEL)
    for l in range(depth):
        xa = _ffn(xa, mod, norm_g, ffn1_wi, ffn1_wo, fg, l, 0, 0, nct)
        p, psm = _inproj(xa, mod, norm_g, cos_t, sin_t, w_big, colscale, ws_hi, ws_lo, l, nct)
        xbc = _conv(p, conv_w8, conv_b3, l, nct)
        yrf, yrb = _ret_scan(p, logit8[l], n_ctx)
        ygf, ygb = _gla_scan(p, psm, wa_hi[l], wa_lo[l], ba[l], n_ctx)
        ysf, ysb = _ssd_scan(xbc, psm, dtb_full[l], alog_full[l], n_ctx)
        xa = _merge(xa, mod, p, xbc, yrf, yrb, ygf, ygb, ysf, ysb, gng, dx, sng,
                    wb_ret, wb_gla, wb_ssd, w_out, l, nct)
        xa = _ffn(xa, mod, norm_g, ffn2_wi, ffn2_wo, fg, l, 6, 2, nct, final=(l == depth - 1))

    return xa[None]
```

```python
import functools
import math

import jax
import jax.numpy as jnp
from jax import lax
from jax.experimental import pallas as pl
from jax.experimental.pallas import tpu as pltpu

F32 = jnp.float32
BF16 = jnp.bfloat16
HIGHEST = lax.Precision.HIGHEST

D_MODEL = 1024
GRID_W = 64
EPS = 1e-6
N_MOD = 9
D_FF = 2816
H_RET, DK_RET, DV_RET, CHUNK_RET = 4, 128, 256, 128
ROPE_BASE = 10000.0
H_GLA, DK_GLA, DV_GLA, GLA_RANK, CHUNK_GLA = 4, 128, 256, 16, 64
GLA_NORMALIZER = 16.0
GLA_SUB = 16
D_INNER = 2 * D_MODEL
SSD_HEADDIM = 64
H_SSD = D_INNER // SSD_HEADDIM
SSD_GROUPS = 4
D_STATE = 128
CONV_K = 5
CHUNK_SSD = 128
N_BC = SSD_GROUPS * D_STATE
D_XBC = D_INNER + 2 * N_BC

COL_RET = 0
COL_GLA = 3072
COL_XBC = 6144
COL_MERGE = 9216
COL_Z = 12288
D_BIG = 14336
SM_AF, SM_AB, SM_DTF, SM_DTB, D_SMALL = 0, 16, 32, 64, 128

TM = 256
CW = 1024
FF_SPLITS = (0, 1536, D_FF)
VMEM_LIMIT = 56 * 1024 * 1024


def _cparams(sem):
    return pltpu.CompilerParams(dimension_semantics=sem, vmem_limit_bytes=VMEM_LIMIT)


def _resident(shape, index_map):
    return pl.BlockSpec(shape, index_map, pipeline_mode=pl.Buffered(1))


def _sigmoid(x):
    return 0.5 * jnp.tanh(0.5 * x) + 0.5


def _silu(x):
    return x * _sigmoid(x)


def _silu_of_half(h):
    return h * jnp.tanh(h) + h


def _sigmoid_of_half(h):
    return 0.5 * jnp.tanh(h) + 0.5


def _softplus(x):
    return jnp.maximum(x, 0.0) + jnp.log(1.0 + jnp.exp(-jnp.abs(x)))


def _log_sigmoid(x, precise=False):
    e = jnp.exp(-jnp.abs(x))
    return jnp.minimum(x, 0.0) - (jnp.log1p(e) if precise else jnp.log(1.0 + e))


def _rms(x):
    return x * lax.rsqrt(jnp.mean(x * x, axis=-1, keepdims=True) + EPS)


def _pre(x, g, shift, scale):
    return _rms(x) * g * (1.0 + scale) + shift


def _dot(a, b):
    return jnp.dot(a, b, preferred_element_type=F32)


def _dot_exact(a, b):
    return jnp.dot(a, b, precision=HIGHEST, preferred_element_type=F32)


def _split2(x):
    hi = x.astype(BF16)
    return hi, (x - hi.astype(F32)).astype(BF16)


def _split3(x):
    hi = x.astype(BF16)
    r = x - hi.astype(F32)
    mid = r.astype(BF16)
    return hi, mid, (r - mid.astype(F32)).astype(BF16)


def _dot_sel(m, x):
    hi, mid, lo = _split3(x)
    return _dot(m, hi) + _dot(m, mid) + _dot(m, lo)


def _dot_split(x, w_hi, w_lo):
    x_hi, x_lo = _split2(x)
    return _dot(x_hi, w_hi) + _dot(x_lo, w_hi) + _dot(x_hi, w_lo)


def _dot_nt(a, b):
    return lax.dot_general(a, b, (((1,), (1,)), ((), ())), preferred_element_type=F32)


def _dot_tn(a, b):
    return lax.dot_general(a, b, (((0,), (0,)), ((), ())), preferred_element_type=F32)


def _adaln_kernel(cond_ref, w_ref, b_ref, o_ref):
    cnd = cond_ref[...]
    o_ref[...] = _dot_exact(_silu(cnd), w_ref[...]) + b_ref[...]


def _adaln(cond8, ada_w, ada_b):
    depth = ada_w.shape[0]
    nblk = N_MOD
    return pl.pallas_call(
        _adaln_kernel,
        grid=(depth, nblk),
        in_specs=[
            pl.BlockSpec((8, D_MODEL), lambda l, j: (0, 0)),
            pl.BlockSpec((None, D_MODEL, D_MODEL), lambda l, j: (l, 0, j)),
            pl.BlockSpec((None, 1, D_MODEL), lambda l, j: (l, 0, j)),
        ],
        out_specs=pl.BlockSpec((None, 8, D_MODEL), lambda l, j: (l, 0, j)),
        out_shape=jax.ShapeDtypeStruct((depth, 8, N_MOD * D_MODEL), F32),
        compiler_params=_cparams(("arbitrary", "arbitrary")),
        name="adaln",
    )(cond8, ada_w, ada_b.reshape(depth, 1, N_MOD * D_MODEL))


def _ffn_kernel(x_ref, mod_ref, g_ref, wa_ref, wu_ref, wo_ref, fg_ref, o_ref, *, k0, gi, final):
    x = x_ref[...]
    h = _pre(x, g_ref[gi:gi + 1, :], mod_ref[0, k0:k0 + 1, :], mod_ref[0, k0 + 1:k0 + 2, :])
    hb = h.astype(BF16)
    out = None
    for lo, hi in zip(FF_SPLITS[:-1], FF_SPLITS[1:]):
        sl = slice(lo, hi)
        a = _dot(hb, wa_ref[:, sl])
        u = _dot(hb, wu_ref[:, sl])
        t = (_silu(a) * u).astype(BF16)
        part = _dot(t, wo_ref[sl, :])
        out = part if out is None else out + part
    y = x + 0.5 * mod_ref[0, k0 + 2:k0 + 3, :] * out
    o_ref[...] = _rms(y) * fg_ref[...] if final else y


def _ffn(xa, mod, norm_g, wi, wo, final_g, l, k0, gi, nct, final=False):
    n = xa.shape[0]
    skip = nct if final else 0
    ntiles = n // TM - skip
    return pl.pallas_call(
        functools.partial(_ffn_kernel, k0=k0, gi=gi, final=final),
        grid=(ntiles,),
        in_specs=[
            pl.BlockSpec((TM, D_MODEL), lambda i: (i + skip, 0)),
            pl.BlockSpec((None, 1, N_MOD, D_MODEL),
                         lambda i: (l, (i + skip >= nct).astype(jnp.int32), 0, 0)),
            _resident((None, 3, D_MODEL), lambda i: (l, 0, 0)),
            _resident((None, D_MODEL, D_FF), lambda i: (l, 0, 0)),
            _resident((None, D_MODEL, D_FF), lambda i: (l, 0, 1)),
            _resident((None, D_FF, D_MODEL), lambda i: (l, 0, 0)),
            _resident((1, D_MODEL), lambda i: (0, 0)),
        ],
        out_specs=pl.BlockSpec((TM, D_MODEL), lambda i: (i, 0)),
        out_shape=jax.ShapeDtypeStruct((ntiles * TM, D_MODEL), F32),
        compiler_params=_cparams(("arbitrary",)),
        name="ffn",
    )(xa, mod, norm_g, wi, wi, wo, final_g)


def _inproj_kernel(x_ref, mod_ref, g_ref, cos_ref, sin_ref, w_ref, cs_ref, wsh_ref, wsl_ref,
                   p_ref, ps_ref):
    h = _pre(x_ref[...], g_ref[1:2, :], mod_ref[0, 3:4, :], mod_ref[0, 4:5, :])
    hb = h.astype(BF16)
    ps_ref[...] = _dot(hb, wsh_ref[...]) + _dot((h - hb.astype(F32)).astype(BF16), wsh_ref[...]) \
        + _dot(hb, wsl_ref[...])
    for j in range(D_BIG // CW):
        cols = slice(j * CW, (j + 1) * CW)
        acc = _dot(hb, w_ref[:, cols]) * cs_ref[:, cols]
        if j == 0:
            cos = cos_ref[...]
            sin = sin_ref[...]
            for b in range(CW // DK_RET):
                t = acc[:, b * DK_RET:(b + 1) * DK_RET]
                p_ref[:, b * DK_RET:(b + 1) * DK_RET] = (
                    t * cos + pltpu.roll(t, DK_RET // 2, 1) * sin).astype(BF16)
        else:
            p_ref[:, cols] = acc.astype(BF16)


def _inproj(xa, mod, norm_g, cos_t, sin_t, w_big, colscale, ws_hi, ws_lo, l, nct):
    n = xa.shape[0]
    return pl.pallas_call(
        _inproj_kernel,
        grid=(n // TM,),
        in_specs=[
            pl.BlockSpec((TM, D_MODEL), lambda i: (i, 0)),
            pl.BlockSpec((None, 1, N_MOD, D_MODEL), lambda i: (l, (i >= nct).astype(jnp.int32), 0, 0)),
            _resident((None, 3, D_MODEL), lambda i: (l, 0, 0)),
            pl.BlockSpec((TM, DK_RET), lambda i: (i, 0)),
            pl.BlockSpec((TM, DK_RET), lambda i: (i, 0)),
            _resident((None, D_MODEL, D_BIG), lambda i: (l, 0, 0)),
            _resident((1, D_BIG), lambda i: (0, 0)),
            _resident((None, D_MODEL, D_SMALL), lambda i: (l, 0, 0)),
            _resident((None, D_MODEL, D_SMALL), lambda i: (l, 0, 0)),
        ],
        out_specs=[
            pl.BlockSpec((TM, D_BIG), lambda i: (i, 0)),
            pl.BlockSpec((TM, D_SMALL), lambda i: (i, 0)),
        ],
        out_shape=[
            jax.ShapeDtypeStruct((n, D_BIG), BF16),
            jax.ShapeDtypeStruct((n, D_SMALL), F32),
        ],
        compiler_params=_cparams(("arbitrary",)),
        name="inproj",
    )(xa, mod, norm_g, cos_t, sin_t, w_big, colscale, ws_hi, ws_lo)


CONV_STRIP = 512
CONV_HALO = 16
CONV_BLK = 128


def _conv_kernel(cur_ref, prev_ref, next_ref, w_ref, b_ref, o_ref, ext_ref, *, nct, ntiles):
    i = pl.program_id(0)
    has_prev = jnp.logical_and(i != 0, i != nct)
    has_next = jnp.logical_and(i != nct - 1, i != ntiles - 1)
    halo0 = jnp.zeros((CONV_HALO, D_XBC), BF16)
    ext_ref[0:CONV_HALO, :] = jnp.where(has_prev, prev_ref[...], halo0)
    ext_ref[CONV_HALO:CONV_HALO + TM, :] = cur_ref[...]
    ext_ref[CONV_HALO + TM:2 * CONV_HALO + TM, :] = jnp.where(has_next, next_ref[...], halo0)
    kwin = CONV_BLK + 2 * CONV_HALO
    rr = lax.broadcasted_iota(jnp.int32, (CONV_BLK, kwin), 0)
    cc = lax.broadcasted_iota(jnp.int32, (CONV_BLK, kwin), 1)
    mid = CONV_K // 2
    shift = {k: (cc == rr + CONV_HALO + k - mid).astype(F32).astype(BF16)
             for k in range(CONV_K) if k != mid}
    for blk in range(TM // CONV_BLK):
        rows = slice(blk * CONV_BLK, (blk + 1) * CONV_BLK)
        for s in range(D_XBC // CONV_STRIP):
            cs = slice(s * CONV_STRIP, (s + 1) * CONV_STRIP)
            win = ext_ref[blk * CONV_BLK:blk * CONV_BLK + kwin, cs]
            acc = b_ref[:, cs] + w_ref[mid:mid + 1, cs] * cur_ref[rows, cs].astype(F32)
            for k in range(CONV_K):
                if k != mid:
                    acc = acc + w_ref[k:k + 1, cs] * _dot(shift[k], win)
            o_ref[rows, cs] = _silu(acc).astype(BF16)


def _conv(p, conv_w, conv_b, l, nct):
    n = p.shape[0]
    ntiles = n // TM
    rh = TM // CONV_HALO
    nblkh = n // CONV_HALO
    cb = COL_XBC // D_XBC
    return pl.pallas_call(
        functools.partial(_conv_kernel, nct=nct, ntiles=ntiles),
        grid=(ntiles,),
        in_specs=[
            pl.BlockSpec((TM, D_XBC), lambda i: (i, cb)),
            pl.BlockSpec((CONV_HALO, D_XBC), lambda i: (jnp.maximum(i * rh - 1, 0), cb)),
            pl.BlockSpec((CONV_HALO, D_XBC), lambda i: (jnp.minimum((i + 1) * rh, nblkh - 1), cb)),
            pl.BlockSpec((None, 8, D_XBC), lambda i: (l, 0, 0)),
            pl.BlockSpec((None, 1, D_XBC), lambda i: (l, 0, 0)),
        ],
        out_specs=pl.BlockSpec((TM, D_XBC), lambda i: (i, 0)),
        out_shape=jax.ShapeDtypeStruct((n, D_XBC), BF16),
        scratch_shapes=[pltpu.VMEM((TM + 2 * CONV_HALO, D_XBC), BF16)],
        compiler_params=_cparams(("arbitrary",)),
        name="conv",
    )(p, p, p, conv_w, conv_b)


def _bwd_chunk(t, ncc, n):
    return jnp.where(t < ncc, ncc - 1 - t, n + ncc - 1 - t)


def _ret_init(lg_ref, s_ref, dm_ref, rs_ref, ks_ref, cd_ref):
    c = CHUNK_RET
    s_ref[...] = jnp.zeros(s_ref.shape, F32)
    la = _log_sigmoid(lg_ref[...], precise=True)
    ii = lax.broadcasted_iota(jnp.int32, (c, c), 0)
    jj = lax.broadcasted_iota(jnp.int32, (c, c), 1)
    ir = lax.broadcasted_iota(jnp.int32, (c, 1), 0)
    for d in range(2):
        for h in range(H_RET):
            a = la[d * H_RET + h:d * H_RET + h + 1, :]
            a2 = jnp.concatenate([a, a], axis=1)
            if d == 0:
                dist, keep = ii - jj, ii >= jj
                rpow = ir + 1
                kpow = c - 1 - ir
            else:
                dist, keep = jj - ii, jj >= ii
                rpow = c - ir
                kpow = ir
            dm_ref[d, h] = jnp.where(keep, jnp.exp(dist.astype(F32) * a), 0.0)
            rs_ref[d, h] = jnp.exp(rpow.astype(F32) * a2)
            ks_ref[d, h] = jnp.exp(kpow.astype(F32) * a)
            cd_ref[d, h] = jnp.exp(float(c) * jnp.broadcast_to(a2, (8, DV_RET)))


def _ret_phases(qkf_ref, vf_ref, yf_ref, qkb_ref, vb_ref, yb_ref, s_ref, dm_ref, rs_ref, ks_ref, cd_ref):
    dirs = ((qkf_ref, vf_ref, yf_ref), (qkb_ref, vb_ref, yb_ref))

    def q_of(qk_ref, h):
        return qk_ref[:, h * DK_RET:(h + 1) * DK_RET]

    def k_of(qk_ref, h):
        return qk_ref[:, H_RET * DK_RET + h * DK_RET:H_RET * DK_RET + (h + 1) * DK_RET]

    qk = [[_dot_nt(q_of(qk_ref, h), k_of(qk_ref, h)) for h in range(H_RET)] for qk_ref, _, _ in dirs]
    qs = [[_dot(q_of(qk_ref, h), s_ref[d, h].astype(BF16)) for h in range(H_RET)]
          for d, (qk_ref, _, _) in enumerate(dirs)]
    def outputs():
        for d, (qk_ref, v_ref, y_ref) in enumerate(dirs):
            for h in range(H_RET):
                v = v_ref[:, h * DV_RET:(h + 1) * DV_RET]
                att = (qk[d][h] * dm_ref[d, h]).astype(BF16)
                y = rs_ref[d, h] * qs[d][h] + _dot(att, v)
                y_ref[:, h * DV_RET:(h + 1) * DV_RET] = y.astype(y_ref.dtype)

    def states():
        for d, (qk_ref, v_ref, y_ref) in enumerate(dirs):
            for h in range(H_RET):
                v = v_ref[:, h * DV_RET:(h + 1) * DV_RET]
                kt = (k_of(qk_ref, h).astype(F32) * ks_ref[d, h]).astype(BF16)
                s_ref[d, h] = cd_ref[d, h][0:1, :] * s_ref[d, h] + _dot_tn(kt, v)

    return outputs, states


GLA_BLOCK = 128


def _gla_dir(d, qk_ref, v_ref, ps_ref, wah_ref, wal_ref, ba_ref, y_ref, st_ref):
    c = CHUNK_GLA
    nb = GLA_BLOCK
    nsub = c // GLA_SUB
    bi = lax.broadcasted_iota(jnp.int32, (nb, nb), 0)
    bj = lax.broadcasted_iota(jnp.int32, (nb, nb), 1)
    same = (bi // c) == (bj // c)
    sub_start = jnp.bitwise_and(bi, -GLA_SUB)
    if d == 0:
        cum = jnp.logical_and(same, bj <= bi)
        ref_m = jnp.logical_and(same, bj < sub_start)
    else:
        cum = jnp.logical_and(same, bj >= bi)
        ref_m = jnp.logical_and(same, bj >= sub_start + GLA_SUB)
    tri = jnp.concatenate([cum, ref_m], axis=0).astype(F32).astype(BF16)
    z = _dot_split(ps_ref[...], wah_ref[d], wal_ref[d]) + ba_ref[d]
    a = _log_sigmoid(z) * (1.0 / GLA_NORMALIZER)
    gr = _dot_sel(tri, a)
    ii = lax.broadcasted_iota(jnp.int32, (c, c), 0)
    jj = lax.broadcasted_iota(jnp.int32, (c, c), 1)
    causal = (jj <= ii) if d == 0 else (jj >= ii)
    end_row = c - 1 if d == 0 else 0
    col_blk = jnp.right_shift(jj, int(math.log2(GLA_SUB)))
    order = list(range(nb // c)) if d == 0 else list(range(nb // c - 1, -1, -1))
    pre = {}
    for ci in order:
        rows = slice(ci * c, (ci + 1) * c)
        g_all = gr[ci * c:(ci + 1) * c]
        r_all = gr[nb + ci * c:nb + (ci + 1) * c]
        for h in range(H_GLA):
            ks = slice(h * DK_GLA, (h + 1) * DK_GLA)
            q = qk_ref[rows, h * DK_GLA:(h + 1) * DK_GLA].astype(F32)
            k = qk_ref[rows, H_GLA * DK_GLA + h * DK_GLA:H_GLA * DK_GLA + (h + 1) * DK_GLA].astype(F32)
            g = g_all[:, ks]
            r = r_all[:, ks]
            g_end = g[end_row:end_row + 1, :]
            kt = (k * jnp.exp(r - g)).astype(BF16)
            qs = []
            for jb in range(nsub):
                rj = r[jb * GLA_SUB:jb * GLA_SUB + 1, :]
                qs.append((q * jnp.exp(jnp.minimum(g - rj, 0.0))).astype(BF16))
            pm = _dot_nt(jnp.concatenate(qs, axis=0), kt)
            qg = (q * jnp.exp(g)).astype(BF16)
            ke = (k * jnp.exp(g_end - g)).astype(BF16)
            pre[(ci, h)] = (pm, qg, ke, jnp.exp(g_end))

    def finish(ci, h):
        rows = slice(ci * c, (ci + 1) * c)
        pm, qg, ke, e_end = pre[(ci, h)]
        v = v_ref[rows, h * DV_GLA:(h + 1) * DV_GLA]
        att = jnp.zeros((c, c), F32)
        for jb in range(nsub):
            att = jnp.where(col_blk == jb, pm[jb * c:(jb + 1) * c], att)
        att = jnp.where(causal, att, 0.0).astype(BF16)
        st = st_ref[d, h]
        y = _dot_nt(qg, st.astype(BF16)) + _dot(att, v)
        y_ref[rows, h * DV_GLA:(h + 1) * DV_GLA] = y.astype(y_ref.dtype)
        st_ref[d, h] = e_end * st + _dot_tn(v, ke)

    return [[functools.partial(finish, ci, h) for h in range(H_GLA)] for ci in order]


def _ssd_dir(d, xbc_ref, ps_ref, dtb_ref, alog_ref, y_ref, s_ref, e_ref):
    c = CHUNK_SSD
    base = SM_DTF if d == 0 else SM_DTB
    ii = lax.broadcasted_iota(jnp.int32, (c, c), 0)
    jj = lax.broadcasted_iota(jnp.int32, (c, c), 1)
    keep = (jj <= ii) if d == 0 else (jj >= ii)
    end_row = c - 1 if d == 0 else 0
    lane = lax.broadcasted_iota(jnp.int32, (1, D_SMALL), 1)
    in_dir = jnp.logical_and(lane >= base, lane < base + H_SSD)
    dt = jnp.where(in_dir, _softplus(ps_ref[...] + dtb_ref[...]), 0.0)
    a = dt * jnp.where(in_dir, -jnp.exp(alog_ref[...]), 0.0)
    g = _dot_sel(keep.astype(F32).astype(BF16), a)
    gt = (g - jnp.where(in_dir, jnp.log(dt), 0.0)).T
    g_end = g[end_row:end_row + 1, :]
    f_state = dt * jnp.exp(g_end - g)
    f_y = jnp.exp(g)
    f_end = jnp.broadcast_to(jnp.exp(g_end), (8, D_SMALL))
    fs_hi, fs_lo = _split2(jnp.concatenate([f_state, f_y, f_end], axis=0))
    fx = _dot(jnp.concatenate([fs_hi, fs_lo], axis=1), e_ref[d])
    wv = (xbc_ref[:, 0:D_INNER].astype(F32) * fx[0:c]).astype(BF16)
    egx = fx[c:2 * c]
    eex = fx[2 * c:2 * c + 1]
    lane2 = lax.broadcasted_iota(jnp.int32, (c, 2 * SSD_HEADDIM), 1)
    hpg = H_SSD // SSD_GROUPS
    gw = hpg * SSD_HEADDIM
    def b_of(grp):
        return xbc_ref[:, D_INNER + grp * D_STATE:D_INNER + (grp + 1) * D_STATE]

    def c_of(grp):
        return xbc_ref[:, D_INNER + N_BC + grp * D_STATE:D_INNER + N_BC + (grp + 1) * D_STATE]

    cbs = [_dot_nt(c_of(grp), b_of(grp)) for grp in range(SSD_GROUPS)]
    yis = [_dot(c_of(grp), s_ref[d, grp].astype(BF16)) for grp in range(SSD_GROUPS)]

    def group(grp):
        bg = b_of(grp)
        cb = cbs[grp]
        sg = s_ref[d, grp]
        yi = yis[grp]
        for pr in range(hpg // 2):
            h0 = grp * hpg + 2 * pr
            cols = slice(h0 * SSD_HEADDIM, (h0 + 2) * SSD_HEADDIM)
            atts = []
            for hh in (h0, h0 + 1):
                col = base + hh
                diff = g[:, col:col + 1] - gt[col:col + 1, :]
                dec = jnp.exp(jnp.where(keep, diff, -jnp.inf))
                atts.append((cb * dec).astype(BF16))
            v2 = xbc_ref[:, cols]
            vv = jnp.concatenate([jnp.where(lane2 < SSD_HEADDIM, v2, jnp.zeros_like(v2)),
                                  jnp.where(lane2 >= SSD_HEADDIM, v2, jnp.zeros_like(v2))], axis=0)
            y2 = _dot(jnp.concatenate(atts, axis=1), vv)
            yo = yi[:, pr * 2 * SSD_HEADDIM:(pr + 1) * 2 * SSD_HEADDIM] * egx[:, cols] + y2
            y_ref[:, cols] = yo.astype(y_ref.dtype)
        gcols = slice(grp * gw, (grp + 1) * gw)
        s_ref[d, grp] = eex[:, gcols] * sg + _dot_tn(bg, wv[:, gcols])

    return [functools.partial(group, grp) for grp in range(SSD_GROUPS)]


def _ssd_init(s_ref, e_ref):
    s_ref[...] = jnp.zeros(s_ref.shape, F32)
    row = lax.broadcasted_iota(jnp.int32, (2 * D_SMALL, D_INNER), 0) % D_SMALL
    head = lax.broadcasted_iota(jnp.int32, (2 * D_SMALL, D_INNER), 1) // SSD_HEADDIM
    e_ref[0] = (row - SM_DTF == head).astype(F32).astype(BF16)
    e_ref[1] = (row - SM_DTB == head).astype(F32).astype(BF16)


def _mix_kernel(lg_ref, wah_ref, wal_ref, ba_ref, dtb_ref, alog_ref,
                rqk_f, rv_f, gqk_f, gv_f, ps_f, xbc_f,
                rqk_b, rv_b, gqk_b, gv_b, ps_b, xbc_b,
                yr_f, yr_b, yg_f, yg_b, ys_f, ys_b,
                rs_s, r_dm, r_rs, r_ks, r_cd, g_st, s_st, s_e):
    @pl.when(pl.program_id(0) == 0)
    def _():
        _ret_init(lg_ref, rs_s, r_dm, r_rs, r_ks, r_cd)
        g_st[...] = jnp.zeros(g_st.shape, F32)
        _ssd_init(s_st, s_e)

    ret_out, ret_state = _ret_phases(rqk_f, rv_f, yr_f, rqk_b, rv_b, yr_b, rs_s, r_dm, r_rs, r_ks, r_cd)
    gla_f = _gla_dir(0, gqk_f, gv_f, ps_f, wah_ref, wal_ref, ba_ref, yg_f, g_st)
    gla_b = _gla_dir(1, gqk_b, gv_b, ps_b, wah_ref, wal_ref, ba_ref, yg_b, g_st)
    ssd_f = _ssd_dir(0, xbc_f, ps_f, dtb_ref, alog_ref, ys_f, s_st, s_e)
    ssd_b = _ssd_dir(1, xbc_b, ps_b, dtb_ref, alog_ref, ys_b, s_st, s_e)
    ret_out()
    for fn in gla_f[0] + gla_b[0] + ssd_f:
        fn()
    ret_state()
    for fn in gla_f[1] + gla_b[1] + ssd_b:
        fn()


def _mix_scan(p, psm, xbc, logit8, wa_hi, wa_lo, ba, dtb_full, alog_full, ncc_rows):
    n = p.shape[0]
    c = CHUNK_RET
    assert c == GLA_BLOCK == CHUNK_SSD
    nch = n // c
    ncc = ncc_rows // c
    bwd = lambda t: _bwd_chunk(t, ncc, nch)
    w = H_RET * DV_RET
    gq = COL_GLA // w
    const = lambda shape: pl.BlockSpec(shape, lambda t: (0,) * len(shape))

    def blocks(idx):
        return [
            pl.BlockSpec((c, w), lambda t: (idx(t), 0)),
            pl.BlockSpec((c, w), lambda t: (idx(t), 1)),
            pl.BlockSpec((c, w), lambda t: (idx(t), gq)),
            pl.BlockSpec((c, w), lambda t: (idx(t), gq + 1)),
            pl.BlockSpec((c, D_SMALL), lambda t: (idx(t), 0)),
            pl.BlockSpec((c, D_XBC), lambda t: (idx(t), 0)),
        ]

    fwd = lambda t: t
    out_w = (w, w, w, w, D_INNER, D_INNER)
    out_idx = (fwd, bwd, fwd, bwd, fwd, bwd)
    return pl.pallas_call(
        _mix_kernel,
        grid=(nch,),
        in_specs=[
            const((8, 128)),
            const((2, D_SMALL, H_GLA * DK_GLA)),
            const((2, D_SMALL, H_GLA * DK_GLA)),
            const((2, 1, H_GLA * DK_GLA)),
            const((1, D_SMALL)),
            const((1, D_SMALL)),
        ] + blocks(fwd) + blocks(bwd),
        out_specs=[pl.BlockSpec((c, ow), functools.partial(lambda t, f: (f(t), 0), f=f))
                   for ow, f in zip(out_w, out_idx)],
        out_shape=[jax.ShapeDtypeStruct((n, ow), BF16) for ow in out_w],
        scratch_shapes=[
            pltpu.VMEM((2, H_RET, DK_RET, DV_RET), F32),
            pltpu.VMEM((2, H_RET, c, c), F32),
            pltpu.VMEM((2, H_RET, c, DV_RET), F32),
            pltpu.VMEM((2, H_RET, c, DK_RET), F32),
            pltpu.VMEM((2, H_RET, 8, DV_RET), F32),
            pltpu.VMEM((2, H_GLA, DV_GLA, DK_GLA), F32),
            pltpu.VMEM((2, SSD_GROUPS, D_STATE, D_INNER // SSD_GROUPS), F32),
            pltpu.VMEM((2, 2 * D_SMALL, D_INNER), BF16),
        ],
        compiler_params=_cparams(("arbitrary",)),
        name="mix_scan",
    )(logit8, wa_hi, wa_lo, ba, dtb_full, alog_full,
      p, p, p, p, psm, xbc, p, p, p, p, psm, xbc)


def _segnorm(y, width):
    outs = [_rms(y[:, s * width:(s + 1) * width]) for s in range(y.shape[1] // width)]
    return jnp.concatenate(outs, axis=1)


def _merge_kernel(x_ref, mod_ref, rg_ref, gr_ref, mg_ref, z_ref, xs_ref,
                  yrf_ref, yrb_ref, ygf_ref, ygb_ref, ysf_ref, ysb_ref,
                  gng_ref, dx_ref, sng_ref, wbr_ref, wbg_ref, wbs_ref, wo_ref, o_ref):
    yr = (yrf_ref[...] + yrb_ref[...]).astype(F32)
    br = _segnorm(yr, DV_RET) * _silu_of_half(rg_ref[...].astype(F32))
    b_ret = _dot(br.astype(BF16), wbr_ref[...])
    yg = (ygf_ref[...] + ygb_ref[...]).astype(F32)
    bg = _segnorm(yg, DV_GLA) * gng_ref[...] * _silu_of_half(gr_ref[...].astype(F32))
    b_gla = _dot(bg.astype(BF16), wbg_ref[...])
    ys = (ysf_ref[...] + ysb_ref[...]).astype(F32)
    ys = (ys + dx_ref[...] * xs_ref[...].astype(F32)) * _silu_of_half(z_ref[...].astype(F32))
    bs = _segnorm(ys, D_INNER // SSD_GROUPS) * sng_ref[...]
    b_ssd = _dot(bs.astype(BF16), wbs_ref[...])
    gates = _sigmoid_of_half(mg_ref[...].astype(F32))
    mix = (gates[:, 0:D_MODEL] * b_ret + gates[:, D_MODEL:2 * D_MODEL] * b_gla
           + gates[:, 2 * D_MODEL:3 * D_MODEL] * b_ssd)
    out = _dot(mix.astype(BF16), wo_ref[...])
    o_ref[...] = x_ref[...] + mod_ref[0, 5:6, :] * out


def _merge(xa, mod, p, xbc, yrf, yrb, ygf, ygb, ysf, ysb, gng, dx, sng, wbr, wbg, wbs, wout, l, nct):
    n = xa.shape[0]
    row = lambda w, cb: pl.BlockSpec((TM, w), lambda i: (i, cb))
    return pl.pallas_call(
        _merge_kernel,
        grid=(n // TM,),
        in_specs=[
            row(D_MODEL, 0),
            pl.BlockSpec((None, 1, N_MOD, D_MODEL), lambda i: (l, (i >= nct).astype(jnp.int32), 0, 0)),
            row(D_MODEL, (COL_RET + 2048) // D_MODEL),
            row(D_MODEL, (COL_GLA + 2048) // D_MODEL),
            row(3 * D_MODEL, COL_MERGE // (3 * D_MODEL)),
            row(D_INNER, COL_Z // D_INNER),
            row(D_INNER, 0),
            row(D_MODEL, 0), row(D_MODEL, 0), row(D_MODEL, 0), row(D_MODEL, 0),
            row(D_INNER, 0), row(D_INNER, 0),
            _resident((None, 1, D_MODEL), lambda i: (l, 0, 0)),
            _resident((None, 1, D_INNER), lambda i: (l, 0, 0)),
            _resident((None, 1, D_INNER), lambda i: (l, 0, 0)),
            _resident((None, D_MODEL, D_MODEL), lambda i: (l, 0, 0)),
            _resident((None, D_MODEL, D_MODEL), lambda i: (l, 0, 0)),
            _resident((None, D_INNER, D_MODEL), lambda i: (l, 0, 0)),
            _resident((None, D_MODEL, D_MODEL), lambda i: (l, 0, 0)),
        ],
        out_specs=row(D_MODEL, 0),
        out_shape=jax.ShapeDtypeStruct((n, D_MODEL), F32),
        compiler_params=_cparams(("arbitrary",)),
        name="merge",
    )(xa, mod, p, p, p, p, xbc, yrf, yrb, ygf, ygb, ysf, ysb, gng, dx, sng, wbr, wbg, wbs, wout)


def _rope_tables(n_ctx, n_lat):
    rows = n_lat // GRID_W
    row = jnp.broadcast_to(jnp.arange(rows, dtype=F32)[:, None], (rows, GRID_W)).reshape(n_lat)
    col = jnp.broadcast_to(jnp.arange(GRID_W, dtype=F32)[None, :], (rows, GRID_W)).reshape(n_lat)
    nf = DK_RET // 4
    freq = ROPE_BASE ** (-jnp.arange(nf, dtype=F32) / nf)
    ang = jnp.concatenate([row[:, None] * freq, col[:, None] * freq], axis=-1)
    cos, sin = jnp.cos(ang), jnp.sin(ang)
    cos_t = jnp.concatenate([cos, cos], axis=1)
    sin_t = jnp.concatenate([-sin, sin], axis=1)
    cos_t = jnp.concatenate([jnp.ones((n_ctx, DK_RET), F32), cos_t], axis=0)
    sin_t = jnp.concatenate([jnp.zeros((n_ctx, DK_RET), F32), sin_t], axis=0)
    return cos_t, sin_t


WP_COLS = 512
W_IN_SRC = ((COL_RET, 0, 3072), (COL_GLA, 3072, 3072), (COL_XBC, 8224, D_XBC),
            (COL_MERGE, 11360, 3 * D_MODEL), (COL_Z, 6176, D_INNER))
W_IN_SMALL_SRC = ((6144, 2 * GLA_RANK), (11296, 2 * H_SSD))


def _wprep_kernel(src_ref, wt_ref, o_ref):
    o_ref[...] = wt_ref[0].T.astype(BF16)


def _wsmall_kernel(a_ref, d_ref, o_ref):
    used = a_ref.shape[1] + d_ref.shape[1]
    rows = jnp.concatenate([a_ref[0], d_ref[0], jnp.zeros((D_SMALL - used, D_MODEL), F32)], axis=0)
    o_ref[...] = rows.T


def _wprep(w_in):
    depth = w_in.shape[0]
    wt = jnp.swapaxes(w_in, 1, 2)
    src = []
    for dst, start, width in W_IN_SRC:
        assert dst == len(src) * WP_COLS and width % WP_COLS == 0 and start % 8 == 0
        src.extend(r // 8 for r in range(start, start + width, WP_COLS))
    big = pl.pallas_call(
        _wprep_kernel,
        grid_spec=pltpu.PrefetchScalarGridSpec(
            num_scalar_prefetch=1,
            grid=(depth, D_BIG // WP_COLS),
            in_specs=[pl.BlockSpec((pl.Element(1), pl.Element(WP_COLS), pl.Element(D_MODEL)),
                                   lambda l, j, tab: (l, tab[j] * 8, 0))],
            out_specs=pl.BlockSpec((None, D_MODEL, WP_COLS), lambda l, j, tab: (l, 0, j)),
        ),
        out_shape=jax.ShapeDtypeStruct((depth, D_MODEL, D_BIG), BF16),
        compiler_params=_cparams(("arbitrary", "arbitrary")),
        name="wprep",
    )(jnp.asarray(src, jnp.int32), wt)
    small = pl.pallas_call(
        _wsmall_kernel,
        grid=(depth,),
        in_specs=[pl.BlockSpec((pl.Element(1), pl.Element(n), pl.Element(D_MODEL)),
                               functools.partial(lambda l, s: (l, s, 0), s=s))
                  for s, n in W_IN_SMALL_SRC],
        out_specs=pl.BlockSpec((None, D_MODEL, D_SMALL), lambda l: (l, 0, 0)),
        out_shape=jax.ShapeDtypeStruct((depth, D_MODEL, D_SMALL), F32),
        compiler_params=_cparams(("arbitrary",)),
        name="wsmall",
    )(wt, wt)
    return big, small


def kernel(x, c, ctx, c_ctx, ada_w, ada_b, norm_g, final_norm_g, ffn1_wi, ffn1_wo, ffn2_wi, ffn2_wo,
           w_in, ret_logit, gla_wa2, gla_ba, gla_norm_g, conv_w, conv_b, dt_bias, a_log, ssd_d,
           ssd_norm_g, wb_ret, wb_gla, wb_ssd, w_out):
    depth = ada_w.shape[0]
    n_ctx, n_lat = ctx.shape[1], x.shape[1]
    assert x.shape[0] == 1 and n_ctx % TM == 0 and n_lat % TM == 0
    nct = n_ctx // TM
    xa = jnp.concatenate([ctx[0], x[0]], axis=0)

    cond8 = jnp.zeros((8, D_MODEL), F32).at[0].set(c_ctx).at[1].set(c[0])
    mod = _adaln(cond8, ada_w, ada_b).reshape(depth, 8, N_MOD, D_MODEL)

    cos_t, sin_t = _rope_tables(n_ctx, n_lat)
    w_big, w_small = _wprep(w_in)
    colscale = jnp.ones((1, D_BIG), F32)
    colscale = colscale.at[:, COL_RET + 512:COL_RET + 1024].set(DK_RET ** -0.5)
    colscale = colscale.at[:, COL_GLA:COL_GLA + 512].set(DK_GLA ** -0.5)
    colscale = colscale.at[:, COL_RET + 2048:COL_RET + 3072].set(0.5)
    colscale = colscale.at[:, COL_GLA + 2048:COL_GLA + 3072].set(0.5)
    colscale = colscale.at[:, COL_MERGE:COL_MERGE + 3 * D_MODEL].set(0.5)
    colscale = colscale.at[:, COL_Z:COL_Z + D_INNER].set(0.5)

    bf = lambda t: t.astype(BF16)
    ffn1_wi, ffn1_wo, ffn2_wi, ffn2_wo = bf(ffn1_wi), bf(ffn1_wo), bf(ffn2_wi), bf(ffn2_wo)
    wb_ret, wb_gla, wb_ssd, w_out = bf(wb_ret), bf(wb_gla), bf(wb_ssd), bf(w_out)

    logit8 = jnp.broadcast_to(ret_logit.reshape(depth, 2 * H_RET, 1), (depth, 2 * H_RET, 128))
    wa_pad = jnp.zeros((depth, 2, D_SMALL, H_GLA * DK_GLA), F32)
    wa_pad = wa_pad.at[:, 0, SM_AF:SM_AF + GLA_RANK].set(gla_wa2[:, 0])
    wa_pad = wa_pad.at[:, 1, SM_AB:SM_AB + GLA_RANK].set(gla_wa2[:, 1])
    wa_hi = wa_pad.astype(BF16)
    wa_lo = (wa_pad - wa_hi.astype(F32)).astype(BF16)
    ws_hi = w_small.astype(BF16)
    ws_lo = (w_small - ws_hi.astype(F32)).astype(BF16)
    ba = gla_ba.reshape(depth, 2, 1, H_GLA * DK_GLA)
    zpad = jnp.zeros((depth, SM_DTF), F32)
    dtb_full = jnp.concatenate([zpad, dt_bias[:, 0], dt_bias[:, 1], zpad], axis=1).reshape(depth, 1, D_SMALL)
    alog_full = jnp.concatenate([zpad, a_log[:, 0], a_log[:, 1], zpad], axis=1).reshape(depth, 1, D_SMALL)
    conv_w8 = jnp.concatenate([conv_w, jnp.zeros((depth, 8 - CONV_K, D_XBC), F32)], axis=1)
    conv_b3 = conv_b.reshape(depth, 1, D_XBC)
    gng = jnp.tile(gla_norm_g, (1, H_GLA)).reshape(depth, 1, H_GLA * DV_GLA)
    dx = jnp.repeat(ssd_d, SSD_HEADDIM, axis=1).reshape(depth, 1, D_INNER)
    sng = ssd_norm_g.reshape(depth, 1, D_INNER)

    fg = final_norm_g.reshape(1, D_MODEL)
    for l in range(depth):
        xa = _ffn(xa, mod, norm_g, ffn1_wi, ffn1_wo, fg, l, 0, 0, nct)
        p, psm = _inproj(xa, mod, norm_g, cos_t, sin_t, w_big, colscale, ws_hi, ws_lo, l, nct)
        xbc = _conv(p, conv_w8, conv_b3, l, nct)
        yrf, yrb, ygf, ygb, ysf, ysb = _mix_scan(p, psm, xbc, logit8[l], wa_hi[l], wa_lo[l], ba[l],
                                                 dtb_full[l], alog_full[l], n_ctx)
        xa = _merge(xa, mod, p, xbc, yrf, yrb, ygf, ygb, ysf, ysb, gng, dx, sng,
                    wb_ret, wb_gla, wb_ssd, w_out, l, nct)
        xa = _ffn(xa, mod, norm_g, ffn2_wi, ffn2_wo, fg, l, 6, 2, nct, final=(l == depth - 1))

    return xa[None]
```

```python
import functools
import math

import jax
import jax.numpy as jnp
from jax import lax
from jax.experimental import pallas as pl
from jax.experimental.pallas import tpu as pltpu

F32 = jnp.float32
BF16 = jnp.bfloat16
HIGHEST = lax.Precision.HIGHEST
LOG2E = 1.4426950408889634

D_MODEL = 1024
GRID_W = 64
EPS = 1e-6
N_MOD = 9
D_FF = 2816
H_RET, DK_RET, DV_RET, CHUNK_RET = 4, 128, 256, 128
ROPE_BASE = 10000.0
H_GLA, DK_GLA, DV_GLA, GLA_RANK, CHUNK_GLA = 4, 128, 256, 16, 64
GLA_NORMALIZER = 16.0
GLA_SUB = 16
D_INNER = 2 * D_MODEL
SSD_HEADDIM = 64
H_SSD = D_INNER // SSD_HEADDIM
SSD_GROUPS = 4
D_STATE = 128
CONV_K = 5
CHUNK_SSD = 128
N_BC = SSD_GROUPS * D_STATE
D_XBC = D_INNER + 2 * N_BC

COL_RET = 0
COL_GLA = 3072
COL_XBC = 6144
COL_MERGE = 9216
COL_Z = 12288
D_BIG = 14336
SM_AF, SM_AB, SM_DTF, SM_DTB, D_SMALL = 0, 16, 32, 64, 128

TM = 256
CW = 1024
FF_SPLITS = (0, 1536, D_FF)
VMEM_LIMIT = 56 * 1024 * 1024


def _cparams(sem):
    return pltpu.CompilerParams(dimension_semantics=sem, vmem_limit_bytes=VMEM_LIMIT)


def _resident(shape, index_map):
    return pl.BlockSpec(shape, index_map, pipeline_mode=pl.Buffered(1))


def _sigmoid(x):
    return 0.5 * jnp.tanh(0.5 * x) + 0.5


def _silu(x):
    return x * _sigmoid(x)


def _silu_of_half(h):
    return h * jnp.tanh(h) + h


def _sigmoid_of_half(h):
    return 0.5 * jnp.tanh(h) + 0.5


def _softplus(x):
    return jnp.maximum(x, 0.0) + jnp.log(1.0 + jnp.exp(-jnp.abs(x)))


def _log_sigmoid(x, precise=False):
    e = jnp.exp(-jnp.abs(x))
    return jnp.minimum(x, 0.0) - (jnp.log1p(e) if precise else jnp.log(1.0 + e))


def _rms(x):
    return x * lax.rsqrt(jnp.mean(x * x, axis=-1, keepdims=True) + EPS)


def _pre(x, g, shift, scale):
    return _rms(x) * g * (1.0 + scale) + shift


def _dot(a, b):
    return jnp.dot(a, b, preferred_element_type=F32)


def _dot_exact(a, b):
    return jnp.dot(a, b, precision=HIGHEST, preferred_element_type=F32)


def _split2(x):
    hi = x.astype(BF16)
    return hi, (x - hi.astype(F32)).astype(BF16)


def _split3(x):
    hi = x.astype(BF16)
    r = x - hi.astype(F32)
    mid = r.astype(BF16)
    return hi, mid, (r - mid.astype(F32)).astype(BF16)


def _dot_sel(m, x):
    hi, mid, lo = _split3(x)
    return _dot(m, hi) + _dot(m, mid) + _dot(m, lo)


def _dot_split(x, w_hi, w_lo):
    x_hi, x_lo = _split2(x)
    return _dot(x_hi, w_hi) + _dot(x_lo, w_hi) + _dot(x_hi, w_lo)


def _dot_nt(a, b):
    return lax.dot_general(a, b, (((1,), (1,)), ((), ())), preferred_element_type=F32)


def _dot_tn(a, b):
    return lax.dot_general(a, b, (((0,), (0,)), ((), ())), preferred_element_type=F32)


def _adaln_kernel(cond_ref, w_ref, b_ref, o_ref):
    cnd = cond_ref[...]
    o_ref[...] = _dot_exact(_silu(cnd), w_ref[...]) + b_ref[...]


def _adaln(cond8, ada_w, ada_b):
    depth = ada_w.shape[0]
    nblk = N_MOD
    return pl.pallas_call(
        _adaln_kernel,
        grid=(depth, nblk),
        in_specs=[
            pl.BlockSpec((8, D_MODEL), lambda l, j: (0, 0)),
            pl.BlockSpec((None, D_MODEL, D_MODEL), lambda l, j: (l, 0, j)),
            pl.BlockSpec((None, 1, D_MODEL), lambda l, j: (l, 0, j)),
        ],
        out_specs=pl.BlockSpec((None, 8, D_MODEL), lambda l, j: (l, 0, j)),
        out_shape=jax.ShapeDtypeStruct((depth, 8, N_MOD * D_MODEL), F32),
        compiler_params=_cparams(("arbitrary", "arbitrary")),
        name="adaln",
    )(cond8, ada_w, ada_b.reshape(depth, 1, N_MOD * D_MODEL))


def _ffn_kernel(x_ref, mod_ref, g_ref, wa_ref, wu_ref, wo_ref, fg_ref, o_ref, *, k0, gi, final):
    _ffn_body(x_ref[...], mod_ref, g_ref, wa_ref, wu_ref, wo_ref, fg_ref, o_ref, k0, gi, final)


def _ffn_body(x, mod_ref, g_ref, wa_ref, wu_ref, wo_ref, fg_ref, o_ref, k0, gi, final):
    h = _pre(x, g_ref[gi:gi + 1, :], mod_ref[0, k0:k0 + 1, :], mod_ref[0, k0 + 1:k0 + 2, :])
    hb = h.astype(BF16)
    out = None
    for lo, hi in zip(FF_SPLITS[:-1], FF_SPLITS[1:]):
        sl = slice(lo, hi)
        a = _dot(hb, wa_ref[:, sl])
        u = _dot(hb, wu_ref[:, sl])
        t = (_silu(a) * u).astype(BF16)
        part = _dot(t, wo_ref[sl, :])
        out = part if out is None else out + part
    y = x + 0.5 * mod_ref[0, k0 + 2:k0 + 3, :] * out
    o_ref[...] = _rms(y) * fg_ref[...] if final else y


def _ffn_split_kernel(ctx_ref, lat_ref, mod_ref, g_ref, wa_ref, wu_ref, wo_ref, fg_ref, o_ref,
                      *, k0, gi, nct):
    x = jnp.where(pl.program_id(0) >= nct, lat_ref[...], ctx_ref[...])
    _ffn_body(x, mod_ref, g_ref, wa_ref, wu_ref, wo_ref, fg_ref, o_ref, k0, gi, False)


def _ffn(xs, mod, norm_g, wi, wo, final_g, l, k0, gi, nct):
    split = isinstance(xs, tuple)
    n = xs[0].shape[0] + xs[1].shape[0] if split else xs.shape[0]
    if split:
        kern = functools.partial(_ffn_split_kernel, k0=k0, gi=gi, nct=nct)
        x_specs = [pl.BlockSpec((TM, D_MODEL), lambda i: (jnp.minimum(i, nct - 1), 0)),
                   pl.BlockSpec((TM, D_MODEL), lambda i: (jnp.maximum(i - nct, 0), 0))]
        x_args = list(xs)
    else:
        kern = functools.partial(_ffn_kernel, k0=k0, gi=gi, final=False)
        x_specs = [pl.BlockSpec((TM, D_MODEL), lambda i: (i, 0))]
        x_args = [xs]
    return pl.pallas_call(
        kern,
        grid=(n // TM,),
        in_specs=x_specs + [
            pl.BlockSpec((None, 1, N_MOD, D_MODEL),
                         lambda i: (l, (i >= nct).astype(jnp.int32), 0, 0)),
            _resident((None, 3, D_MODEL), lambda i: (l, 0, 0)),
            _resident((None, D_MODEL, D_FF), lambda i: (l, 0, 0)),
            _resident((None, D_MODEL, D_FF), lambda i: (l, 0, 1)),
            _resident((None, D_FF, D_MODEL), lambda i: (l, 0, 0)),
            _resident((1, D_MODEL), lambda i: (0, 0)),
        ],
        out_specs=pl.BlockSpec((TM, D_MODEL), lambda i: (i, 0)),
        out_shape=jax.ShapeDtypeStruct((n, D_MODEL), F32),
        compiler_params=_cparams(("arbitrary",)),
        name="ffn",
    )(*x_args, mod, norm_g, wi, wi, wo, final_g)


def _inproj_kernel(x_ref, mod_ref, g_ref, cos_ref, sin_ref, w_ref, cs_ref, wsh_ref, wsl_ref,
                   p_ref, ps_ref):
    h = _pre(x_ref[...], g_ref[1:2, :], mod_ref[0, 3:4, :], mod_ref[0, 4:5, :])
    hb = h.astype(BF16)
    ps_ref[...] = _dot(hb, wsh_ref[...]) + _dot((h - hb.astype(F32)).astype(BF16), wsh_ref[...]) \
        + _dot(hb, wsl_ref[...])
    for j in range(D_BIG // CW):
        cols = slice(j * CW, (j + 1) * CW)
        acc = _dot(hb, w_ref[:, cols]) * cs_ref[:, cols]
        if j == 0:
            cos = cos_ref[...]
            sin = sin_ref[...]
            for b in range(CW // DK_RET):
                t = acc[:, b * DK_RET:(b + 1) * DK_RET]
                p_ref[:, b * DK_RET:(b + 1) * DK_RET] = (
                    t * cos + pltpu.roll(t, DK_RET // 2, 1) * sin).astype(BF16)
        else:
            p_ref[:, cols] = acc.astype(BF16)


def _inproj(xa, mod, norm_g, cos_t, sin_t, w_big, colscale, ws_hi, ws_lo, l, nct):
    n = xa.shape[0]
    return pl.pallas_call(
        _inproj_kernel,
        grid=(n // TM,),
        in_specs=[
            pl.BlockSpec((TM, D_MODEL), lambda i: (i, 0)),
            pl.BlockSpec((None, 1, N_MOD, D_MODEL), lambda i: (l, (i >= nct).astype(jnp.int32), 0, 0)),
            _resident((None, 3, D_MODEL), lambda i: (l, 0, 0)),
            pl.BlockSpec((TM, DK_RET), lambda i: (i, 0)),
            pl.BlockSpec((TM, DK_RET), lambda i: (i, 0)),
            _resident((None, D_MODEL, D_BIG), lambda i: (l, 0, 0)),
            _resident((1, D_BIG), lambda i: (0, 0)),
            _resident((None, D_MODEL, D_SMALL), lambda i: (l, 0, 0)),
            _resident((None, D_MODEL, D_SMALL), lambda i: (l, 0, 0)),
        ],
        out_specs=[
            pl.BlockSpec((TM, D_BIG), lambda i: (i, 0)),
            pl.BlockSpec((TM, D_SMALL), lambda i: (i, 0)),
        ],
        out_shape=[
            jax.ShapeDtypeStruct((n, D_BIG), BF16),
            jax.ShapeDtypeStruct((n, D_SMALL), F32),
        ],
        compiler_params=_cparams(("arbitrary",)),
        name="inproj",
    )(xa, mod, norm_g, cos_t, sin_t, w_big, colscale, ws_hi, ws_lo)


CONV_STRIP = 512
CONV_HALO = 16
CONV_BLK = 128


def _conv_kernel(cur_ref, prev_ref, next_ref, w_ref, b_ref, o_ref, ext_ref, *, nct, ntiles):
    i = pl.program_id(0)
    has_prev = jnp.logical_and(i != 0, i != nct)
    has_next = jnp.logical_and(i != nct - 1, i != ntiles - 1)
    halo0 = jnp.zeros((CONV_HALO, D_XBC), BF16)
    ext_ref[0:CONV_HALO, :] = jnp.where(has_prev, prev_ref[...], halo0)
    ext_ref[CONV_HALO:CONV_HALO + TM, :] = cur_ref[...]
    ext_ref[CONV_HALO + TM:2 * CONV_HALO + TM, :] = jnp.where(has_next, next_ref[...], halo0)
    kwin = CONV_BLK + 2 * CONV_HALO
    rr = lax.broadcasted_iota(jnp.int32, (CONV_BLK, kwin), 0)
    cc = lax.broadcasted_iota(jnp.int32, (CONV_BLK, kwin), 1)
    mid = CONV_K // 2
    shift = {k: (cc == rr + CONV_HALO + k - mid).astype(F32).astype(BF16)
             for k in range(CONV_K) if k != mid}
    for blk in range(TM // CONV_BLK):
        rows = slice(blk * CONV_BLK, (blk + 1) * CONV_BLK)
        for s in range(D_XBC // CONV_STRIP):
            cs = slice(s * CONV_STRIP, (s + 1) * CONV_STRIP)
            win = ext_ref[blk * CONV_BLK:blk * CONV_BLK + kwin, cs]
            acc = b_ref[:, cs] + w_ref[mid:mid + 1, cs] * cur_ref[rows, cs].astype(F32)
            for k in range(CONV_K):
                if k != mid:
                    acc = acc + w_ref[k:k + 1, cs] * _dot(shift[k], win)
            o_ref[rows, cs] = _silu(acc).astype(BF16)


def _conv(p, conv_w, conv_b, l, nct):
    n = p.shape[0]
    ntiles = n // TM
    rh = TM // CONV_HALO
    nblkh = n // CONV_HALO
    cb = COL_XBC // D_XBC
    return pl.pallas_call(
        functools.partial(_conv_kernel, nct=nct, ntiles=ntiles),
        grid=(ntiles,),
        in_specs=[
            pl.BlockSpec((TM, D_XBC), lambda i: (i, cb)),
            pl.BlockSpec((CONV_HALO, D_XBC), lambda i: (jnp.maximum(i * rh - 1, 0), cb)),
            pl.BlockSpec((CONV_HALO, D_XBC), lambda i: (jnp.minimum((i + 1) * rh, nblkh - 1), cb)),
            pl.BlockSpec((None, 8, D_XBC), lambda i: (l, 0, 0)),
            pl.BlockSpec((None, 1, D_XBC), lambda i: (l, 0, 0)),
        ],
        out_specs=pl.BlockSpec((TM, D_XBC), lambda i: (i, 0)),
        out_shape=jax.ShapeDtypeStruct((n, D_XBC), BF16),
        scratch_shapes=[pltpu.VMEM((TM + 2 * CONV_HALO, D_XBC), BF16)],
        compiler_params=_cparams(("arbitrary",)),
        name="conv",
    )(p, p, p, conv_w, conv_b)


def _bwd_chunk(t, ncc, n):
    return jnp.where(t < ncc, ncc - 1 - t, n + ncc - 1 - t)


def _ret_init(lg_ref, s_ref, dm_ref, rs_ref, ks_ref, cd_ref):
    c = CHUNK_RET
    s_ref[...] = jnp.zeros(s_ref.shape, F32)
    la = _log_sigmoid(lg_ref[...], precise=True)
    ii = lax.broadcasted_iota(jnp.int32, (c, c), 0)
    jj = lax.broadcasted_iota(jnp.int32, (c, c), 1)
    ir = lax.broadcasted_iota(jnp.int32, (c, 1), 0)
    for d in range(2):
        for h in range(H_RET):
            a = la[d * H_RET + h:d * H_RET + h + 1, :]
            a2 = jnp.concatenate([a, a], axis=1)
            if d == 0:
                dist, keep = ii - jj, ii >= jj
                rpow = ir + 1
                kpow = c - 1 - ir
            else:
                dist, keep = jj - ii, jj >= ii
                rpow = c - ir
                kpow = ir
            dm_ref[d, h] = jnp.where(keep, jnp.exp(dist.astype(F32) * a), 0.0)
            rs_ref[d, h] = jnp.exp(rpow.astype(F32) * a2)
            ks_ref[d, h] = jnp.exp(kpow.astype(F32) * a)
            cd_ref[d, h] = jnp.exp(float(c) * jnp.broadcast_to(a2, (8, DV_RET)))


def _ret_phases(qkf_ref, vf_ref, yf_ref, qkb_ref, vb_ref, yb_ref, s_ref, dm_ref, rs_ref, ks_ref, cd_ref):
    dirs = ((qkf_ref, vf_ref, yf_ref), (qkb_ref, vb_ref, yb_ref))

    def q_of(qk_ref, h):
        return qk_ref[:, h * DK_RET:(h + 1) * DK_RET]

    def k_of(qk_ref, h):
        return qk_ref[:, H_RET * DK_RET + h * DK_RET:H_RET * DK_RET + (h + 1) * DK_RET]

    qk = [[_dot_nt(q_of(qk_ref, h), k_of(qk_ref, h)) for h in range(H_RET)] for qk_ref, _, _ in dirs]
    qs = [[_dot(q_of(qk_ref, h), s_ref[d, h].astype(BF16)) for h in range(H_RET)]
          for d, (qk_ref, _, _) in enumerate(dirs)]
    def outputs():
        for d, (qk_ref, v_ref, y_ref) in enumerate(dirs):
            for h in range(H_RET):
                v = v_ref[:, h * DV_RET:(h + 1) * DV_RET]
                att = (qk[d][h] * dm_ref[d, h]).astype(BF16)
                y = rs_ref[d, h] * qs[d][h] + _dot(att, v)
                y_ref[:, h * DV_RET:(h + 1) * DV_RET] = y.astype(y_ref.dtype)

    def states():
        for d, (qk_ref, v_ref, y_ref) in enumerate(dirs):
            for h in range(H_RET):
                v = v_ref[:, h * DV_RET:(h + 1) * DV_RET]
                kt = (k_of(qk_ref, h).astype(F32) * ks_ref[d, h]).astype(BF16)
                s_ref[d, h] = cd_ref[d, h][0:1, :] * s_ref[d, h] + _dot_tn(kt, v)

    return outputs, states


GLA_BLOCK = 128


def _gla_dir(d, qk_ref, v_ref, ps_ref, wah_ref, wal_ref, ba_ref, y_ref, st_ref):
    c = CHUNK_GLA
    nb = GLA_BLOCK
    nsub = c // GLA_SUB
    bi = lax.broadcasted_iota(jnp.int32, (nb, nb), 0)
    bj = lax.broadcasted_iota(jnp.int32, (nb, nb), 1)
    same = (bi // c) == (bj // c)
    sub_start = jnp.bitwise_and(bi, -GLA_SUB)
    if d == 0:
        cum = jnp.logical_and(same, bj <= bi)
        ref_m = jnp.logical_and(same, bj < sub_start)
    else:
        cum = jnp.logical_and(same, bj >= bi)
        ref_m = jnp.logical_and(same, bj >= sub_start + GLA_SUB)
    tri = jnp.concatenate([cum, ref_m], axis=0).astype(F32).astype(BF16)
    z = _dot_split(ps_ref[...], wah_ref[d], wal_ref[d]) + ba_ref[d]
    a = _log_sigmoid(z) * (1.0 / GLA_NORMALIZER)
    gr = _dot_sel(tri, a)
    ii = lax.broadcasted_iota(jnp.int32, (c, c), 0)
    jj = lax.broadcasted_iota(jnp.int32, (c, c), 1)
    causal = (jj <= ii) if d == 0 else (jj >= ii)
    end_row = c - 1 if d == 0 else 0
    col_blk = jnp.right_shift(jj, int(math.log2(GLA_SUB)))
    order = list(range(nb // c)) if d == 0 else list(range(nb // c - 1, -1, -1))
    pre = {}
    for ci in order:
        rows = slice(ci * c, (ci + 1) * c)
        g_all = gr[ci * c:(ci + 1) * c]
        r_all = gr[nb + ci * c:nb + (ci + 1) * c]
        for h in range(H_GLA):
            ks = slice(h * DK_GLA, (h + 1) * DK_GLA)
            q = qk_ref[rows, h * DK_GLA:(h + 1) * DK_GLA].astype(F32)
            k = qk_ref[rows, H_GLA * DK_GLA + h * DK_GLA:H_GLA * DK_GLA + (h + 1) * DK_GLA].astype(F32)
            g = g_all[:, ks]
            r = r_all[:, ks]
            g_end = g[end_row:end_row + 1, :]
            kt = (k * jnp.exp(r - g)).astype(BF16)
            qs = []
            for jb in range(nsub):
                rj = r[jb * GLA_SUB:jb * GLA_SUB + 1, :]
                qs.append((q * jnp.exp(jnp.minimum(g - rj, 0.0))).astype(BF16))
            pm = _dot_nt(jnp.concatenate(qs, axis=0), kt)
            qg = (q * jnp.exp(g)).astype(BF16)
            ke = (k * jnp.exp(g_end - g)).astype(BF16)
            pre[(ci, h)] = (pm, qg, ke, jnp.exp(g_end))

    def finish(ci, h):
        rows = slice(ci * c, (ci + 1) * c)
        pm, qg, ke, e_end = pre[(ci, h)]
        v = v_ref[rows, h * DV_GLA:(h + 1) * DV_GLA]
        att = jnp.zeros((c, c), F32)
        for jb in range(nsub):
            att = jnp.where(col_blk == jb, pm[jb * c:(jb + 1) * c], att)
        att = jnp.where(causal, att, 0.0).astype(BF16)
        st = st_ref[d, h]
        y = _dot_nt(qg, st.astype(BF16)) + _dot(att, v)
        y_ref[rows, h * DV_GLA:(h + 1) * DV_GLA] = y.astype(y_ref.dtype)
        st_ref[d, h] = e_end * st + _dot_tn(v, ke)

    return [[functools.partial(finish, ci, h) for h in range(H_GLA)] for ci in order]


def _ssd_dir(d, xbc_ref, ps_ref, dtb_ref, alog_ref, y_ref, s_ref, e_ref):
    c = CHUNK_SSD
    base = SM_DTF if d == 0 else SM_DTB
    ii = lax.broadcasted_iota(jnp.int32, (c, c), 0)
    jj = lax.broadcasted_iota(jnp.int32, (c, c), 1)
    keep = (jj <= ii) if d == 0 else (jj >= ii)
    end_row = c - 1 if d == 0 else 0
    lane = lax.broadcasted_iota(jnp.int32, (1, D_SMALL), 1)
    in_dir = jnp.logical_and(lane >= base, lane < base + H_SSD)
    dt = jnp.where(in_dir, _softplus(ps_ref[...] + dtb_ref[...]), 0.0)
    a = dt * jnp.where(in_dir, -jnp.exp(alog_ref[...]), 0.0)
    g = _dot_sel(keep.astype(F32).astype(BF16), a)
    g2 = g * LOG2E
    gt = ((g - jnp.where(in_dir, jnp.log(dt), 0.0)) * LOG2E).T
    g_end = g[end_row:end_row + 1, :]
    f_state = dt * jnp.exp(g_end - g)
    f_y = jnp.exp(g)
    f_end = jnp.broadcast_to(jnp.exp(g_end), (8, D_SMALL))
    fs_hi, fs_lo = _split2(jnp.concatenate([f_state, f_y, f_end], axis=0))
    fx = _dot(jnp.concatenate([fs_hi, fs_lo], axis=1), e_ref[d])
    wv = (xbc_ref[:, 0:D_INNER].astype(F32) * fx[0:c]).astype(BF16)
    egx = fx[c:2 * c]
    eex = fx[2 * c:2 * c + 1]
    lane2 = lax.broadcasted_iota(jnp.int32, (c, 2 * SSD_HEADDIM), 1)
    hpg = H_SSD // SSD_GROUPS
    gw = hpg * SSD_HEADDIM
    def b_of(grp):
        return xbc_ref[:, D_INNER + grp * D_STATE:D_INNER + (grp + 1) * D_STATE]

    def c_of(grp):
        return xbc_ref[:, D_INNER + N_BC + grp * D_STATE:D_INNER + N_BC + (grp + 1) * D_STATE]

    cbs = [_dot_nt(c_of(grp), b_of(grp)) for grp in range(SSD_GROUPS)]
    yis = [_dot(c_of(grp), s_ref[d, grp].astype(BF16)) for grp in range(SSD_GROUPS)]

    def group(grp):
        bg = b_of(grp)
        cb = cbs[grp]
        sg = s_ref[d, grp]
        yi = yis[grp]
        for pr in range(hpg // 2):
            h0 = grp * hpg + 2 * pr
            cols = slice(h0 * SSD_HEADDIM, (h0 + 2) * SSD_HEADDIM)
            atts = []
            for hh in (h0, h0 + 1):
                col = base + hh
                diff = g2[:, col:col + 1] - gt[col:col + 1, :]
                dec = jnp.exp2(jnp.where(keep, diff, -jnp.inf))
                atts.append((cb * dec).astype(BF16))
            v2 = xbc_ref[:, cols]
            vv = jnp.concatenate([jnp.where(lane2 < SSD_HEADDIM, v2, jnp.zeros_like(v2)),
                                  jnp.where(lane2 >= SSD_HEADDIM, v2, jnp.zeros_like(v2))], axis=0)
            y2 = _dot(jnp.concatenate(atts, axis=1), vv)
            yo = yi[:, pr * 2 * SSD_HEADDIM:(pr + 1) * 2 * SSD_HEADDIM] * egx[:, cols] + y2
            y_ref[:, cols] = yo.astype(y_ref.dtype)
        gcols = slice(grp * gw, (grp + 1) * gw)
        s_ref[d, grp] = eex[:, gcols] * sg + _dot_tn(bg, wv[:, gcols])

    return [functools.partial(group, grp) for grp in range(SSD_GROUPS)]


def _ssd_init(s_ref, e_ref):
    s_ref[...] = jnp.zeros(s_ref.shape, F32)
    row = lax.broadcasted_iota(jnp.int32, (2 * D_SMALL, D_INNER), 0) % D_SMALL
    head = lax.broadcasted_iota(jnp.int32, (2 * D_SMALL, D_INNER), 1) // SSD_HEADDIM
    e_ref[0] = (row - SM_DTF == head).astype(F32).astype(BF16)
    e_ref[1] = (row - SM_DTB == head).astype(F32).astype(BF16)


def _mix_kernel(lg_ref, wah_ref, wal_ref, ba_ref, dtb_ref, alog_ref,
                rqk_f, rv_f, gqk_f, gv_f, ps_f, xbc_f,
                rqk_b, rv_b, gqk_b, gv_b, ps_b, xbc_b,
                yr_f, yr_b, yg_f, yg_b, ys_f, ys_b,
                rs_s, r_dm, r_rs, r_ks, r_cd, g_st, s_st, s_e):
    @pl.when(pl.program_id(0) == 0)
    def _():
        _ret_init(lg_ref, rs_s, r_dm, r_rs, r_ks, r_cd)
        g_st[...] = jnp.zeros(g_st.shape, F32)
        _ssd_init(s_st, s_e)

    ret_out, ret_state = _ret_phases(rqk_f, rv_f, yr_f, rqk_b, rv_b, yr_b, rs_s, r_dm, r_rs, r_ks, r_cd)
    gla_f = _gla_dir(0, gqk_f, gv_f, ps_f, wah_ref, wal_ref, ba_ref, yg_f, g_st)
    gla_b = _gla_dir(1, gqk_b, gv_b, ps_b, wah_ref, wal_ref, ba_ref, yg_b, g_st)
    ssd_f = _ssd_dir(0, xbc_f, ps_f, dtb_ref, alog_ref, ys_f, s_st, s_e)
    ssd_b = _ssd_dir(1, xbc_b, ps_b, dtb_ref, alog_ref, ys_b, s_st, s_e)
    ret_out()
    for fn in gla_f[0] + gla_b[0] + ssd_f:
        fn()
    ret_state()
    for fn in gla_f[1] + gla_b[1] + ssd_b:
        fn()


def _mix_scan(p, psm, xbc, logit8, wa_hi, wa_lo, ba, dtb_full, alog_full, ncc_rows):
    n = p.shape[0]
    c = CHUNK_RET
    assert c == GLA_BLOCK == CHUNK_SSD
    nch = n // c
    ncc = ncc_rows // c
    bwd = lambda t: _bwd_chunk(t, ncc, nch)
    w = H_RET * DV_RET
    gq = COL_GLA // w
    const = lambda shape: pl.BlockSpec(shape, lambda t: (0,) * len(shape))

    def blocks(idx):
        return [
            pl.BlockSpec((c, w), lambda t: (idx(t), 0)),
            pl.BlockSpec((c, w), lambda t: (idx(t), 1)),
            pl.BlockSpec((c, w), lambda t: (idx(t), gq)),
            pl.BlockSpec((c, w), lambda t: (idx(t), gq + 1)),
            pl.BlockSpec((c, D_SMALL), lambda t: (idx(t), 0)),
            pl.BlockSpec((c, D_XBC), lambda t: (idx(t), 0)),
        ]

    fwd = lambda t: t
    out_w = (w, w, w, w, D_INNER, D_INNER)
    out_idx = (fwd, bwd, fwd, bwd, fwd, bwd)
    return pl.pallas_call(
        _mix_kernel,
        grid=(nch,),
        in_specs=[
            const((8, 128)),
            const((2, D_SMALL, H_GLA * DK_GLA)),
            const((2, D_SMALL, H_GLA * DK_GLA)),
            const((2, 1, H_GLA * DK_GLA)),
            const((1, D_SMALL)),
            const((1, D_SMALL)),
        ] + blocks(fwd) + blocks(bwd),
        out_specs=[pl.BlockSpec((c, ow), functools.partial(lambda t, f: (f(t), 0), f=f))
                   for ow, f in zip(out_w, out_idx)],
        out_shape=[jax.ShapeDtypeStruct((n, ow), BF16) for ow in out_w],
        scratch_shapes=[
            pltpu.VMEM((2, H_RET, DK_RET, DV_RET), F32),
            pltpu.VMEM((2, H_RET, c, c), F32),
            pltpu.VMEM((2, H_RET, c, DV_RET), F32),
            pltpu.VMEM((2, H_RET, c, DK_RET), F32),
            pltpu.VMEM((2, H_RET, 8, DV_RET), F32),
            pltpu.VMEM((2, H_GLA, DV_GLA, DK_GLA), F32),
            pltpu.VMEM((2, SSD_GROUPS, D_STATE, D_INNER // SSD_GROUPS), F32),
            pltpu.VMEM((2, 2 * D_SMALL, D_INNER), BF16),
        ],
        compiler_params=_cparams(("arbitrary",)),
        name="mix_scan",
    )(logit8, wa_hi, wa_lo, ba, dtb_full, alog_full,
      p, p, p, p, psm, xbc, p, p, p, p, psm, xbc)


def _segnorm(y, width):
    outs = [_rms(y[:, s * width:(s + 1) * width]) for s in range(y.shape[1] // width)]
    return jnp.concatenate(outs, axis=1)


def _merge_kernel(x_ref, mod_ref, rg_ref, gr_ref, mg_ref, z_ref, xs_ref,
                  yrf_ref, yrb_ref, ygf_ref, ygb_ref, ysf_ref, ysb_ref,
                  gng_ref, dx_ref, sng_ref, wbr_ref, wbg_ref, wbs_ref, wo_ref,
                  ng_ref, wa_ref, wu_ref, wf_ref, fg_ref, o_ref, *, final):
    yr = (yrf_ref[...] + yrb_ref[...]).astype(F32)
    br = _segnorm(yr, DV_RET) * _silu_of_half(rg_ref[...].astype(F32))
    b_ret = _dot(br.astype(BF16), wbr_ref[...])
    yg = (ygf_ref[...] + ygb_ref[...]).astype(F32)
    bg = _segnorm(yg, DV_GLA) * gng_ref[...] * _silu_of_half(gr_ref[...].astype(F32))
    b_gla = _dot(bg.astype(BF16), wbg_ref[...])
    ys = (ysf_ref[...] + ysb_ref[...]).astype(F32)
    ys = (ys + dx_ref[...] * xs_ref[...].astype(F32)) * _silu_of_half(z_ref[...].astype(F32))
    bs = _segnorm(ys, D_INNER // SSD_GROUPS) * sng_ref[...]
    b_ssd = _dot(bs.astype(BF16), wbs_ref[...])
    gates = _sigmoid_of_half(mg_ref[...].astype(F32))
    mix = (gates[:, 0:D_MODEL] * b_ret + gates[:, D_MODEL:2 * D_MODEL] * b_gla
           + gates[:, 2 * D_MODEL:3 * D_MODEL] * b_ssd)
    out = _dot(mix.astype(BF16), wo_ref[...])
    x_mid = x_ref[...] + mod_ref[0, 5:6, :] * out
    _ffn_body(x_mid, mod_ref, ng_ref, wa_ref, wu_ref, wf_ref, fg_ref, o_ref, 6, 2, final)


def _merge(xa, mod, p, xbc, yrf, yrb, ygf, ygb, ysf, ysb, gng, dx, sng, wbr, wbg, wbs, wout,
           norm_g, wi, wf, final_g, l, nct, final):
    n = xa.shape[0]
    skip = nct if final else 0
    ntiles = n // TM - skip
    row = lambda w, cb: pl.BlockSpec((TM, w), lambda i: (i + skip, cb))
    return pl.pallas_call(
        functools.partial(_merge_kernel, final=final),
        grid=(ntiles,),
        in_specs=[
            row(D_MODEL, 0),
            pl.BlockSpec((None, 1, N_MOD, D_MODEL),
                         lambda i: (l, (i + skip >= nct).astype(jnp.int32), 0, 0)),
            row(D_MODEL, (COL_RET + 2048) // D_MODEL),
            row(D_MODEL, (COL_GLA + 2048) // D_MODEL),
            row(3 * D_MODEL, COL_MERGE // (3 * D_MODEL)),
            row(D_INNER, COL_Z // D_INNER),
            row(D_INNER, 0),
            row(D_MODEL, 0), row(D_MODEL, 0), row(D_MODEL, 0), row(D_MODEL, 0),
            row(D_INNER, 0), row(D_INNER, 0),
            _resident((None, 1, D_MODEL), lambda i: (l, 0, 0)),
            _resident((None, 1, D_INNER), lambda i: (l, 0, 0)),
            _resident((None, 1, D_INNER), lambda i: (l, 0, 0)),
            _resident((None, D_MODEL, D_MODEL), lambda i: (l, 0, 0)),
            _resident((None, D_MODEL, D_MODEL), lambda i: (l, 0, 0)),
            _resident((None, D_INNER, D_MODEL), lambda i: (l, 0, 0)),
            _resident((None, D_MODEL, D_MODEL), lambda i: (l, 0, 0)),
            _resident((None, 3, D_MODEL), lambda i: (l, 0, 0)),
            _resident((None, D_MODEL, D_FF), lambda i: (l, 0, 0)),
            _resident((None, D_MODEL, D_FF), lambda i: (l, 0, 1)),
            _resident((None, D_FF, D_MODEL), lambda i: (l, 0, 0)),
            _resident((1, D_MODEL), lambda i: (0, 0)),
        ],
        out_specs=pl.BlockSpec((TM, D_MODEL), lambda i: (i, 0)),
        out_shape=jax.ShapeDtypeStruct((ntiles * TM, D_MODEL), F32),
        compiler_params=_cparams(("arbitrary",)),
        name="merge",
    )(xa, mod, p, p, p, p, xbc, yrf, yrb, ygf, ygb, ysf, ysb, gng, dx, sng, wbr, wbg, wbs, wout,
      norm_g, wi, wi, wf, final_g)


def _rope_tables(n_ctx, n_lat):
    rows = n_lat // GRID_W
    nf = DK_RET // 4
    freq = ROPE_BASE ** (-jnp.arange(nf, dtype=F32) / nf)
    ang_r = jnp.arange(rows, dtype=F32)[:, None] * freq
    ang_c = jnp.arange(GRID_W, dtype=F32)[:, None] * freq

    def table(fn):
        tr = jnp.broadcast_to(fn(ang_r)[:, None, :], (rows, GRID_W, nf))
        tc = jnp.broadcast_to(fn(ang_c)[None, :, :], (rows, GRID_W, nf))
        return jnp.concatenate([tr, tc], axis=-1).reshape(n_lat, 2 * nf)

    cos, sin = table(jnp.cos), table(jnp.sin)
    cos_t = jnp.concatenate([cos, cos], axis=1)
    sin_t = jnp.concatenate([-sin, sin], axis=1)
    cos_t = jnp.concatenate([jnp.ones((n_ctx, DK_RET), F32), cos_t], axis=0)
    sin_t = jnp.concatenate([jnp.zeros((n_ctx, DK_RET), F32), sin_t], axis=0)
    return cos_t, sin_t


WP_COLS = 512
W_IN_SRC = ((COL_RET, 0, 3072), (COL_GLA, 3072, 3072), (COL_XBC, 8224, D_XBC),
            (COL_MERGE, 11360, 3 * D_MODEL), (COL_Z, 6176, D_INNER))
W_IN_SMALL_SRC = ((6144, 2 * GLA_RANK), (11296, 2 * H_SSD))


def _wprep_kernel(src_ref, wt_ref, o_ref):
    o_ref[...] = wt_ref[0].T.astype(BF16)


def _wsmall_kernel(a_ref, d_ref, o_ref):
    used = a_ref.shape[1] + d_ref.shape[1]
    rows = jnp.concatenate([a_ref[0], d_ref[0], jnp.zeros((D_SMALL - used, D_MODEL), F32)], axis=0)
    o_ref[...] = rows.T


def _wprep(w_in):
    depth = w_in.shape[0]
    wt = jnp.swapaxes(w_in, 1, 2)
    src = []
    for dst, start, width in W_IN_SRC:
        assert dst == len(src) * WP_COLS and width % WP_COLS == 0 and start % 8 == 0
        src.extend(r // 8 for r in range(start, start + width, WP_COLS))
    big = pl.pallas_call(
        _wprep_kernel,
        grid_spec=pltpu.PrefetchScalarGridSpec(
            num_scalar_prefetch=1,
            grid=(depth, D_BIG // WP_COLS),
            in_specs=[pl.BlockSpec((pl.Element(1), pl.Element(WP_COLS), pl.Element(D_MODEL)),
                                   lambda l, j, tab: (l, tab[j] * 8, 0))],
            out_specs=pl.BlockSpec((None, D_MODEL, WP_COLS), lambda l, j, tab: (l, 0, j)),
        ),
        out_shape=jax.ShapeDtypeStruct((depth, D_MODEL, D_BIG), BF16),
        compiler_params=_cparams(("arbitrary", "arbitrary")),
        name="wprep",
    )(jnp.asarray(src, jnp.int32), wt)
    small = pl.pallas_call(
        _wsmall_kernel,
        grid=(depth,),
        in_specs=[pl.BlockSpec((pl.Element(1), pl.Element(n), pl.Element(D_MODEL)),
                               functools.partial(lambda l, s: (l, s, 0), s=s))
                  for s, n in W_IN_SMALL_SRC],
        out_specs=pl.BlockSpec((None, D_MODEL, D_SMALL), lambda l: (l, 0, 0)),
        out_shape=jax.ShapeDtypeStruct((depth, D_MODEL, D_SMALL), F32),
        compiler_params=_cparams(("arbitrary",)),
        name="wsmall",
    )(wt, wt)
    return big, small


def kernel(x, c, ctx, c_ctx, ada_w, ada_b, norm_g, final_norm_g, ffn1_wi, ffn1_wo, ffn2_wi, ffn2_wo,
           w_in, ret_logit, gla_wa2, gla_ba, gla_norm_g, conv_w, conv_b, dt_bias, a_log, ssd_d,
           ssd_norm_g, wb_ret, wb_gla, wb_ssd, w_out):
    depth = ada_w.shape[0]
    n_ctx, n_lat = ctx.shape[1], x.shape[1]
    assert x.shape[0] == 1 and n_ctx % TM == 0 and n_lat % TM == 0
    nct = n_ctx // TM
    xa = (ctx[0], x[0])

    cond8 = jnp.zeros((8, D_MODEL), F32).at[0].set(c_ctx).at[1].set(c[0])
    mod = _adaln(cond8, ada_w, ada_b).reshape(depth, 8, N_MOD, D_MODEL)

    cos_t, sin_t = _rope_tables(n_ctx, n_lat)
    w_big, w_small = _wprep(w_in)
    colscale = jnp.ones((1, D_BIG), F32)
    colscale = colscale.at[:, COL_RET + 512:COL_RET + 1024].set(DK_RET ** -0.5)
    colscale = colscale.at[:, COL_GLA:COL_GLA + 512].set(DK_GLA ** -0.5)
    colscale = colscale.at[:, COL_RET + 2048:COL_RET + 3072].set(0.5)
    colscale = colscale.at[:, COL_GLA + 2048:COL_GLA + 3072].set(0.5)
    colscale = colscale.at[:, COL_MERGE:COL_MERGE + 3 * D_MODEL].set(0.5)
    colscale = colscale.at[:, COL_Z:COL_Z + D_INNER].set(0.5)

    bf = lambda t: t.astype(BF16)
    ffn1_wi, ffn1_wo, ffn2_wi, ffn2_wo = bf(ffn1_wi), bf(ffn1_wo), bf(ffn2_wi), bf(ffn2_wo)
    wb_ret, wb_gla, wb_ssd, w_out = bf(wb_ret), bf(wb_gla), bf(wb_ssd), bf(w_out)

    logit8 = jnp.broadcast_to(ret_logit.reshape(depth, 2 * H_RET, 1), (depth, 2 * H_RET, 128))
    wa_pad = jnp.zeros((depth, 2, D_SMALL, H_GLA * DK_GLA), F32)
    wa_pad = wa_pad.at[:, 0, SM_AF:SM_AF + GLA_RANK].set(gla_wa2[:, 0])
    wa_pad = wa_pad.at[:, 1, SM_AB:SM_AB + GLA_RANK].set(gla_wa2[:, 1])
    wa_hi = wa_pad.astype(BF16)
    wa_lo = (wa_pad - wa_hi.astype(F32)).astype(BF16)
    ws_hi = w_small.astype(BF16)
    ws_lo = (w_small - ws_hi.astype(F32)).astype(BF16)
    ba = gla_ba.reshape(depth, 2, 1, H_GLA * DK_GLA)
    zpad = jnp.zeros((depth, SM_DTF), F32)
    dtb_full = jnp.concatenate([zpad, dt_bias[:, 0], dt_bias[:, 1], zpad], axis=1).reshape(depth, 1, D_SMALL)
    alog_full = jnp.concatenate([zpad, a_log[:, 0], a_log[:, 1], zpad], axis=1).reshape(depth, 1, D_SMALL)
    conv_w8 = jnp.concatenate([conv_w, jnp.zeros((depth, 8 - CONV_K, D_XBC), F32)], axis=1)
    conv_b3 = conv_b.reshape(depth, 1, D_XBC)
    gng = jnp.tile(gla_norm_g, (1, H_GLA)).reshape(depth, 1, H_GLA * DV_GLA)
    dx = jnp.repeat(ssd_d, SSD_HEADDIM, axis=1).reshape(depth, 1, D_INNER)
    sng = ssd_norm_g.reshape(depth, 1, D_INNER)

    fg = final_norm_g.reshape(1, D_MODEL)
    for l in range(depth):
        xa = _ffn(xa, mod, norm_g, ffn1_wi, ffn1_wo, fg, l, 0, 0, nct)
        p, psm = _inproj(xa, mod, norm_g, cos_t, sin_t, w_big, colscale, ws_hi, ws_lo, l, nct)
        xbc = _conv(p, conv_w8, conv_b3, l, nct)
        yrf, yrb, ygf, ygb, ysf, ysb = _mix_scan(p, psm, xbc, logit8[l], wa_hi[l], wa_lo[l], ba[l],
                                                 dtb_full[l], alog_full[l], n_ctx)
        xa = _merge(xa, mod, p, xbc, yrf, yrb, ygf, ygb, ysf, ysb, gng, dx, sng,
                    wb_ret, wb_gla, wb_ssd, w_out, norm_g, ffn2_wi, ffn2_wo, fg, l, nct,
                    final=(l == depth - 1))

    return xa[None]
```

```python
import functools
import math

import jax
import jax.numpy as jnp
from jax import lax
from jax.experimental import pallas as pl
from jax.experimental.pallas import tpu as pltpu

F32 = jnp.float32
BF16 = jnp.bfloat16
HIGHEST = lax.Precision.HIGHEST
LOG2E = 1.4426950408889634

D_MODEL = 1024
GRID_W = 64
EPS = 1e-6
N_MOD = 9
D_FF = 2816
H_RET, DK_RET, DV_RET, CHUNK_RET = 4, 128, 256, 128
ROPE_BASE = 10000.0
H_GLA, DK_GLA, DV_GLA, GLA_RANK, CHUNK_GLA = 4, 128, 256, 16, 64
GLA_NORMALIZER = 16.0
GLA_SUB = 16
D_INNER = 2 * D_MODEL
SSD_HEADDIM = 64
H_SSD = D_INNER // SSD_HEADDIM
SSD_GROUPS = 4
D_STATE = 128
CONV_K = 5
CHUNK_SSD = 128
N_BC = SSD_GROUPS * D_STATE
D_XBC = D_INNER + 2 * N_BC

COL_RET = 0
COL_GLA = 2048
D_QKV = 2048
D_Y = 4096
COL_RETG = 4096
COL_GLAR = 5120
COL_XBC = 6144
COL_MERGE = 9216
COL_Z = 12288
D_BIG = 14336
SM_AF, SM_AB, SM_DTF, SM_DTB, D_SMALL = 0, 16, 32, 64, 128

TM = 256
CW = 1024
FF_SPLITS = (0, 1536, D_FF)
VMEM_LIMIT = 56 * 1024 * 1024


def _cparams(sem):
    return pltpu.CompilerParams(dimension_semantics=sem, vmem_limit_bytes=VMEM_LIMIT)


def _resident(shape, index_map):
    return pl.BlockSpec(shape, index_map, pipeline_mode=pl.Buffered(1))


def _sigmoid(x):
    return 0.5 * jnp.tanh(0.5 * x) + 0.5


def _silu(x):
    return x * _sigmoid(x)


def _silu_of_half(h):
    return h * jnp.tanh(h) + h


def _sigmoid_of_half(h):
    return 0.5 * jnp.tanh(h) + 0.5


def _softplus(x):
    return jnp.maximum(x, 0.0) + jnp.log(1.0 + jnp.exp(-jnp.abs(x)))


def _log_sigmoid(x, precise=False):
    e = jnp.exp(-jnp.abs(x))
    return jnp.minimum(x, 0.0) - (jnp.log1p(e) if precise else jnp.log(1.0 + e))


def _rms(x):
    return x * lax.rsqrt(jnp.mean(x * x, axis=-1, keepdims=True) + EPS)


def _pre(x, g, shift, scale):
    return _rms(x) * g * (1.0 + scale) + shift


def _dot(a, b):
    return jnp.dot(a, b, preferred_element_type=F32)


def _dot_exact(a, b):
    return jnp.dot(a, b, precision=HIGHEST, preferred_element_type=F32)


def _split2(x):
    hi = x.astype(BF16)
    return hi, (x - hi.astype(F32)).astype(BF16)


def _split3(x):
    hi = x.astype(BF16)
    r = x - hi.astype(F32)
    mid = r.astype(BF16)
    return hi, mid, (r - mid.astype(F32)).astype(BF16)


def _dot_sel(m, x):
    hi, mid, lo = _split3(x)
    return _dot(m, hi) + _dot(m, mid) + _dot(m, lo)


def _dot_split(x, w_hi, w_lo):
    x_hi, x_lo = _split2(x)
    return _dot(x_hi, w_hi) + _dot(x_lo, w_hi) + _dot(x_hi, w_lo)


def _dot_nt(a, b):
    return lax.dot_general(a, b, (((1,), (1,)), ((), ())), preferred_element_type=F32)


def _dot_tn(a, b):
    return lax.dot_general(a, b, (((0,), (0,)), ((), ())), preferred_element_type=F32)


def _adaln_kernel(cond_ref, w_ref, b_ref, o_ref):
    cnd = cond_ref[...]
    o_ref[...] = _dot_exact(_silu(cnd), w_ref[...]) + b_ref[...]


def _adaln(cond8, ada_w, ada_b):
    depth = ada_w.shape[0]
    nblk = N_MOD
    return pl.pallas_call(
        _adaln_kernel,
        grid=(depth, nblk),
        in_specs=[
            pl.BlockSpec((8, D_MODEL), lambda l, j: (0, 0)),
            pl.BlockSpec((None, D_MODEL, D_MODEL), lambda l, j: (l, 0, j)),
            pl.BlockSpec((None, 1, D_MODEL), lambda l, j: (l, 0, j)),
        ],
        out_specs=pl.BlockSpec((None, 8, D_MODEL), lambda l, j: (l, 0, j)),
        out_shape=jax.ShapeDtypeStruct((depth, 8, N_MOD * D_MODEL), F32),
        compiler_params=_cparams(("arbitrary", "arbitrary")),
        name="adaln",
    )(cond8, ada_w, ada_b.reshape(depth, 1, N_MOD * D_MODEL))


def _ffn_kernel(x_ref, mod_ref, g_ref, wa_ref, wu_ref, wo_ref, fg_ref, o_ref, *, k0, gi, final):
    _ffn_body(x_ref[...], mod_ref, g_ref, wa_ref, wu_ref, wo_ref, fg_ref, o_ref, k0, gi, final)


def _ffn_body(x, mod_ref, g_ref, wa_ref, wu_ref, wo_ref, fg_ref, o_ref, k0, gi, final):
    h = _pre(x, g_ref[gi:gi + 1, :], mod_ref[0, k0:k0 + 1, :], mod_ref[0, k0 + 1:k0 + 2, :])
    hb = h.astype(BF16)
    out = None
    for lo, hi in zip(FF_SPLITS[:-1], FF_SPLITS[1:]):
        sl = slice(lo, hi)
        a = _dot(hb, wa_ref[:, sl])
        u = _dot(hb, wu_ref[:, sl])
        t = (_silu(a) * u).astype(BF16)
        part = _dot(t, wo_ref[sl, :])
        out = part if out is None else out + part
    y = x + 0.5 * mod_ref[0, k0 + 2:k0 + 3, :] * out
    o_ref[...] = _rms(y) * fg_ref[...] if final else y


def _ffn_split_kernel(ctx_ref, lat_ref, mod_ref, g_ref, wa_ref, wu_ref, wo_ref, fg_ref, o_ref,
                      *, k0, gi, nct):
    x = jnp.where(pl.program_id(0) >= nct, lat_ref[...], ctx_ref[...])
    _ffn_body(x, mod_ref, g_ref, wa_ref, wu_ref, wo_ref, fg_ref, o_ref, k0, gi, False)


def _ffn(xs, mod, norm_g, wi, wo, final_g, l, k0, gi, nct):
    split = isinstance(xs, tuple)
    n = xs[0].shape[0] + xs[1].shape[0] if split else xs.shape[0]
    if split:
        kern = functools.partial(_ffn_split_kernel, k0=k0, gi=gi, nct=nct)
        x_specs = [pl.BlockSpec((TM, D_MODEL), lambda i: (jnp.minimum(i, nct - 1), 0)),
                   pl.BlockSpec((TM, D_MODEL), lambda i: (jnp.maximum(i - nct, 0), 0))]
        x_args = list(xs)
    else:
        kern = functools.partial(_ffn_kernel, k0=k0, gi=gi, final=False)
        x_specs = [pl.BlockSpec((TM, D_MODEL), lambda i: (i, 0))]
        x_args = [xs]
    return pl.pallas_call(
        kern,
        grid=(n // TM,),
        in_specs=x_specs + [
            pl.BlockSpec((None, 1, N_MOD, D_MODEL),
                         lambda i: (l, (i >= nct).astype(jnp.int32), 0, 0)),
            _resident((None, 3, D_MODEL), lambda i: (l, 0, 0)),
            _resident((None, D_MODEL, D_FF), lambda i: (l, 0, 0)),
            _resident((None, D_MODEL, D_FF), lambda i: (l, 0, 1)),
            _resident((None, D_FF, D_MODEL), lambda i: (l, 0, 0)),
            _resident((1, D_MODEL), lambda i: (0, 0)),
        ],
        out_specs=pl.BlockSpec((TM, D_MODEL), lambda i: (i, 0)),
        out_shape=jax.ShapeDtypeStruct((n, D_MODEL), F32),
        compiler_params=_cparams(("arbitrary",)),
        name="ffn",
    )(*x_args, mod, norm_g, wi, wi, wo, final_g)


def _inproj_kernel(x_ref, mod_ref, g_ref, cos_ref, sin_ref, w_ref, cs_ref, wsh_ref, wsl_ref,
                   p_ref, ps_ref):
    h = _pre(x_ref[...], g_ref[1:2, :], mod_ref[0, 3:4, :], mod_ref[0, 4:5, :])
    hb = h.astype(BF16)
    hl = _dot(hb, wsh_ref[...])
    ps_ref[...] = (hl[:, 0:D_SMALL] + hl[:, D_SMALL:2 * D_SMALL]
                   + _dot((h - hb.astype(F32)).astype(BF16), wsl_ref[...]))
    for j in range(D_BIG // CW):
        cols = slice(j * CW, (j + 1) * CW)
        acc = _dot(hb, w_ref[:, cols]) * cs_ref[:, cols]
        if j == 0:
            cos = cos_ref[...]
            sin = sin_ref[...]
            for b in range(CW // DK_RET):
                t = acc[:, b * DK_RET:(b + 1) * DK_RET]
                p_ref[:, b * DK_RET:(b + 1) * DK_RET] = (
                    t * cos + pltpu.roll(t, DK_RET // 2, 1) * sin).astype(BF16)
        else:
            p_ref[:, cols] = acc.astype(BF16)


def _inproj(xa, mod, norm_g, cos_t, sin_t, w_big, colscale, ws_hi, ws_lo, l, nct):
    n = xa.shape[0]
    return pl.pallas_call(
        _inproj_kernel,
        grid=(n // TM,),
        in_specs=[
            pl.BlockSpec((TM, D_MODEL), lambda i: (i, 0)),
            pl.BlockSpec((None, 1, N_MOD, D_MODEL), lambda i: (l, (i >= nct).astype(jnp.int32), 0, 0)),
            _resident((None, 3, D_MODEL), lambda i: (l, 0, 0)),
            pl.BlockSpec((TM, DK_RET), lambda i: (i, 0)),
            pl.BlockSpec((TM, DK_RET), lambda i: (i, 0)),
            _resident((None, D_MODEL, D_BIG), lambda i: (l, 0, 0)),
            _resident((1, D_BIG), lambda i: (0, 0)),
            _resident((None, D_MODEL, 2 * D_SMALL), lambda i: (l, 0, 0)),
            _resident((None, D_MODEL, D_SMALL), lambda i: (l, 0, 0)),
        ],
        out_specs=[
            pl.BlockSpec((TM, D_BIG), lambda i: (i, 0)),
            pl.BlockSpec((TM, D_SMALL), lambda i: (i, 0)),
        ],
        out_shape=[
            jax.ShapeDtypeStruct((n, D_BIG), BF16),
            jax.ShapeDtypeStruct((n, D_SMALL), F32),
        ],
        compiler_params=_cparams(("arbitrary",)),
        name="inproj",
    )(xa, mod, norm_g, cos_t, sin_t, w_big, colscale, jnp.concatenate([ws_hi, ws_lo], axis=-1), ws_hi)


CONV_STRIP = 512
CONV_HALO = 16
CONV_BLK = 128


def _conv_kernel(cur_ref, prev_ref, next_ref, w_ref, b_ref, o_ref, ext_ref, *, nct, ntiles):
    i = pl.program_id(0)
    has_prev = jnp.logical_and(i != 0, i != nct)
    has_next = jnp.logical_and(i != nct - 1, i != ntiles - 1)
    halo0 = jnp.zeros((CONV_HALO, D_XBC), BF16)
    ext_ref[0:CONV_HALO, :] = jnp.where(has_prev, prev_ref[...], halo0)
    ext_ref[CONV_HALO:CONV_HALO + TM, :] = cur_ref[...]
    ext_ref[CONV_HALO + TM:2 * CONV_HALO + TM, :] = jnp.where(has_next, next_ref[...], halo0)
    kwin = CONV_BLK + 2 * CONV_HALO
    rr = lax.broadcasted_iota(jnp.int32, (CONV_BLK, kwin), 0)
    cc = lax.broadcasted_iota(jnp.int32, (CONV_BLK, kwin), 1)
    mid = CONV_K // 2
    shift = {k: (cc == rr + CONV_HALO + k - mid).astype(F32).astype(BF16)
             for k in range(CONV_K) if k != mid}
    for blk in range(TM // CONV_BLK):
        rows = slice(blk * CONV_BLK, (blk + 1) * CONV_BLK)
        for s in range(D_XBC // CONV_STRIP):
            cs = slice(s * CONV_STRIP, (s + 1) * CONV_STRIP)
            win = ext_ref[blk * CONV_BLK:blk * CONV_BLK + kwin, cs]
            acc = b_ref[:, cs] + w_ref[mid:mid + 1, cs] * cur_ref[rows, cs].astype(F32)
            for k in range(CONV_K):
                if k != mid:
                    acc = acc + w_ref[k:k + 1, cs] * _dot(shift[k], win)
            o_ref[rows, cs] = _silu(acc).astype(BF16)


def _conv(p, conv_w, conv_b, l, nct):
    n = p.shape[0]
    ntiles = n // TM
    rh = TM // CONV_HALO
    nblkh = n // CONV_HALO
    cb = COL_XBC // D_XBC
    return pl.pallas_call(
        functools.partial(_conv_kernel, nct=nct, ntiles=ntiles),
        grid=(ntiles,),
        in_specs=[
            pl.BlockSpec((TM, D_XBC), lambda i: (i, cb)),
            pl.BlockSpec((CONV_HALO, D_XBC), lambda i: (jnp.maximum(i * rh - 1, 0), cb)),
            pl.BlockSpec((CONV_HALO, D_XBC), lambda i: (jnp.minimum((i + 1) * rh, nblkh - 1), cb)),
            pl.BlockSpec((None, 8, D_XBC), lambda i: (l, 0, 0)),
            pl.BlockSpec((None, 1, D_XBC), lambda i: (l, 0, 0)),
        ],
        out_specs=pl.BlockSpec((TM, D_XBC), lambda i: (i, 0)),
        out_shape=jax.ShapeDtypeStruct((n, D_XBC), BF16),
        scratch_shapes=[pltpu.VMEM((TM + 2 * CONV_HALO, D_XBC), BF16)],
        compiler_params=_cparams(("arbitrary",)),
        name="conv",
    )(p, p, p, conv_w, conv_b)


def _bwd_chunk(t, ncc, n):
    return jnp.where(t < ncc, ncc - 1 - t, n + ncc - 1 - t)


def _ret_init(lg_ref, s_ref, dm_ref, rs_ref, ks_ref, cd_ref):
    c = CHUNK_RET
    s_ref[...] = jnp.zeros(s_ref.shape, F32)
    la = _log_sigmoid(lg_ref[...], precise=True)
    ii = lax.broadcasted_iota(jnp.int32, (c, c), 0)
    jj = lax.broadcasted_iota(jnp.int32, (c, c), 1)
    ir = lax.broadcasted_iota(jnp.int32, (c, 1), 0)
    for d in range(2):
        for h in range(H_RET):
            a = la[d * H_RET + h:d * H_RET + h + 1, :]
            a2 = jnp.concatenate([a, a], axis=1)
            if d == 0:
                dist, keep = ii - jj, ii >= jj
                rpow = ir + 1
                kpow = c - 1 - ir
            else:
                dist, keep = jj - ii, jj >= ii
                rpow = c - ir
                kpow = ir
            dm_ref[d, h] = jnp.where(keep, jnp.exp(dist.astype(F32) * a), 0.0)
            rs_ref[d, h] = jnp.exp(rpow.astype(F32) * a2)
            ks_ref[d, h] = jnp.exp(kpow.astype(F32) * a)
            cd_ref[d, h] = jnp.exp(float(c) * jnp.broadcast_to(a2, (8, DV_RET)))


def _ret_phases(qkf_ref, vf_ref, yf_ref, qkb_ref, vb_ref, yb_ref, s_ref, dm_ref, rs_ref, ks_ref, cd_ref):
    dirs = ((qkf_ref, vf_ref, yf_ref), (qkb_ref, vb_ref, yb_ref))

    def q_of(qk_ref, h):
        return qk_ref[:, h * DK_RET:(h + 1) * DK_RET]

    def k_of(qk_ref, h):
        return qk_ref[:, H_RET * DK_RET + h * DK_RET:H_RET * DK_RET + (h + 1) * DK_RET]

    qk = [[_dot_nt(q_of(qk_ref, h), k_of(qk_ref, h)) for h in range(H_RET)] for qk_ref, _, _ in dirs]
    qs = [[_dot(q_of(qk_ref, h), s_ref[d, h].astype(BF16)) for h in range(H_RET)]
          for d, (qk_ref, _, _) in enumerate(dirs)]
    def outputs():
        for d, (qk_ref, v_ref, y_ref) in enumerate(dirs):
            for h in range(H_RET):
                v = v_ref[:, h * DV_RET:(h + 1) * DV_RET]
                att = (qk[d][h] * dm_ref[d, h]).astype(BF16)
                y = rs_ref[d, h] * qs[d][h] + _dot(att, v)
                y_ref[:, h * DV_RET:(h + 1) * DV_RET] = y.astype(y_ref.dtype)

    def states():
        for d, (qk_ref, v_ref, y_ref) in enumerate(dirs):
            for h in range(H_RET):
                v = v_ref[:, h * DV_RET:(h + 1) * DV_RET]
                kt = (k_of(qk_ref, h).astype(F32) * ks_ref[d, h]).astype(BF16)
                s_ref[d, h] = cd_ref[d, h][0:1, :] * s_ref[d, h] + _dot_tn(kt, v)

    return outputs, states


GLA_BLOCK = 128


def _gla_dir(d, qk_ref, v_ref, ps_ref, wah_ref, wal_ref, ba_ref, y_ref, st_ref):
    c = CHUNK_GLA
    nb = GLA_BLOCK
    nsub = c // GLA_SUB
    bi = lax.broadcasted_iota(jnp.int32, (nb, nb), 0)
    bj = lax.broadcasted_iota(jnp.int32, (nb, nb), 1)
    same = (bi // c) == (bj // c)
    sub_start = jnp.bitwise_and(bi, -GLA_SUB)
    if d == 0:
        cum = jnp.logical_and(same, bj <= bi)
        ref_m = jnp.logical_and(same, bj < sub_start)
    else:
        cum = jnp.logical_and(same, bj >= bi)
        ref_m = jnp.logical_and(same, bj >= sub_start + GLA_SUB)
    tri = jnp.concatenate([cum, ref_m], axis=0).astype(F32).astype(BF16)
    z = _dot_split(ps_ref[...], wah_ref[d], wal_ref[d]) + ba_ref[d]
    a = _log_sigmoid(z) * (1.0 / GLA_NORMALIZER)
    gr = _dot_sel(tri, a)
    ii = lax.broadcasted_iota(jnp.int32, (c, c), 0)
    jj = lax.broadcasted_iota(jnp.int32, (c, c), 1)
    causal = (jj <= ii) if d == 0 else (jj >= ii)
    end_row = c - 1 if d == 0 else 0
    col_blk = jnp.right_shift(jj, int(math.log2(GLA_SUB)))

    def sub_rows(jb):
        return slice(jb * GLA_SUB, c) if d == 0 else slice(0, (jb + 1) * GLA_SUB)

    order = list(range(nb // c)) if d == 0 else list(range(nb // c - 1, -1, -1))
    pre = {}
    for ci in order:
        rows = slice(ci * c, (ci + 1) * c)
        g_all = gr[ci * c:(ci + 1) * c]
        r_all = gr[nb + ci * c:nb + (ci + 1) * c]
        for h in range(H_GLA):
            ks = slice(h * DK_GLA, (h + 1) * DK_GLA)
            q = qk_ref[rows, h * DK_GLA:(h + 1) * DK_GLA].astype(F32)
            k = qk_ref[rows, H_GLA * DK_GLA + h * DK_GLA:H_GLA * DK_GLA + (h + 1) * DK_GLA].astype(F32)
            g = g_all[:, ks]
            r = r_all[:, ks]
            g_end = g[end_row:end_row + 1, :]
            kt = (k * jnp.exp(r - g)).astype(BF16)
            qs = []
            for jb in range(nsub):
                rj = r[jb * GLA_SUB:jb * GLA_SUB + 1, :]
                need = sub_rows(jb)
                qs.append((q[need] * jnp.exp(jnp.minimum(g[need] - rj, 0.0))).astype(BF16))
            pm = _dot_nt(jnp.concatenate(qs, axis=0), kt)
            qg = (q * jnp.exp(g)).astype(BF16)
            ke = (k * jnp.exp(g_end - g)).astype(BF16)
            pre[(ci, h)] = (pm, qg, ke, jnp.exp(g_end))

    def finish(ci, h):
        rows = slice(ci * c, (ci + 1) * c)
        pm, qg, ke, e_end = pre[(ci, h)]
        v = v_ref[rows, h * DV_GLA:(h + 1) * DV_GLA]
        att = jnp.zeros((c, c), F32)
        off = 0
        for jb in range(nsub):
            need = sub_rows(jb)
            nr = need.stop - need.start
            pads = [jnp.zeros((need.start, c), F32), pm[off:off + nr], jnp.zeros((c - need.stop, c), F32)]
            piece = jnp.concatenate([t for t in pads if t.shape[0]], axis=0)
            att = jnp.where(col_blk == jb, piece, att)
            off += nr
        att = jnp.where(causal, att, 0.0).astype(BF16)
        st = st_ref[d, h]
        y = _dot_nt(qg, st.astype(BF16)) + _dot(att, v)
        y_ref[rows, h * DV_GLA:(h + 1) * DV_GLA] = y.astype(y_ref.dtype)
        st_ref[d, h] = e_end * st + _dot_tn(v, ke)

    return [[functools.partial(finish, ci, h) for h in range(H_GLA)] for ci in order]


def _ssd_dir(d, xbc_ref, ps_ref, dtb_ref, alog_ref, y_ref, s_ref, e_ref):
    c = CHUNK_SSD
    base = SM_DTF if d == 0 else SM_DTB
    ii = lax.broadcasted_iota(jnp.int32, (c, c), 0)
    jj = lax.broadcasted_iota(jnp.int32, (c, c), 1)
    keep = (jj <= ii) if d == 0 else (jj >= ii)
    end_row = c - 1 if d == 0 else 0
    lane = lax.broadcasted_iota(jnp.int32, (1, D_SMALL), 1)
    in_dir = jnp.logical_and(lane >= base, lane < base + H_SSD)
    dt = jnp.where(in_dir, _softplus(ps_ref[...] + dtb_ref[...]), 0.0)
    a = dt * jnp.where(in_dir, -jnp.exp(alog_ref[...]), 0.0)
    g = _dot_sel(keep.astype(F32).astype(BF16), a)
    g2 = g * LOG2E
    gt = ((g - jnp.where(in_dir, jnp.log(dt), 0.0)) * LOG2E).T
    g_end = g[end_row:end_row + 1, :]
    f_state = dt * jnp.exp(g_end - g)
    f_y = jnp.exp(g)
    f_end = jnp.broadcast_to(jnp.exp(g_end), (8, D_SMALL))
    fs_hi, fs_lo = _split2(jnp.concatenate([f_state, f_y, f_end], axis=0))
    fx = _dot(jnp.concatenate([fs_hi, fs_lo], axis=1), e_ref[d])
    wv = (xbc_ref[:, 0:D_INNER].astype(F32) * fx[0:c]).astype(BF16)
    egx = fx[c:2 * c]
    eex = fx[2 * c:2 * c + 1]
    lane2 = lax.broadcasted_iota(jnp.int32, (c, 2 * SSD_HEADDIM), 1)
    hpg = H_SSD // SSD_GROUPS
    gw = hpg * SSD_HEADDIM
    def b_of(grp):
        return xbc_ref[:, D_INNER + grp * D_STATE:D_INNER + (grp + 1) * D_STATE]

    def c_of(grp):
        return xbc_ref[:, D_INNER + N_BC + grp * D_STATE:D_INNER + N_BC + (grp + 1) * D_STATE]

    cbs = [_dot_nt(c_of(grp), b_of(grp)).astype(BF16) for grp in range(SSD_GROUPS)]
    yis = [_dot(c_of(grp), s_ref[d, grp].astype(BF16)) for grp in range(SSD_GROUPS)]

    def group(grp):
        bg = b_of(grp)
        cb = cbs[grp]
        sg = s_ref[d, grp]
        yi = yis[grp]
        for pr in range(hpg // 2):
            h0 = grp * hpg + 2 * pr
            cols = slice(h0 * SSD_HEADDIM, (h0 + 2) * SSD_HEADDIM)
            atts = []
            for hh in (h0, h0 + 1):
                col = base + hh
                diff = g2[:, col:col + 1] - gt[col:col + 1, :]
                dec = jnp.exp2(jnp.where(keep, diff, -jnp.inf))
                atts.append(cb * dec.astype(BF16))
            v2 = xbc_ref[:, cols]
            vv = jnp.concatenate([jnp.where(lane2 < SSD_HEADDIM, v2, jnp.zeros_like(v2)),
                                  jnp.where(lane2 >= SSD_HEADDIM, v2, jnp.zeros_like(v2))], axis=0)
            y2 = _dot(jnp.concatenate(atts, axis=1), vv)
            yo = yi[:, pr * 2 * SSD_HEADDIM:(pr + 1) * 2 * SSD_HEADDIM] * egx[:, cols] + y2
            y_ref[:, cols] = yo.astype(y_ref.dtype)
        gcols = slice(grp * gw, (grp + 1) * gw)
        s_ref[d, grp] = eex[:, gcols] * sg + _dot_tn(bg, wv[:, gcols])

    return [functools.partial(group, grp) for grp in range(SSD_GROUPS)]


def _ssd_init(s_ref, e_ref):
    s_ref[...] = jnp.zeros(s_ref.shape, F32)
    row = lax.broadcasted_iota(jnp.int32, (2 * D_SMALL, D_INNER), 0) % D_SMALL
    head = lax.broadcasted_iota(jnp.int32, (2 * D_SMALL, D_INNER), 1) // SSD_HEADDIM
    e_ref[0] = (row - SM_DTF == head).astype(F32).astype(BF16)
    e_ref[1] = (row - SM_DTB == head).astype(F32).astype(BF16)


def _mix_kernel(lg_ref, wah_ref, wal_ref, ba_ref, dtb_ref, alog_ref,
                qkv_f, ps_f, xbc_f, qkv_b, ps_b, xbc_b, y_f, y_b,
                rs_s, r_dm, r_rs, r_ks, r_cd, g_st, s_st, s_e):
    @pl.when(pl.program_id(0) == 0)
    def _():
        _ret_init(lg_ref, rs_s, r_dm, r_rs, r_ks, r_cd)
        g_st[...] = jnp.zeros(g_st.shape, F32)
        _ssd_init(s_st, s_e)

    w = H_RET * DV_RET

    def split_in(ref):
        return (ref.at[:, COL_RET:COL_RET + w], ref.at[:, COL_RET + w:COL_RET + 2 * w],
                ref.at[:, COL_GLA:COL_GLA + w], ref.at[:, COL_GLA + w:COL_GLA + 2 * w])

    def split_out(ref):
        return ref.at[:, 0:w], ref.at[:, w:2 * w], ref.at[:, 2 * w:2 * w + D_INNER]

    rqk_f, rv_f, gqk_f, gv_f = split_in(qkv_f)
    rqk_b, rv_b, gqk_b, gv_b = split_in(qkv_b)
    yr_f, yg_f, ys_f = split_out(y_f)
    yr_b, yg_b, ys_b = split_out(y_b)

    ret_out, ret_state = _ret_phases(rqk_f, rv_f, yr_f, rqk_b, rv_b, yr_b, rs_s, r_dm, r_rs, r_ks, r_cd)
    gla_f = _gla_dir(0, gqk_f, gv_f, ps_f, wah_ref, wal_ref, ba_ref, yg_f, g_st)
    gla_b = _gla_dir(1, gqk_b, gv_b, ps_b, wah_ref, wal_ref, ba_ref, yg_b, g_st)
    ssd_f = _ssd_dir(0, xbc_f, ps_f, dtb_ref, alog_ref, ys_f, s_st, s_e)
    ssd_b = _ssd_dir(1, xbc_b, ps_b, dtb_ref, alog_ref, ys_b, s_st, s_e)
    ret_out()
    for fn in gla_f[0] + gla_b[0] + ssd_f:
        fn()
    ret_state()
    for fn in gla_f[1] + gla_b[1] + ssd_b:
        fn()


def _mix_scan(p, psm, xbc, logit8, wa_hi, wa_lo, ba, dtb_full, alog_full, ncc_rows):
    n = p.shape[0]
    c = CHUNK_RET
    assert c == GLA_BLOCK == CHUNK_SSD
    nch = n // c
    ncc = ncc_rows // c
    bwd = lambda t: _bwd_chunk(t, ncc, nch)
    assert COL_RET == 0 and COL_GLA == D_QKV
    const = lambda shape: pl.BlockSpec(shape, lambda t: (0,) * len(shape))

    def blocks(idx):
        return [
            pl.BlockSpec((c, 2 * D_QKV), lambda t: (idx(t), 0)),
            pl.BlockSpec((c, D_SMALL), lambda t: (idx(t), 0)),
            pl.BlockSpec((c, D_XBC), lambda t: (idx(t), 0)),
        ]

    fwd = lambda t: t
    out_w = (D_Y, D_Y)
    out_idx = (fwd, bwd)
    return pl.pallas_call(
        _mix_kernel,
        grid=(nch,),
        in_specs=[
            const((8, 128)),
            const((2, D_SMALL, H_GLA * DK_GLA)),
            const((2, D_SMALL, H_GLA * DK_GLA)),
            const((2, 1, H_GLA * DK_GLA)),
            const((1, D_SMALL)),
            const((1, D_SMALL)),
        ] + blocks(fwd) + blocks(bwd),
        out_specs=[pl.BlockSpec((c, ow), functools.partial(lambda t, f: (f(t), 0), f=f))
                   for ow, f in zip(out_w, out_idx)],
        out_shape=[jax.ShapeDtypeStruct((n, ow), BF16) for ow in out_w],
        scratch_shapes=[
            pltpu.VMEM((2, H_RET, DK_RET, DV_RET), F32),
            pltpu.VMEM((2, H_RET, c, c), F32),
            pltpu.VMEM((2, H_RET, c, DV_RET), F32),
            pltpu.VMEM((2, H_RET, c, DK_RET), F32),
            pltpu.VMEM((2, H_RET, 8, DV_RET), F32),
            pltpu.VMEM((2, H_GLA, DV_GLA, DK_GLA), F32),
            pltpu.VMEM((2, SSD_GROUPS, D_STATE, D_INNER // SSD_GROUPS), F32),
            pltpu.VMEM((2, 2 * D_SMALL, D_INNER), BF16),
        ],
        compiler_params=_cparams(("arbitrary",)),
        name="mix_scan",
    )(logit8, wa_hi, wa_lo, ba, dtb_full, alog_full, p, psm, xbc, p, psm, xbc)


def _segnorm(y, width):
    outs = [_rms(y[:, s * width:(s + 1) * width]) for s in range(y.shape[1] // width)]
    return jnp.concatenate(outs, axis=1)


def _merge_kernel(x_ref, mod_ref, rg_ref, gr_ref, mg_ref, z_ref, xs_ref, yf_ref, yb_ref,
                  gng_ref, dx_ref, sng_ref, wbr_ref, wbg_ref, wbs_ref, wo_ref,
                  ng_ref, wa_ref, wu_ref, wf_ref, fg_ref, o_ref, *, final):
    c_r, c_g, c_s = slice(0, D_MODEL), slice(D_MODEL, 2 * D_MODEL), slice(2 * D_MODEL, D_Y)
    yr = (yf_ref[:, c_r] + yb_ref[:, c_r]).astype(F32)
    br = _segnorm(yr, DV_RET) * _silu_of_half(rg_ref[...].astype(F32))
    b_ret = _dot(br.astype(BF16), wbr_ref[...])
    yg = (yf_ref[:, c_g] + yb_ref[:, c_g]).astype(F32)
    bg = _segnorm(yg, DV_GLA) * gng_ref[...] * _silu_of_half(gr_ref[...].astype(F32))
    b_gla = _dot(bg.astype(BF16), wbg_ref[...])
    ys = (yf_ref[:, c_s] + yb_ref[:, c_s]).astype(F32)
    ys = (ys + dx_ref[...] * xs_ref[...].astype(F32)) * _silu_of_half(z_ref[...].astype(F32))
    bs = _segnorm(ys, D_INNER // SSD_GROUPS) * sng_ref[...]
    b_ssd = _dot(bs.astype(BF16), wbs_ref[...])
    gates = _sigmoid_of_half(mg_ref[...].astype(F32))
    mix = (gates[:, 0:D_MODEL] * b_ret + gates[:, D_MODEL:2 * D_MODEL] * b_gla
           + gates[:, 2 * D_MODEL:3 * D_MODEL] * b_ssd)
    out = _dot(mix.astype(BF16), wo_ref[...])
    x_mid = x_ref[...] + mod_ref[0, 5:6, :] * out
    _ffn_body(x_mid, mod_ref, ng_ref, wa_ref, wu_ref, wf_ref, fg_ref, o_ref, 6, 2, final)


def _merge(xa, mod, p, xbc, yf, yb, gng, dx, sng, wbr, wbg, wbs, wout,
           norm_g, wi, wf, final_g, l, nct, final):
    n = xa.shape[0]
    skip = nct if final else 0
    ntiles = n // TM - skip
    row = lambda w, cb: pl.BlockSpec((TM, w), lambda i: (i + skip, cb))
    return pl.pallas_call(
        functools.partial(_merge_kernel, final=final),
        grid=(ntiles,),
        in_specs=[
            row(D_MODEL, 0),
            pl.BlockSpec((None, 1, N_MOD, D_MODEL),
                         lambda i: (l, (i + skip >= nct).astype(jnp.int32), 0, 0)),
            row(D_MODEL, COL_RETG // D_MODEL),
            row(D_MODEL, COL_GLAR // D_MODEL),
            row(3 * D_MODEL, COL_MERGE // (3 * D_MODEL)),
            row(D_INNER, COL_Z // D_INNER),
            row(D_INNER, 0),
            row(D_Y, 0), row(D_Y, 0),
            _resident((None, 1, D_MODEL), lambda i: (l, 0, 0)),
            _resident((None, 1, D_INNER), lambda i: (l, 0, 0)),
            _resident((None, 1, D_INNER), lambda i: (l, 0, 0)),
            _resident((None, D_MODEL, D_MODEL), lambda i: (l, 0, 0)),
            _resident((None, D_MODEL, D_MODEL), lambda i: (l, 0, 0)),
            _resident((None, D_INNER, D_MODEL), lambda i: (l, 0, 0)),
            _resident((None, D_MODEL, D_MODEL), lambda i: (l, 0, 0)),
            _resident((None, 3, D_MODEL), lambda i: (l, 0, 0)),
            _resident((None, D_MODEL, D_FF), lambda i: (l, 0, 0)),
            _resident((None, D_MODEL, D_FF), lambda i: (l, 0, 1)),
            _resident((None, D_FF, D_MODEL), lambda i: (l, 0, 0)),
            _resident((1, D_MODEL), lambda i: (0, 0)),
        ],
        out_specs=pl.BlockSpec((TM, D_MODEL), lambda i: (i, 0)),
        out_shape=jax.ShapeDtypeStruct((ntiles * TM, D_MODEL), F32),
        compiler_params=_cparams(("arbitrary",)),
        name="merge",
    )(xa, mod, p, p, p, p, xbc, yf, yb, gng, dx, sng, wbr, wbg, wbs, wout,
      norm_g, wi, wi, wf, final_g)


def _rope_tables(n_ctx, n_lat):
    rows = n_lat // GRID_W
    nf = DK_RET // 4
    freq = ROPE_BASE ** (-jnp.arange(nf, dtype=F32) / nf)
    ang_r = jnp.arange(rows, dtype=F32)[:, None] * freq
    ang_c = jnp.arange(GRID_W, dtype=F32)[:, None] * freq

    def table(fn):
        tr = jnp.broadcast_to(fn(ang_r)[:, None, :], (rows, GRID_W, nf))
        tc = jnp.broadcast_to(fn(ang_c)[None, :, :], (rows, GRID_W, nf))
        return jnp.concatenate([tr, tc], axis=-1).reshape(n_lat, 2 * nf)

    cos, sin = table(jnp.cos), table(jnp.sin)
    cos_t = jnp.concatenate([cos, cos], axis=1)
    sin_t = jnp.concatenate([-sin, sin], axis=1)
    cos_t = jnp.concatenate([jnp.ones((n_ctx, DK_RET), F32), cos_t], axis=0)
    sin_t = jnp.concatenate([jnp.zeros((n_ctx, DK_RET), F32), sin_t], axis=0)
    return cos_t, sin_t


WP_COLS = 512
W_IN_SRC = ((COL_RET, 0, D_QKV), (COL_GLA, 3072, D_QKV), (COL_RETG, 2048, D_MODEL),
            (COL_GLAR, 5120, D_MODEL), (COL_XBC, 8224, D_XBC),
            (COL_MERGE, 11360, 3 * D_MODEL), (COL_Z, 6176, D_INNER))
W_IN_SMALL_SRC = ((6144, 2 * GLA_RANK), (11296, 2 * H_SSD))


def _wprep_kernel(src_ref, wt_ref, o_ref):
    o_ref[...] = wt_ref[0].T.astype(BF16)


def _wsmall_kernel(a_ref, d_ref, o_ref):
    used = a_ref.shape[1] + d_ref.shape[1]
    rows = jnp.concatenate([a_ref[0], d_ref[0], jnp.zeros((D_SMALL - used, D_MODEL), F32)], axis=0)
    o_ref[...] = rows.T


def _wprep(w_in):
    depth = w_in.shape[0]
    wt = jnp.swapaxes(w_in, 1, 2)
    src = []
    for dst, start, width in W_IN_SRC:
        assert dst == len(src) * WP_COLS and width % WP_COLS == 0 and start % 8 == 0
        src.extend(r // 8 for r in range(start, start + width, WP_COLS))
    big = pl.pallas_call(
        _wprep_kernel,
        grid_spec=pltpu.PrefetchScalarGridSpec(
            num_scalar_prefetch=1,
            grid=(depth, D_BIG // WP_COLS),
            in_specs=[pl.BlockSpec((pl.Element(1), pl.Element(WP_COLS), pl.Element(D_MODEL)),
                                   lambda l, j, tab: (l, tab[j] * 8, 0))],
            out_specs=pl.BlockSpec((None, D_MODEL, WP_COLS), lambda l, j, tab: (l, 0, j)),
        ),
        out_shape=jax.ShapeDtypeStruct((depth, D_MODEL, D_BIG), BF16),
        compiler_params=_cparams(("arbitrary", "arbitrary")),
        name="wprep",
    )(jnp.asarray(src, jnp.int32), wt)
    small = pl.pallas_call(
        _wsmall_kernel,
        grid=(depth,),
        in_specs=[pl.BlockSpec((pl.Element(1), pl.Element(n), pl.Element(D_MODEL)),
                               functools.partial(lambda l, s: (l, s, 0), s=s))
                  for s, n in W_IN_SMALL_SRC],
        out_specs=pl.BlockSpec((None, D_MODEL, D_SMALL), lambda l: (l, 0, 0)),
        out_shape=jax.ShapeDtypeStruct((depth, D_MODEL, D_SMALL), F32),
        compiler_params=_cparams(("arbitrary",)),
        name="wsmall",
    )(wt, wt)
    return big, small


def kernel(x, c, ctx, c_ctx, ada_w, ada_b, norm_g, final_norm_g, ffn1_wi, ffn1_wo, ffn2_wi, ffn2_wo,
           w_in, ret_logit, gla_wa2, gla_ba, gla_norm_g, conv_w, conv_b, dt_bias, a_log, ssd_d,
           ssd_norm_g, wb_ret, wb_gla, wb_ssd, w_out):
    depth = ada_w.shape[0]
    n_ctx, n_lat = ctx.shape[1], x.shape[1]
    assert x.shape[0] == 1 and n_ctx % TM == 0 and n_lat % TM == 0
    nct = n_ctx // TM
    xa = (ctx[0], x[0])

    cond8 = jnp.zeros((8, D_MODEL), F32).at[0].set(c_ctx).at[1].set(c[0])
    mod = _adaln(cond8, ada_w, ada_b).reshape(depth, 8, N_MOD, D_MODEL)

    cos_t, sin_t = _rope_tables(n_ctx, n_lat)
    w_big, w_small = _wprep(w_in)
    colscale = jnp.ones((1, D_BIG), F32)
    colscale = colscale.at[:, COL_RET + 512:COL_RET + 1024].set(DK_RET ** -0.5)
    colscale = colscale.at[:, COL_GLA:COL_GLA + 512].set(DK_GLA ** -0.5)
    colscale = colscale.at[:, COL_RETG:COL_RETG + D_MODEL].set(0.5)
    colscale = colscale.at[:, COL_GLAR:COL_GLAR + D_MODEL].set(0.5)
    colscale = colscale.at[:, COL_MERGE:COL_MERGE + 3 * D_MODEL].set(0.5)
    colscale = colscale.at[:, COL_Z:COL_Z + D_INNER].set(0.5)

    bf = lambda t: t.astype(BF16)
    ffn1_wi, ffn1_wo, ffn2_wi, ffn2_wo = bf(ffn1_wi), bf(ffn1_wo), bf(ffn2_wi), bf(ffn2_wo)
    wb_ret, wb_gla, wb_ssd, w_out = bf(wb_ret), bf(wb_gla), bf(wb_ssd), bf(w_out)

    logit8 = jnp.broadcast_to(ret_logit.reshape(depth, 2 * H_RET, 1), (depth, 2 * H_RET, 128))
    wa_pad = jnp.zeros((depth, 2, D_SMALL, H_GLA * DK_GLA), F32)
    wa_pad = wa_pad.at[:, 0, SM_AF:SM_AF + GLA_RANK].set(gla_wa2[:, 0])
    wa_pad = wa_pad.at[:, 1, SM_AB:SM_AB + GLA_RANK].set(gla_wa2[:, 1])
    wa_hi = wa_pad.astype(BF16)
    wa_lo = (wa_pad - wa_hi.astype(F32)).astype(BF16)
    ws_hi = w_small.astype(BF16)
    ws_lo = (w_small - ws_hi.astype(F32)).astype(BF16)
    ba = gla_ba.reshape(depth, 2, 1, H_GLA * DK_GLA)
    zpad = jnp.zeros((depth, SM_DTF), F32)
    dtb_full = jnp.concatenate([zpad, dt_bias[:, 0], dt_bias[:, 1], zpad], axis=1).reshape(depth, 1, D_SMALL)
    alog_full = jnp.concatenate([zpad, a_log[:, 0], a_log[:, 1], zpad], axis=1).reshape(depth, 1, D_SMALL)
    conv_w8 = jnp.concatenate([conv_w, jnp.zeros((depth, 8 - CONV_K, D_XBC), F32)], axis=1)
    conv_b3 = conv_b.reshape(depth, 1, D_XBC)
    gng = jnp.tile(gla_norm_g, (1, H_GLA)).reshape(depth, 1, H_GLA * DV_GLA)
    dx = jnp.repeat(ssd_d, SSD_HEADDIM, axis=1).reshape(depth, 1, D_INNER)
    sng = ssd_norm_g.reshape(depth, 1, D_INNER)

    fg = final_norm_g.reshape(1, D_MODEL)
    for l in range(depth):
        xa = _ffn(xa, mod, norm_g, ffn1_wi, ffn1_wo, fg, l, 0, 0, nct)
        p, psm = _inproj(xa, mod, norm_g, cos_t, sin_t, w_big, colscale, ws_hi, ws_lo, l, nct)
        xbc = _conv(p, conv_w8, conv_b3, l, nct)
        yf, yb = _mix_scan(p, psm, xbc, logit8[l], wa_hi[l], wa_lo[l], ba[l],
                           dtb_full[l], alog_full[l], n_ctx)
        xa = _merge(xa, mod, p, xbc, yf, yb, gng, dx, sng,
                    wb_ret, wb_gla, wb_ssd, w_out, norm_g, ffn2_wi, ffn2_wo, fg, l, nct,
                    final=(l == depth - 1))

    return xa[None]
```

```python
import functools
import math

import jax
import jax.numpy as jnp
from jax import lax
from jax.experimental import pallas as pl
from jax.experimental.pallas import tpu as pltpu

F32 = jnp.float32
BF16 = jnp.bfloat16
HIGHEST = lax.Precision.HIGHEST
LOG2E = 1.4426950408889634

D_MODEL = 1024
GRID_W = 64
EPS = 1e-6
N_MOD = 9
D_FF = 2816
H_RET, DK_RET, DV_RET, CHUNK_RET = 4, 128, 256, 128
ROPE_BASE = 10000.0
H_GLA, DK_GLA, DV_GLA, GLA_RANK, CHUNK_GLA = 4, 128, 256, 16, 64
GLA_NORMALIZER = 16.0
GLA_SUB = 16
D_INNER = 2 * D_MODEL
SSD_HEADDIM = 64
H_SSD = D_INNER // SSD_HEADDIM
SSD_GROUPS = 4
D_STATE = 128
CONV_K = 5
CHUNK_SSD = 128
N_BC = SSD_GROUPS * D_STATE
D_XBC = D_INNER + 2 * N_BC

COL_RET = 0
COL_GLA = 2048
D_QKV = 2048
D_Y = 4096
COL_RETG = 4096
COL_GLAR = 5120
COL_XBC = 6144
COL_MERGE = 9216
COL_Z = 12288
D_BIG = 14336
SM_AF, SM_AB, SM_DTF, SM_DTB, D_SMALL = 0, 16, 32, 64, 128

TM = 256
CW = 1024
FF_SPLITS = (0, 1536, D_FF)
VMEM_LIMIT = 56 * 1024 * 1024


def _cparams(sem):
    return pltpu.CompilerParams(dimension_semantics=sem, vmem_limit_bytes=VMEM_LIMIT)


def _resident(shape, index_map):
    return pl.BlockSpec(shape, index_map, pipeline_mode=pl.Buffered(1))


def _sigmoid(x):
    return 0.5 * jnp.tanh(0.5 * x) + 0.5


def _silu(x):
    return x * _sigmoid(x)


def _silu_of_half(h):
    return h * jnp.tanh(h) + h


def _sigmoid_of_half(h):
    return 0.5 * jnp.tanh(h) + 0.5


def _softplus(x):
    return jnp.maximum(x, 0.0) + jnp.log(1.0 + jnp.exp(-jnp.abs(x)))


def _log_sigmoid(x, precise=False):
    e = jnp.exp(-jnp.abs(x))
    return jnp.minimum(x, 0.0) - (jnp.log1p(e) if precise else jnp.log(1.0 + e))


def _rms(x):
    return x * lax.rsqrt(jnp.mean(x * x, axis=-1, keepdims=True) + EPS)


def _pre(x, g, shift, scale):
    return _rms(x) * g * (1.0 + scale) + shift


def _dot(a, b):
    return jnp.dot(a, b, preferred_element_type=F32)


def _dot_exact(a, b):
    return jnp.dot(a, b, precision=HIGHEST, preferred_element_type=F32)


def _split2(x):
    hi = x.astype(BF16)
    return hi, (x - hi.astype(F32)).astype(BF16)


def _split3(x):
    hi = x.astype(BF16)
    r = x - hi.astype(F32)
    mid = r.astype(BF16)
    return hi, mid, (r - mid.astype(F32)).astype(BF16)


def _dot_sel(m, x):
    hi, mid, lo = _split3(x)
    return _dot(m, hi) + _dot(m, mid) + _dot(m, lo)


def _dot_split(x, w_hi, w_lo):
    x_hi, x_lo = _split2(x)
    return _dot(x_hi, w_hi) + _dot(x_lo, w_hi) + _dot(x_hi, w_lo)


def _dot_nt(a, b):
    return lax.dot_general(a, b, (((1,), (1,)), ((), ())), preferred_element_type=F32)


def _dot_tn(a, b):
    return lax.dot_general(a, b, (((0,), (0,)), ((), ())), preferred_element_type=F32)


N_COND = 2
LANES = 128


def _adaln_kernel(cond_ref, w_ref, b_ref, o_ref, s_ref):
    s_ref[...] = _silu(cond_ref[...])
    reps = D_MODEL // LANES

    def body(k8, accs):
        r0 = pl.multiple_of(k8 * 8, 8)
        wk = w_ref[pl.ds(r0, 8), :]
        return tuple(acc + wk * jnp.concatenate([s_ref[r, pl.ds(r0, 8), :]] * reps, axis=1)
                     for r, acc in enumerate(accs))

    zero = jnp.zeros((8, D_MODEL), F32)
    accs = lax.fori_loop(0, D_MODEL // 8, body, (zero,) * N_COND)
    rows = [jnp.sum(a, axis=0, keepdims=True) + b_ref[...] for a in accs]
    o_ref[...] = jnp.concatenate(rows + [jnp.zeros((8 - N_COND, D_MODEL), F32)], axis=0)


def _adaln(cond_b, ada_w, ada_b):
    depth = ada_w.shape[0]
    nblk = N_MOD
    return pl.pallas_call(
        _adaln_kernel,
        grid=(depth, nblk),
        in_specs=[
            pl.BlockSpec((N_COND, D_MODEL, LANES), lambda l, j: (0, 0, 0)),
            pl.BlockSpec((None, D_MODEL, D_MODEL), lambda l, j: (l, 0, j)),
            pl.BlockSpec((None, 1, D_MODEL), lambda l, j: (l, 0, j)),
        ],
        out_specs=pl.BlockSpec((None, 8, D_MODEL), lambda l, j: (l, 0, j)),
        out_shape=jax.ShapeDtypeStruct((depth, 8, N_MOD * D_MODEL), F32),
        scratch_shapes=[pltpu.VMEM((N_COND, D_MODEL, LANES), F32)],
        compiler_params=_cparams(("arbitrary", "arbitrary")),
        name="adaln",
    )(cond_b, ada_w, ada_b.reshape(depth, 1, N_MOD * D_MODEL))


def _ffn_kernel(x_ref, mod_ref, g_ref, wa_ref, wu_ref, wo_ref, fg_ref, o_ref, *, k0, gi, final):
    _ffn_body(x_ref[...], mod_ref, g_ref, wa_ref, wu_ref, wo_ref, fg_ref, o_ref, k0, gi, final)


def _ffn_body(x, mod_ref, g_ref, wa_ref, wu_ref, wo_ref, fg_ref, o_ref, k0, gi, final):
    h = _pre(x, g_ref[gi:gi + 1, :], mod_ref[0, k0:k0 + 1, :], mod_ref[0, k0 + 1:k0 + 2, :])
    hb = h.astype(BF16)
    out = None
    for lo, hi in zip(FF_SPLITS[:-1], FF_SPLITS[1:]):
        sl = slice(lo, hi)
        a = _dot(hb, wa_ref[:, sl])
        u = _dot(hb, wu_ref[:, sl])
        t = (_silu(a) * u).astype(BF16)
        part = _dot(t, wo_ref[sl, :])
        out = part if out is None else out + part
    y = x + 0.5 * mod_ref[0, k0 + 2:k0 + 3, :] * out
    o_ref[...] = _rms(y) * fg_ref[...] if final else y


def _ffn_split_kernel(ctx_ref, lat_ref, mod_ref, g_ref, wa_ref, wu_ref, wo_ref, fg_ref, o_ref,
                      *, k0, gi, nct):
    x = jnp.where(pl.program_id(0) >= nct, lat_ref[...], ctx_ref[...])
    _ffn_body(x, mod_ref, g_ref, wa_ref, wu_ref, wo_ref, fg_ref, o_ref, k0, gi, False)


def _ffn(xs, mod, norm_g, wi, wo, final_g, l, k0, gi, nct):
    split = isinstance(xs, tuple)
    n = xs[0].shape[0] + xs[1].shape[0] if split else xs.shape[0]
    if split:
        kern = functools.partial(_ffn_split_kernel, k0=k0, gi=gi, nct=nct)
        x_specs = [pl.BlockSpec((TM, D_MODEL), lambda i: (jnp.minimum(i, nct - 1), 0)),
                   pl.BlockSpec((TM, D_MODEL), lambda i: (jnp.maximum(i - nct, 0), 0))]
        x_args = list(xs)
    else:
        kern = functools.partial(_ffn_kernel, k0=k0, gi=gi, final=False)
        x_specs = [pl.BlockSpec((TM, D_MODEL), lambda i: (i, 0))]
        x_args = [xs]
    return pl.pallas_call(
        kern,
        grid=(n // TM,),
        in_specs=x_specs + [
            pl.BlockSpec((None, 1, N_MOD, D_MODEL),
                         lambda i: (l, (i >= nct).astype(jnp.int32), 0, 0)),
            _resident((None, 3, D_MODEL), lambda i: (l, 0, 0)),
            _resident((None, D_MODEL, D_FF), lambda i: (l, 0, 0)),
            _resident((None, D_MODEL, D_FF), lambda i: (l, 0, 1)),
            _resident((None, D_FF, D_MODEL), lambda i: (l, 0, 0)),
            _resident((1, D_MODEL), lambda i: (0, 0)),
        ],
        out_specs=pl.BlockSpec((TM, D_MODEL), lambda i: (i, 0)),
        out_shape=jax.ShapeDtypeStruct((n, D_MODEL), F32),
        compiler_params=_cparams(("arbitrary",)),
        name="ffn",
    )(*x_args, mod, norm_g, wi, wi, wo, final_g)


def _inproj_kernel(x_ref, mod_ref, g_ref, cos_ref, sin_ref, w_ref, cs_ref, wsh_ref, wsl_ref,
                   p_ref, ps_ref):
    h = _pre(x_ref[...], g_ref[1:2, :], mod_ref[0, 3:4, :], mod_ref[0, 4:5, :])
    hb = h.astype(BF16)
    hl = _dot(hb, wsh_ref[...])
    ps_ref[...] = (hl[:, 0:D_SMALL] + hl[:, D_SMALL:2 * D_SMALL]
                   + _dot((h - hb.astype(F32)).astype(BF16), wsl_ref[...]))
    for j in range(D_BIG // CW):
        cols = slice(j * CW, (j + 1) * CW)
        acc = _dot(hb, w_ref[:, cols]) * cs_ref[:, cols]
        if j == 0:
            cos = cos_ref[...]
            sin = sin_ref[...]
            for b in range(CW // DK_RET):
                t = acc[:, b * DK_RET:(b + 1) * DK_RET]
                p_ref[:, b * DK_RET:(b + 1) * DK_RET] = (
                    t * cos + pltpu.roll(t, DK_RET // 2, 1) * sin).astype(BF16)
        else:
            p_ref[:, cols] = acc.astype(BF16)


def _inproj(xa, mod, norm_g, cos_t, sin_t, w_big, colscale, ws_hi, ws_lo, l, nct):
    n = xa.shape[0]
    return pl.pallas_call(
        _inproj_kernel,
        grid=(n // TM,),
        in_specs=[
            pl.BlockSpec((TM, D_MODEL), lambda i: (i, 0)),
            pl.BlockSpec((None, 1, N_MOD, D_MODEL), lambda i: (l, (i >= nct).astype(jnp.int32), 0, 0)),
            _resident((None, 3, D_MODEL), lambda i: (l, 0, 0)),
            pl.BlockSpec((TM, DK_RET), lambda i: (i, 0)),
            pl.BlockSpec((TM, DK_RET), lambda i: (i, 0)),
            _resident((None, D_MODEL, D_BIG), lambda i: (l, 0, 0)),
            _resident((1, D_BIG), lambda i: (0, 0)),
            _resident((None, D_MODEL, 2 * D_SMALL), lambda i: (l, 0, 0)),
            _resident((None, D_MODEL, D_SMALL), lambda i: (l, 0, 0)),
        ],
        out_specs=[
            pl.BlockSpec((TM, D_BIG), lambda i: (i, 0)),
            pl.BlockSpec((TM, D_SMALL), lambda i: (i, 0)),
        ],
        out_shape=[
            jax.ShapeDtypeStruct((n, D_BIG), BF16),
            jax.ShapeDtypeStruct((n, D_SMALL), F32),
        ],
        compiler_params=_cparams(("arbitrary",)),
        name="inproj",
    )(xa, mod, norm_g, cos_t, sin_t, w_big, colscale, jnp.concatenate([ws_hi, ws_lo], axis=-1), ws_hi)


CONV_STRIP = 512
CONV_HALO = 16
CONV_BLK = 128


def _conv_kernel(cur_ref, prev_ref, next_ref, w_ref, b_ref, o_ref, ext_ref, *, nct, ntiles):
    i = pl.program_id(0)
    has_prev = jnp.logical_and(i != 0, i != nct)
    has_next = jnp.logical_and(i != nct - 1, i != ntiles - 1)
    halo0 = jnp.zeros((CONV_HALO, D_XBC), BF16)
    ext_ref[0:CONV_HALO, :] = jnp.where(has_prev, prev_ref[...], halo0)
    ext_ref[CONV_HALO:CONV_HALO + TM, :] = cur_ref[...]
    ext_ref[CONV_HALO + TM:2 * CONV_HALO + TM, :] = jnp.where(has_next, next_ref[...], halo0)
    kwin = CONV_BLK + 2 * CONV_HALO
    rr = lax.broadcasted_iota(jnp.int32, (CONV_BLK, kwin), 0)
    cc = lax.broadcasted_iota(jnp.int32, (CONV_BLK, kwin), 1)
    mid = CONV_K // 2
    shift = {k: (cc == rr + CONV_HALO + k - mid).astype(F32).astype(BF16)
             for k in range(CONV_K) if k != mid}
    for blk in range(TM // CONV_BLK):
        rows = slice(blk * CONV_BLK, (blk + 1) * CONV_BLK)
        for s in range(D_XBC // CONV_STRIP):
            cs = slice(s * CONV_STRIP, (s + 1) * CONV_STRIP)
            win = ext_ref[blk * CONV_BLK:blk * CONV_BLK + kwin, cs]
            acc = b_ref[:, cs] + w_ref[mid:mid + 1, cs] * cur_ref[rows, cs].astype(F32)
            for k in range(CONV_K):
                if k != mid:
                    acc = acc + w_ref[k:k + 1, cs] * _dot(shift[k], win)
            o_ref[rows, cs] = _silu(acc).astype(BF16)


def _conv(p, conv_w, conv_b, l, nct):
    n = p.shape[0]
    ntiles = n // TM
    rh = TM // CONV_HALO
    nblkh = n // CONV_HALO
    cb = COL_XBC // D_XBC
    return pl.pallas_call(
        functools.partial(_conv_kernel, nct=nct, ntiles=ntiles),
        grid=(ntiles,),
        in_specs=[
            pl.BlockSpec((TM, D_XBC), lambda i: (i, cb)),
            pl.BlockSpec((CONV_HALO, D_XBC), lambda i: (jnp.maximum(i * rh - 1, 0), cb)),
            pl.BlockSpec((CONV_HALO, D_XBC), lambda i: (jnp.minimum((i + 1) * rh, nblkh - 1), cb)),
            pl.BlockSpec((None, 8, D_XBC), lambda i: (l, 0, 0)),
            pl.BlockSpec((None, 1, D_XBC), lambda i: (l, 0, 0)),
        ],
        out_specs=pl.BlockSpec((TM, D_XBC), lambda i: (i, 0)),
        out_shape=jax.ShapeDtypeStruct((n, D_XBC), BF16),
        scratch_shapes=[pltpu.VMEM((TM + 2 * CONV_HALO, D_XBC), BF16)],
        compiler_params=_cparams(("arbitrary",)),
        name="conv",
    )(p, p, p, conv_w, conv_b)


def _bwd_chunk(t, ncc, n):
    return jnp.where(t < ncc, ncc - 1 - t, n + ncc - 1 - t)


def _ret_init(lg_ref, s_ref, dm_ref, rs_ref, ks_ref, cd_ref):
    c = CHUNK_RET
    s_ref[...] = jnp.zeros(s_ref.shape, F32)
    la = _log_sigmoid(lg_ref[...], precise=True)
    ii = lax.broadcasted_iota(jnp.int32, (c, c), 0)
    jj = lax.broadcasted_iota(jnp.int32, (c, c), 1)
    ir = lax.broadcasted_iota(jnp.int32, (c, 1), 0)
    for d in range(2):
        for h in range(H_RET):
            a = la[d * H_RET + h:d * H_RET + h + 1, :]
            a2 = jnp.concatenate([a, a], axis=1)
            if d == 0:
                dist, keep = ii - jj, ii >= jj
                rpow = ir + 1
                kpow = c - 1 - ir
            else:
                dist, keep = jj - ii, jj >= ii
                rpow = c - ir
                kpow = ir
            dm_ref[d, h] = jnp.where(keep, jnp.exp(dist.astype(F32) * a), 0.0)
            rs_ref[d, h] = jnp.exp(rpow.astype(F32) * a2)
            ks_ref[d, h] = jnp.exp(kpow.astype(F32) * a)
            cd_ref[d, h] = jnp.exp(float(c) * jnp.broadcast_to(a2, (8, DV_RET)))


def _ret_phases(qkf_ref, vf_ref, yf_ref, qkb_ref, vb_ref, yb_ref, s_ref, dm_ref, rs_ref, ks_ref, cd_ref):
    dirs = ((qkf_ref, vf_ref, yf_ref), (qkb_ref, vb_ref, yb_ref))

    def q_of(qk_ref, h):
        return qk_ref[:, h * DK_RET:(h + 1) * DK_RET]

    def k_of(qk_ref, h):
        return qk_ref[:, H_RET * DK_RET + h * DK_RET:H_RET * DK_RET + (h + 1) * DK_RET]

    qk = [[_dot_nt(q_of(qk_ref, h), k_of(qk_ref, h)) for h in range(H_RET)] for qk_ref, _, _ in dirs]
    qs = [[_dot(q_of(qk_ref, h), s_ref[d, h].astype(BF16)) for h in range(H_RET)]
          for d, (qk_ref, _, _) in enumerate(dirs)]
    def outputs():
        for d, (qk_ref, v_ref, y_ref) in enumerate(dirs):
            for h in range(H_RET):
                v = v_ref[:, h * DV_RET:(h + 1) * DV_RET]
                att = (qk[d][h] * dm_ref[d, h]).astype(BF16)
                y = rs_ref[d, h] * qs[d][h] + _dot(att, v)
                y_ref[:, h * DV_RET:(h + 1) * DV_RET] = y.astype(y_ref.dtype)

    def states():
        for d, (qk_ref, v_ref, y_ref) in enumerate(dirs):
            for h in range(H_RET):
                v = v_ref[:, h * DV_RET:(h + 1) * DV_RET]
                kt = (k_of(qk_ref, h).astype(F32) * ks_ref[d, h]).astype(BF16)
                s_ref[d, h] = cd_ref[d, h][0:1, :] * s_ref[d, h] + _dot_tn(kt, v)

    return outputs, states


GLA_BLOCK = 128


def _gla_dir(d, qk_ref, v_ref, ps_ref, wah_ref, wal_ref, ba_ref, y_ref, st_ref):
    c = CHUNK_GLA
    nb = GLA_BLOCK
    nsub = c // GLA_SUB
    bi = lax.broadcasted_iota(jnp.int32, (nb, nb), 0)
    bj = lax.broadcasted_iota(jnp.int32, (nb, nb), 1)
    same = (bi // c) == (bj // c)
    sub_start = jnp.bitwise_and(bi, -GLA_SUB)
    if d == 0:
        cum = jnp.logical_and(same, bj <= bi)
        ref_m = jnp.logical_and(same, bj < sub_start)
    else:
        cum = jnp.logical_and(same, bj >= bi)
        ref_m = jnp.logical_and(same, bj >= sub_start + GLA_SUB)
    tri = jnp.concatenate([cum, ref_m], axis=0).astype(F32).astype(BF16)
    z = _dot_split(ps_ref[...], wah_ref[d], wal_ref[d]) + ba_ref[d]
    a = _log_sigmoid(z) * (1.0 / GLA_NORMALIZER)
    gr = _dot_sel(tri, a)
    ii = lax.broadcasted_iota(jnp.int32, (c, c), 0)
    jj = lax.broadcasted_iota(jnp.int32, (c, c), 1)
    causal = (jj <= ii) if d == 0 else (jj >= ii)
    end_row = c - 1 if d == 0 else 0
    col_blk = jnp.right_shift(jj, int(math.log2(GLA_SUB)))

    def sub_rows(jb):
        return slice(jb * GLA_SUB, c) if d == 0 else slice(0, (jb + 1) * GLA_SUB)

    order = list(range(nb // c)) if d == 0 else list(range(nb // c - 1, -1, -1))
    pre = {}
    for ci in order:
        rows = slice(ci * c, (ci + 1) * c)
        g_all = gr[ci * c:(ci + 1) * c]
        r_all = gr[nb + ci * c:nb + (ci + 1) * c]
        for h in range(H_GLA):
            ks = slice(h * DK_GLA, (h + 1) * DK_GLA)
            q = qk_ref[rows, h * DK_GLA:(h + 1) * DK_GLA].astype(F32)
            k = qk_ref[rows, H_GLA * DK_GLA + h * DK_GLA:H_GLA * DK_GLA + (h + 1) * DK_GLA].astype(F32)
            g = g_all[:, ks]
            r = r_all[:, ks]
            g_end = g[end_row:end_row + 1, :]
            kt = (k * jnp.exp(r - g)).astype(BF16)
            qs = []
            for jb in range(nsub):
                rj = r[jb * GLA_SUB:jb * GLA_SUB + 1, :]
                need = sub_rows(jb)
                qs.append((q[need] * jnp.exp(jnp.minimum(g[need] - rj, 0.0))).astype(BF16))
            pm = _dot_nt(jnp.concatenate(qs, axis=0), kt)
            qg = (q * jnp.exp(g)).astype(BF16)
            ke = (k * jnp.exp(g_end - g)).astype(BF16)
            pre[(ci, h)] = (pm, qg, ke, jnp.exp(g_end))

    def finish(ci, h):
        rows = slice(ci * c, (ci + 1) * c)
        pm, qg, ke, e_end = pre[(ci, h)]
        v = v_ref[rows, h * DV_GLA:(h + 1) * DV_GLA]
        att = jnp.zeros((c, c), F32)
        off = 0
        for jb in range(nsub):
            need = sub_rows(jb)
            nr = need.stop - need.start
            pads = [jnp.zeros((need.start, c), F32), pm[off:off + nr], jnp.zeros((c - need.stop, c), F32)]
            piece = jnp.concatenate([t for t in pads if t.shape[0]], axis=0)
            att = jnp.where(col_blk == jb, piece, att)
            off += nr
        att = jnp.where(causal, att, 0.0).astype(BF16)
        st = st_ref[d, h]
        y = _dot_nt(qg, st.astype(BF16)) + _dot(att, v)
        y_ref[rows, h * DV_GLA:(h + 1) * DV_GLA] = y.astype(y_ref.dtype)
        st_ref[d, h] = e_end * st + _dot_tn(v, ke)

    return [[functools.partial(finish, ci, h) for h in range(H_GLA)] for ci in order]


def _ssd_dir(d, xbc_ref, ps_ref, dtb_ref, alog_ref, y_ref, s_ref, e_ref):
    c = CHUNK_SSD
    base = SM_DTF if d == 0 else SM_DTB
    ii = lax.broadcasted_iota(jnp.int32, (c, c), 0)
    jj = lax.broadcasted_iota(jnp.int32, (c, c), 1)
    keep = (jj <= ii) if d == 0 else (jj >= ii)
    end_row = c - 1 if d == 0 else 0
    lane = lax.broadcasted_iota(jnp.int32, (1, D_SMALL), 1)
    in_dir = jnp.logical_and(lane >= base, lane < base + H_SSD)
    dt = jnp.where(in_dir, _softplus(ps_ref[...] + dtb_ref[...]), 0.0)
    a = dt * jnp.where(in_dir, -jnp.exp(alog_ref[...]), 0.0)
    g = _dot_sel(keep.astype(F32).astype(BF16), a)
    g2 = g * LOG2E
    gt = ((g - jnp.where(in_dir, jnp.log(dt), 0.0)) * LOG2E).T
    g_end = g[end_row:end_row + 1, :]
    f_state = dt * jnp.exp(g_end - g)
    f_y = jnp.exp(g)
    f_end = jnp.broadcast_to(jnp.exp(g_end), (8, D_SMALL))
    fs_hi, fs_lo = _split2(jnp.concatenate([f_state, f_y, f_end], axis=0))
    fx = _dot(jnp.concatenate([fs_hi, fs_lo], axis=1), e_ref[d])
    wv = (xbc_ref[:, 0:D_INNER].astype(F32) * fx[0:c]).astype(BF16)
    egx = fx[c:2 * c]
    eex = fx[2 * c:2 * c + 1]
    lane2 = lax.broadcasted_iota(jnp.int32, (c, 2 * SSD_HEADDIM), 1)
    hpg = H_SSD // SSD_GROUPS
    gw = hpg * SSD_HEADDIM
    def b_of(grp):
        return xbc_ref[:, D_INNER + grp * D_STATE:D_INNER + (grp + 1) * D_STATE]

    def c_of(grp):
        return xbc_ref[:, D_INNER + N_BC + grp * D_STATE:D_INNER + N_BC + (grp + 1) * D_STATE]

    cbs = [_dot_nt(c_of(grp), b_of(grp)).astype(BF16) for grp in range(SSD_GROUPS)]
    yis = [_dot(c_of(grp), s_ref[d, grp].astype(BF16)) for grp in range(SSD_GROUPS)]

    def group(grp):
        bg = b_of(grp)
        cb = cbs[grp]
        sg = s_ref[d, grp]
        yi = yis[grp]
        for pr in range(hpg // 2):
            h0 = grp * hpg + 2 * pr
            cols = slice(h0 * SSD_HEADDIM, (h0 + 2) * SSD_HEADDIM)
            atts = []
            for hh in (h0, h0 + 1):
                col = base + hh
                diff = g2[:, col:col + 1] - gt[col:col + 1, :]
                dec = jnp.exp2(jnp.where(keep, diff, -jnp.inf))
                atts.append(cb * dec.astype(BF16))
            v2 = xbc_ref[:, cols]
            vv = jnp.concatenate([jnp.where(lane2 < SSD_HEADDIM, v2, jnp.zeros_like(v2)),
                                  jnp.where(lane2 >= SSD_HEADDIM, v2, jnp.zeros_like(v2))], axis=0)
            y2 = _dot(jnp.concatenate(atts, axis=1), vv)
            yo = yi[:, pr * 2 * SSD_HEADDIM:(pr + 1) * 2 * SSD_HEADDIM] * egx[:, cols] + y2
            y_ref[:, cols] = yo.astype(y_ref.dtype)
        gcols = slice(grp * gw, (grp + 1) * gw)
        s_ref[d, grp] = eex[:, gcols] * sg + _dot_tn(bg, wv[:, gcols])

    return [functools.partial(group, grp) for grp in range(SSD_GROUPS)]


def _ssd_init(s_ref, e_ref):
    s_ref[...] = jnp.zeros(s_ref.shape, F32)
    row = lax.broadcasted_iota(jnp.int32, (2 * D_SMALL, D_INNER), 0) % D_SMALL
    head = lax.broadcasted_iota(jnp.int32, (2 * D_SMALL, D_INNER), 1) // SSD_HEADDIM
    e_ref[0] = (row - SM_DTF == head).astype(F32).astype(BF16)
    e_ref[1] = (row - SM_DTB == head).astype(F32).astype(BF16)


def _mix_kernel(lg_ref, wah_ref, wal_ref, ba_ref, dtb_ref, alog_ref,
                qkv_f, ps_f, xbc_f, qkv_b, ps_b, xbc_b, y_f, y_b,
                rs_s, r_dm, r_rs, r_ks, r_cd, g_st, s_st, s_e):
    @pl.when(pl.program_id(0) == 0)
    def _():
        _ret_init(lg_ref, rs_s, r_dm, r_rs, r_ks, r_cd)
        g_st[...] = jnp.zeros(g_st.shape, F32)
        _ssd_init(s_st, s_e)

    w = H_RET * DV_RET

    def split_in(ref):
        return (ref.at[:, COL_RET:COL_RET + w], ref.at[:, COL_RET + w:COL_RET + 2 * w],
                ref.at[:, COL_GLA:COL_GLA + w], ref.at[:, COL_GLA + w:COL_GLA + 2 * w])

    def split_out(ref):
        return ref.at[:, 0:w], ref.at[:, w:2 * w], ref.at[:, 2 * w:2 * w + D_INNER]

    rqk_f, rv_f, gqk_f, gv_f = split_in(qkv_f)
    rqk_b, rv_b, gqk_b, gv_b = split_in(qkv_b)
    yr_f, yg_f, ys_f = split_out(y_f)
    yr_b, yg_b, ys_b = split_out(y_b)

    ret_out, ret_state = _ret_phases(rqk_f, rv_f, yr_f, rqk_b, rv_b, yr_b, rs_s, r_dm, r_rs, r_ks, r_cd)
    gla_f = _gla_dir(0, gqk_f, gv_f, ps_f, wah_ref, wal_ref, ba_ref, yg_f, g_st)
    gla_b = _gla_dir(1, gqk_b, gv_b, ps_b, wah_ref, wal_ref, ba_ref, yg_b, g_st)
    ssd_f = _ssd_dir(0, xbc_f, ps_f, dtb_ref, alog_ref, ys_f, s_st, s_e)
    ssd_b = _ssd_dir(1, xbc_b, ps_b, dtb_ref, alog_ref, ys_b, s_st, s_e)
    ret_out()
    for fn in gla_f[0] + gla_b[0] + ssd_f:
        fn()
    ret_state()
    for fn in gla_f[1] + gla_b[1] + ssd_b:
        fn()


def _mix_scan(p, psm, xbc, logit8, wa_hi, wa_lo, ba, dtb_full, alog_full, ncc_rows):
    n = p.shape[0]
    c = CHUNK_RET
    assert c == GLA_BLOCK == CHUNK_SSD
    nch = n // c
    ncc = ncc_rows // c
    bwd = lambda t: _bwd_chunk(t, ncc, nch)
    assert COL_RET == 0 and COL_GLA == D_QKV
    const = lambda shape: pl.BlockSpec(shape, lambda t: (0,) * len(shape))

    def blocks(idx):
        return [
            pl.BlockSpec((c, 2 * D_QKV), lambda t: (idx(t), 0)),
            pl.BlockSpec((c, D_SMALL), lambda t: (idx(t), 0)),
            pl.BlockSpec((c, D_XBC), lambda t: (idx(t), 0)),
        ]

    fwd = lambda t: t
    out_w = (D_Y, D_Y)
    out_idx = (fwd, bwd)
    return pl.pallas_call(
        _mix_kernel,
        grid=(nch,),
        in_specs=[
            const((8, 128)),
            const((2, D_SMALL, H_GLA * DK_GLA)),
            const((2, D_SMALL, H_GLA * DK_GLA)),
            const((2, 1, H_GLA * DK_GLA)),
            const((1, D_SMALL)),
            const((1, D_SMALL)),
        ] + blocks(fwd) + blocks(bwd),
        out_specs=[pl.BlockSpec((c, ow), functools.partial(lambda t, f: (f(t), 0), f=f))
                   for ow, f in zip(out_w, out_idx)],
        out_shape=[jax.ShapeDtypeStruct((n, ow), BF16) for ow in out_w],
        scratch_shapes=[
            pltpu.VMEM((2, H_RET, DK_RET, DV_RET), F32),
            pltpu.VMEM((2, H_RET, c, c), F32),
            pltpu.VMEM((2, H_RET, c, DV_RET), F32),
            pltpu.VMEM((2, H_RET, c, DK_RET), F32),
            pltpu.VMEM((2, H_RET, 8, DV_RET), F32),
            pltpu.VMEM((2, H_GLA, DV_GLA, DK_GLA), F32),
            pltpu.VMEM((2, SSD_GROUPS, D_STATE, D_INNER // SSD_GROUPS), F32),
            pltpu.VMEM((2, 2 * D_SMALL, D_INNER), BF16),
        ],
        compiler_params=_cparams(("arbitrary",)),
        name="mix_scan",
    )(logit8, wa_hi, wa_lo, ba, dtb_full, alog_full, p, psm, xbc, p, psm, xbc)


def _segnorm(y, width):
    outs = [_rms(y[:, s * width:(s + 1) * width]) for s in range(y.shape[1] // width)]
    return jnp.concatenate(outs, axis=1)


def _merge_kernel(x_ref, mod_ref, rg_ref, gr_ref, mg_ref, z_ref, xs_ref, yf_ref, yb_ref,
                  gng_ref, dx_ref, sng_ref, wbr_ref, wbg_ref, wbs_ref, wo_ref,
                  ng_ref, wa_ref, wu_ref, wf_ref, fg_ref, o_ref, *, final):
    c_r, c_g, c_s = slice(0, D_MODEL), slice(D_MODEL, 2 * D_MODEL), slice(2 * D_MODEL, D_Y)
    yr = (yf_ref[:, c_r] + yb_ref[:, c_r]).astype(F32)
    br = _segnorm(yr, DV_RET) * _silu_of_half(rg_ref[...].astype(F32))
    b_ret = _dot(br.astype(BF16), wbr_ref[...])
    yg = (yf_ref[:, c_g] + yb_ref[:, c_g]).astype(F32)
    bg = _segnorm(yg, DV_GLA) * gng_ref[...] * _silu_of_half(gr_ref[...].astype(F32))
    b_gla = _dot(bg.astype(BF16), wbg_ref[...])
    ys = (yf_ref[:, c_s] + yb_ref[:, c_s]).astype(F32)
    ys = (ys + dx_ref[...] * xs_ref[...].astype(F32)) * _silu_of_half(z_ref[...].astype(F32))
    bs = _segnorm(ys, D_INNER // SSD_GROUPS) * sng_ref[...]
    b_ssd = _dot(bs.astype(BF16), wbs_ref[...])
    gates = _sigmoid_of_half(mg_ref[...].astype(F32))
    mix = (gates[:, 0:D_MODEL] * b_ret + gates[:, D_MODEL:2 * D_MODEL] * b_gla
           + gates[:, 2 * D_MODEL:3 * D_MODEL] * b_ssd)
    out = _dot(mix.astype(BF16), wo_ref[...])
    x_mid = x_ref[...] + mod_ref[0, 5:6, :] * out
    _ffn_body(x_mid, mod_ref, ng_ref, wa_ref, wu_ref, wf_ref, fg_ref, o_ref, 6, 2, final)


def _merge(xa, mod, p, xbc, yf, yb, gng, dx, sng, wbr, wbg, wbs, wout,
           norm_g, wi, wf, final_g, l, nct, final):
    n = xa.shape[0]
    skip = nct if final else 0
    ntiles = n // TM - skip
    row = lambda w, cb: pl.BlockSpec((TM, w), lambda i: (i + skip, cb))
    return pl.pallas_call(
        functools.partial(_merge_kernel, final=final),
        grid=(ntiles,),
        in_specs=[
            row(D_MODEL, 0),
            pl.BlockSpec((None, 1, N_MOD, D_MODEL),
                         lambda i: (l, (i + skip >= nct).astype(jnp.int32), 0, 0)),
            row(D_MODEL, COL_RETG // D_MODEL),
            row(D_MODEL, COL_GLAR // D_MODEL),
            row(3 * D_MODEL, COL_MERGE // (3 * D_MODEL)),
            row(D_INNER, COL_Z // D_INNER),
            row(D_INNER, 0),
            row(D_Y, 0), row(D_Y, 0),
            _resident((None, 1, D_MODEL), lambda i: (l, 0, 0)),
            _resident((None, 1, D_INNER), lambda i: (l, 0, 0)),
            _resident((None, 1, D_INNER), lambda i: (l, 0, 0)),
            _resident((None, D_MODEL, D_MODEL), lambda i: (l, 0, 0)),
            _resident((None, D_MODEL, D_MODEL), lambda i: (l, 0, 0)),
            _resident((None, D_INNER, D_MODEL), lambda i: (l, 0, 0)),
            _resident((None, D_MODEL, D_MODEL), lambda i: (l, 0, 0)),
            _resident((None, 3, D_MODEL), lambda i: (l, 0, 0)),
            _resident((None, D_MODEL, D_FF), lambda i: (l, 0, 0)),
            _resident((None, D_MODEL, D_FF), lambda i: (l, 0, 1)),
            _resident((None, D_FF, D_MODEL), lambda i: (l, 0, 0)),
            _resident((1, D_MODEL), lambda i: (0, 0)),
        ],
        out_specs=pl.BlockSpec((TM, D_MODEL), lambda i: (i, 0)),
        out_shape=jax.ShapeDtypeStruct((ntiles * TM, D_MODEL), F32),
        compiler_params=_cparams(("arbitrary",)),
        name="merge",
    )(xa, mod, p, p, p, p, xbc, yf, yb, gng, dx, sng, wbr, wbg, wbs, wout,
      norm_g, wi, wi, wf, final_g)


def _rope_tables(n_ctx, n_lat):
    rows = n_lat // GRID_W
    nf = DK_RET // 4
    freq = ROPE_BASE ** (-jnp.arange(nf, dtype=F32) / nf)
    ang_r = jnp.arange(rows, dtype=F32)[:, None] * freq
    ang_c = jnp.arange(GRID_W, dtype=F32)[:, None] * freq

    def table(fn):
        tr = jnp.broadcast_to(fn(ang_r)[:, None, :], (rows, GRID_W, nf))
        tc = jnp.broadcast_to(fn(ang_c)[None, :, :], (rows, GRID_W, nf))
        return jnp.concatenate([tr, tc], axis=-1).reshape(n_lat, 2 * nf)

    cos, sin = table(jnp.cos), table(jnp.sin)
    cos_t = jnp.concatenate([cos, cos], axis=1)
    sin_t = jnp.concatenate([-sin, sin], axis=1)
    cos_t = jnp.concatenate([jnp.ones((n_ctx, DK_RET), F32), cos_t], axis=0)
    sin_t = jnp.concatenate([jnp.zeros((n_ctx, DK_RET), F32), sin_t], axis=0)
    return cos_t, sin_t


WP_COLS = 1024
W_IN_SRC = ((COL_RET, 0, D_QKV), (COL_GLA, 3072, D_QKV), (COL_RETG, 2048, D_MODEL),
            (COL_GLAR, 5120, D_MODEL), (COL_XBC, 8224, D_XBC),
            (COL_MERGE, 11360, 3 * D_MODEL), (COL_Z, 6176, D_INNER))
W_IN_SMALL_SRC = ((6144, 2 * GLA_RANK), (11296, 2 * H_SSD))


def _wprep_kernel(src_ref, wt_ref, o_ref):
    o_ref[...] = wt_ref[0].T.astype(BF16)


def _wsmall_kernel(a_ref, d_ref, o_ref):
    used = a_ref.shape[1] + d_ref.shape[1]
    rows = jnp.concatenate([a_ref[0], d_ref[0], jnp.zeros((D_SMALL - used, D_MODEL), F32)], axis=0)
    o_ref[...] = rows.T


def _wprep(w_in):
    depth = w_in.shape[0]
    wt = jnp.swapaxes(w_in, 1, 2)
    src = []
    for dst, start, width in W_IN_SRC:
        assert dst == len(src) * WP_COLS and width % WP_COLS == 0 and start % 8 == 0
        src.extend(r // 8 for r in range(start, start + width, WP_COLS))
    big = pl.pallas_call(
        _wprep_kernel,
        grid_spec=pltpu.PrefetchScalarGridSpec(
            num_scalar_prefetch=1,
            grid=(depth, D_BIG // WP_COLS),
            in_specs=[pl.BlockSpec((pl.Element(1), pl.Element(WP_COLS), pl.Element(D_MODEL)),
                                   lambda l, j, tab: (l, tab[j] * 8, 0))],
            out_specs=pl.BlockSpec((None, D_MODEL, WP_COLS), lambda l, j, tab: (l, 0, j)),
        ),
        out_shape=jax.ShapeDtypeStruct((depth, D_MODEL, D_BIG), BF16),
        compiler_params=_cparams(("arbitrary", "arbitrary")),
        name="wprep",
    )(jnp.asarray(src, jnp.int32), wt)
    small = pl.pallas_call(
        _wsmall_kernel,
        grid=(depth,),
        in_specs=[pl.BlockSpec((pl.Element(1), pl.Element(n), pl.Element(D_MODEL)),
                               functools.partial(lambda l, s: (l, s, 0), s=s))
                  for s, n in W_IN_SMALL_SRC],
        out_specs=pl.BlockSpec((None, D_MODEL, D_SMALL), lambda l: (l, 0, 0)),
        out_shape=jax.ShapeDtypeStruct((depth, D_MODEL, D_SMALL), F32),
        compiler_params=_cparams(("arbitrary",)),
        name="wsmall",
    )(wt, wt)
    return big, small


def kernel(x, c, ctx, c_ctx, ada_w, ada_b, norm_g, final_norm_g, ffn1_wi, ffn1_wo, ffn2_wi, ffn2_wo,
           w_in, ret_logit, gla_wa2, gla_ba, gla_norm_g, conv_w, conv_b, dt_bias, a_log, ssd_d,
           ssd_norm_g, wb_ret, wb_gla, wb_ssd, w_out):
    depth = ada_w.shape[0]
    n_ctx, n_lat = ctx.shape[1], x.shape[1]
    assert x.shape[0] == 1 and n_ctx % TM == 0 and n_lat % TM == 0
    nct = n_ctx // TM
    xa = (ctx[0], x[0])

    cond_b = jnp.broadcast_to(jnp.stack([c_ctx, c[0]])[:, :, None], (N_COND, D_MODEL, LANES))
    mod = _adaln(cond_b, ada_w, ada_b).reshape(depth, 8, N_MOD, D_MODEL)

    cos_t, sin_t = _rope_tables(n_ctx, n_lat)
    w_big, w_small = _wprep(w_in)
    colscale = jnp.ones((1, D_BIG), F32)
    colscale = colscale.at[:, COL_RET + 512:COL_RET + 1024].set(DK_RET ** -0.5)
    colscale = colscale.at[:, COL_GLA:COL_GLA + 512].set(DK_GLA ** -0.5)
    colscale = colscale.at[:, COL_RETG:COL_RETG + D_MODEL].set(0.5)
    colscale = colscale.at[:, COL_GLAR:COL_GLAR + D_MODEL].set(0.5)
    colscale = colscale.at[:, COL_MERGE:COL_MERGE + 3 * D_MODEL].set(0.5)
    colscale = colscale.at[:, COL_Z:COL_Z + D_INNER].set(0.5)

    bf = lambda t: t.astype(BF16)
    ffn1_wi, ffn1_wo, ffn2_wi, ffn2_wo = bf(ffn1_wi), bf(ffn1_wo), bf(ffn2_wi), bf(ffn2_wo)
    wb_ret, wb_gla, wb_ssd, w_out = bf(wb_ret), bf(wb_gla), bf(wb_ssd), bf(w_out)

    logit8 = jnp.broadcast_to(ret_logit.reshape(depth, 2 * H_RET, 1), (depth, 2 * H_RET, 128))
    wa_pad = jnp.zeros((depth, 2, D_SMALL, H_GLA * DK_GLA), F32)
    wa_pad = wa_pad.at[:, 0, SM_AF:SM_AF + GLA_RANK].set(gla_wa2[:, 0])
    wa_pad = wa_pad.at[:, 1, SM_AB:SM_AB + GLA_RANK].set(gla_wa2[:, 1])
    wa_hi = wa_pad.astype(BF16)
    wa_lo = (wa_pad - wa_hi.astype(F32)).astype(BF16)
    ws_hi = w_small.astype(BF16)
    ws_lo = (w_small - ws_hi.astype(F32)).astype(BF16)
    ba = gla_ba.reshape(depth, 2, 1, H_GLA * DK_GLA)
    zpad = jnp.zeros((depth, SM_DTF), F32)
    dtb_full = jnp.concatenate([zpad, dt_bias[:, 0], dt_bias[:, 1], zpad], axis=1).reshape(depth, 1, D_SMALL)
    alog_full = jnp.concatenate([zpad, a_log[:, 0], a_log[:, 1], zpad], axis=1).reshape(depth, 1, D_SMALL)
    conv_w8 = jnp.concatenate([conv_w, jnp.zeros((depth, 8 - CONV_K, D_XBC), F32)], axis=1)
    conv_b3 = conv_b.reshape(depth, 1, D_XBC)
    gng = jnp.tile(gla_norm_g, (1, H_GLA)).reshape(depth, 1, H_GLA * DV_GLA)
    dx = jnp.repeat(ssd_d, SSD_HEADDIM, axis=1).reshape(depth, 1, D_INNER)
    sng = ssd_norm_g.reshape(depth, 1, D_INNER)

    fg = final_norm_g.reshape(1, D_MODEL)
    for l in range(depth):
        xa = _ffn(xa, mod, norm_g, ffn1_wi, ffn1_wo, fg, l, 0, 0, nct)
        p, psm = _inproj(xa, mod, norm_g, cos_t, sin_t, w_big, colscale, ws_hi, ws_lo, l, nct)
        xbc = _conv(p, conv_w8, conv_b3, l, nct)
        yf, yb = _mix_scan(p, psm, xbc, logit8[l], wa_hi[l], wa_lo[l], ba[l],
                           dtb_full[l], alog_full[l], n_ctx)
        xa = _merge(xa, mod, p, xbc, yf, yb, gng, dx, sng,
                    wb_ret, wb_gla, wb_ssd, w_out, norm_g, ffn2_wi, ffn2_wo, fg, l, nct,
                    final=(l == depth - 1))

    return xa[None]
```

```python
import functools
import math

import jax
import jax.numpy as jnp
from jax import lax
from jax.experimental import pallas as pl
from jax.experimental.pallas import tpu as pltpu

F32 = jnp.float32
BF16 = jnp.bfloat16
HIGHEST = lax.Precision.HIGHEST
LOG2E = 1.4426950408889634

D_MODEL = 1024
GRID_W = 64
EPS = 1e-6
N_MOD = 9
D_FF = 2816
H_RET, DK_RET, DV_RET, CHUNK_RET = 4, 128, 256, 128
ROPE_BASE = 10000.0
H_GLA, DK_GLA, DV_GLA, GLA_RANK, CHUNK_GLA = 4, 128, 256, 16, 64
GLA_NORMALIZER = 16.0
GLA_SUB = 16
D_INNER = 2 * D_MODEL
SSD_HEADDIM = 64
H_SSD = D_INNER // SSD_HEADDIM
SSD_GROUPS = 4
D_STATE = 128
CONV_K = 5
CHUNK_SSD = 128
N_BC = SSD_GROUPS * D_STATE
D_XBC = D_INNER + 2 * N_BC

COL_RET = 0
COL_GLA = 2048
D_QKV = 2048
D_Y = 4096
COL_RETG = 4096
COL_GLAR = 5120
COL_XBC = 6144
COL_MERGE = 9216
COL_Z = 12288
D_BIG = 14336
SM_AF, SM_AB, SM_DTF, SM_DTB, D_SMALL = 0, 16, 32, 64, 128

TM = 256
CW = 1024
FF_SPLITS = (0, 1536, D_FF)
VMEM_LIMIT = 56 * 1024 * 1024


def _cparams(sem):
    return pltpu.CompilerParams(dimension_semantics=sem, vmem_limit_bytes=VMEM_LIMIT)


def _resident(shape, index_map):
    return pl.BlockSpec(shape, index_map, pipeline_mode=pl.Buffered(1))


def _sigmoid(x):
    return 0.5 * jnp.tanh(0.5 * x) + 0.5


def _silu(x):
    return x * _sigmoid(x)


def _silu_of_half(h):
    return h * jnp.tanh(h) + h


def _sigmoid_of_half(h):
    return 0.5 * jnp.tanh(h) + 0.5


def _softplus(x):
    return jnp.maximum(x, 0.0) + jnp.log(1.0 + jnp.exp(-jnp.abs(x)))


def _log_sigmoid(x, precise=False):
    e = jnp.exp(-jnp.abs(x))
    return jnp.minimum(x, 0.0) - (jnp.log1p(e) if precise else jnp.log(1.0 + e))


def _rms(x):
    return x * lax.rsqrt(jnp.mean(x * x, axis=-1, keepdims=True) + EPS)


def _pre(x, g, shift, scale):
    return _rms(x) * g * (1.0 + scale) + shift


def _dot(a, b):
    return jnp.dot(a, b, preferred_element_type=F32)


def _dot_exact(a, b):
    return jnp.dot(a, b, precision=HIGHEST, preferred_element_type=F32)


def _split2(x):
    hi = x.astype(BF16)
    return hi, (x - hi.astype(F32)).astype(BF16)


def _split3(x):
    hi = x.astype(BF16)
    r = x - hi.astype(F32)
    mid = r.astype(BF16)
    return hi, mid, (r - mid.astype(F32)).astype(BF16)


def _dot_sel(m, x):
    hi, mid, lo = _split3(x)
    return _dot(m, hi) + _dot(m, mid) + _dot(m, lo)


def _dot_split(x, w_hi, w_lo):
    x_hi, x_lo = _split2(x)
    return _dot(x_hi, w_hi) + _dot(x_lo, w_hi) + _dot(x_hi, w_lo)


def _dot_nt(a, b):
    return lax.dot_general(a, b, (((1,), (1,)), ((), ())), preferred_element_type=F32)


def _dot_tn(a, b):
    return lax.dot_general(a, b, (((0,), (0,)), ((), ())), preferred_element_type=F32)


N_COND = 2
LANES = 128
ADA_COLS = 2304


def _adaln_kernel(cond_ref, w_ref, b_ref, o_ref, s_ref):
    s_ref[...] = _silu(cond_ref[...])
    width = w_ref.shape[1]
    reps = width // LANES

    def body(k8, accs):
        r0 = pl.multiple_of(k8 * 8, 8)
        wk = w_ref[pl.ds(r0, 8), :]
        return tuple(acc + wk * jnp.concatenate([s_ref[r, pl.ds(r0, 8), :]] * reps, axis=1)
                     for r, acc in enumerate(accs))

    zero = jnp.zeros((8, width), F32)
    accs = lax.fori_loop(0, D_MODEL // 8, body, (zero,) * N_COND)
    rows = [jnp.sum(a, axis=0, keepdims=True) + b_ref[...] for a in accs]
    o_ref[...] = jnp.concatenate(rows + [jnp.zeros((8 - N_COND, width), F32)], axis=0)


def _adaln(cond_b, ada_w, ada_b):
    depth = ada_w.shape[0]
    nblk = N_MOD * D_MODEL // ADA_COLS
    return pl.pallas_call(
        _adaln_kernel,
        grid=(depth, nblk),
        in_specs=[
            pl.BlockSpec((N_COND, D_MODEL, LANES), lambda l, j: (0, 0, 0)),
            pl.BlockSpec((None, D_MODEL, ADA_COLS), lambda l, j: (l, 0, j)),
            pl.BlockSpec((None, 1, ADA_COLS), lambda l, j: (l, 0, j)),
        ],
        out_specs=pl.BlockSpec((None, 8, ADA_COLS), lambda l, j: (l, 0, j)),
        out_shape=jax.ShapeDtypeStruct((depth, 8, N_MOD * D_MODEL), F32),
        scratch_shapes=[pltpu.VMEM((N_COND, D_MODEL, LANES), F32)],
        compiler_params=_cparams(("arbitrary", "arbitrary")),
        name="adaln",
    )(cond_b, ada_w, ada_b.reshape(depth, 1, N_MOD * D_MODEL))


def _ffn_kernel(x_ref, mod_ref, g_ref, wa_ref, wu_ref, wo_ref, fg_ref, o_ref, *, k0, gi, final):
    _ffn_body(x_ref[...], mod_ref, g_ref, wa_ref, wu_ref, wo_ref, fg_ref, o_ref, k0, gi, final)


def _ffn_body(x, mod_ref, g_ref, wa_ref, wu_ref, wo_ref, fg_ref, o_ref, k0, gi, final):
    h = _pre(x, g_ref[gi:gi + 1, :], mod_ref[0, k0:k0 + 1, :], mod_ref[0, k0 + 1:k0 + 2, :])
    hb = h.astype(BF16)
    out = None
    for lo, hi in zip(FF_SPLITS[:-1], FF_SPLITS[1:]):
        sl = slice(lo, hi)
        a = _dot(hb, wa_ref[:, sl])
        u = _dot(hb, wu_ref[:, sl])
        t = (_silu(a) * u).astype(BF16)
        part = _dot(t, wo_ref[sl, :])
        out = part if out is None else out + part
    y = x + 0.5 * mod_ref[0, k0 + 2:k0 + 3, :] * out
    o_ref[...] = _rms(y) * fg_ref[...] if final else y


def _ffn_split_kernel(ctx_ref, lat_ref, mod_ref, g_ref, wa_ref, wu_ref, wo_ref, fg_ref, o_ref,
                      *, k0, gi, nct):
    x = jnp.where(pl.program_id(0) >= nct, lat_ref[...], ctx_ref[...])
    _ffn_body(x, mod_ref, g_ref, wa_ref, wu_ref, wo_ref, fg_ref, o_ref, k0, gi, False)


def _ffn(xs, mod, norm_g, wi, wo, final_g, l, k0, gi, nct):
    split = isinstance(xs, tuple)
    n = xs[0].shape[0] + xs[1].shape[0] if split else xs.shape[0]
    if split:
        kern = functools.partial(_ffn_split_kernel, k0=k0, gi=gi, nct=nct)
        x_specs = [pl.BlockSpec((TM, D_MODEL), lambda i: (jnp.minimum(i, nct - 1), 0)),
                   pl.BlockSpec((TM, D_MODEL), lambda i: (jnp.maximum(i - nct, 0), 0))]
        x_args = list(xs)
    else:
        kern = functools.partial(_ffn_kernel, k0=k0, gi=gi, final=False)
        x_specs = [pl.BlockSpec((TM, D_MODEL), lambda i: (i, 0))]
        x_args = [xs]
    return pl.pallas_call(
        kern,
        grid=(n // TM,),
        in_specs=x_specs + [
            pl.BlockSpec((None, 1, N_MOD, D_MODEL),
                         lambda i: (l, (i >= nct).astype(jnp.int32), 0, 0)),
            _resident((None, 3, D_MODEL), lambda i: (l, 0, 0)),
            _resident((None, D_MODEL, D_FF), lambda i: (l, 0, 0)),
            _resident((None, D_MODEL, D_FF), lambda i: (l, 0, 1)),
            _resident((None, D_FF, D_MODEL), lambda i: (l, 0, 0)),
            _resident((1, D_MODEL), lambda i: (0, 0)),
        ],
        out_specs=pl.BlockSpec((TM, D_MODEL), lambda i: (i, 0)),
        out_shape=jax.ShapeDtypeStruct((n, D_MODEL), F32),
        compiler_params=_cparams(("arbitrary",)),
        name="ffn",
    )(*x_args, mod, norm_g, wi, wi, wo, final_g)


def _inproj_kernel(x_ref, mod_ref, g_ref, cos_ref, sin_ref, w_ref, cs_ref, wsh_ref, wsl_ref,
                   p_ref, ps_ref):
    h = _pre(x_ref[...], g_ref[1:2, :], mod_ref[0, 3:4, :], mod_ref[0, 4:5, :])
    hb = h.astype(BF16)
    hl = _dot(hb, wsh_ref[...])
    ps_ref[...] = (hl[:, 0:D_SMALL] + hl[:, D_SMALL:2 * D_SMALL]
                   + _dot((h - hb.astype(F32)).astype(BF16), wsl_ref[...]))
    for j in range(D_BIG // CW):
        cols = slice(j * CW, (j + 1) * CW)
        acc = _dot(hb, w_ref[:, cols]) * cs_ref[:, cols]
        if j == 0:
            cos = cos_ref[...]
            sin = sin_ref[...]
            for b in range(CW // DK_RET):
                t = acc[:, b * DK_RET:(b + 1) * DK_RET]
                p_ref[:, b * DK_RET:(b + 1) * DK_RET] = (
                    t * cos + pltpu.roll(t, DK_RET // 2, 1) * sin).astype(BF16)
        else:
            p_ref[:, cols] = acc.astype(BF16)


def _inproj(xa, mod, norm_g, cos_t, sin_t, w_big, colscale, ws_hi, ws_lo, l, nct):
    n = xa.shape[0]
    return pl.pallas_call(
        _inproj_kernel,
        grid=(n // TM,),
        in_specs=[
            pl.BlockSpec((TM, D_MODEL), lambda i: (i, 0)),
            pl.BlockSpec((None, 1, N_MOD, D_MODEL), lambda i: (l, (i >= nct).astype(jnp.int32), 0, 0)),
            _resident((None, 3, D_MODEL), lambda i: (l, 0, 0)),
            pl.BlockSpec((TM, DK_RET), lambda i: (i, 0)),
            pl.BlockSpec((TM, DK_RET), lambda i: (i, 0)),
            _resident((None, D_MODEL, D_BIG), lambda i: (l, 0, 0)),
            _resident((1, D_BIG), lambda i: (0, 0)),
            _resident((None, D_MODEL, 2 * D_SMALL), lambda i: (l, 0, 0)),
            _resident((None, D_MODEL, D_SMALL), lambda i: (l, 0, 0)),
        ],
        out_specs=[
            pl.BlockSpec((TM, D_BIG), lambda i: (i, 0)),
            pl.BlockSpec((TM, D_SMALL), lambda i: (i, 0)),
        ],
        out_shape=[
            jax.ShapeDtypeStruct((n, D_BIG), BF16),
            jax.ShapeDtypeStruct((n, D_SMALL), F32),
        ],
        compiler_params=_cparams(("arbitrary",)),
        name="inproj",
    )(xa, mod, norm_g, cos_t, sin_t, w_big, colscale, jnp.concatenate([ws_hi, ws_lo], axis=-1), ws_hi)


CONV_STRIP = 512
CONV_HALO = 16
CONV_BLK = 128


def _conv_kernel(cur_ref, prev_ref, next_ref, w_ref, b_ref, o_ref, ext_ref, *, nct, ntiles):
    i = pl.program_id(0)
    has_prev = jnp.logical_and(i != 0, i != nct)
    has_next = jnp.logical_and(i != nct - 1, i != ntiles - 1)
    halo0 = jnp.zeros((CONV_HALO, D_XBC), BF16)
    ext_ref[0:CONV_HALO, :] = jnp.where(has_prev, prev_ref[...], halo0)
    ext_ref[CONV_HALO:CONV_HALO + TM, :] = cur_ref[...]
    ext_ref[CONV_HALO + TM:2 * CONV_HALO + TM, :] = jnp.where(has_next, next_ref[...], halo0)
    kwin = CONV_BLK + 2 * CONV_HALO
    rr = lax.broadcasted_iota(jnp.int32, (CONV_BLK, kwin), 0)
    cc = lax.broadcasted_iota(jnp.int32, (CONV_BLK, kwin), 1)
    mid = CONV_K // 2
    shift = {k: (cc == rr + CONV_HALO + k - mid).astype(F32).astype(BF16)
             for k in range(CONV_K) if k != mid}
    for blk in range(TM // CONV_BLK):
        rows = slice(blk * CONV_BLK, (blk + 1) * CONV_BLK)
        for s in range(D_XBC // CONV_STRIP):
            cs = slice(s * CONV_STRIP, (s + 1) * CONV_STRIP)
            win = ext_ref[blk * CONV_BLK:blk * CONV_BLK + kwin, cs]
            acc = b_ref[:, cs] + w_ref[mid:mid + 1, cs] * cur_ref[rows, cs].astype(F32)
            for k in range(CONV_K):
                if k != mid:
                    acc = acc + w_ref[k:k + 1, cs] * _dot(shift[k], win)
            o_ref[rows, cs] = _silu(acc).astype(BF16)


def _conv(p, conv_w, conv_b, l, nct):
    n = p.shape[0]
    ntiles = n // TM
    rh = TM // CONV_HALO
    nblkh = n // CONV_HALO
    cb = COL_XBC // D_XBC
    return pl.pallas_call(
        functools.partial(_conv_kernel, nct=nct, ntiles=ntiles),
        grid=(ntiles,),
        in_specs=[
            pl.BlockSpec((TM, D_XBC), lambda i: (i, cb)),
            pl.BlockSpec((CONV_HALO, D_XBC), lambda i: (jnp.maximum(i * rh - 1, 0), cb)),
            pl.BlockSpec((CONV_HALO, D_XBC), lambda i: (jnp.minimum((i + 1) * rh, nblkh - 1), cb)),
            pl.BlockSpec((None, 8, D_XBC), lambda i: (l, 0, 0)),
            pl.BlockSpec((None, 1, D_XBC), lambda i: (l, 0, 0)),
        ],
        out_specs=pl.BlockSpec((TM, D_XBC), lambda i: (i, 0)),
        out_shape=jax.ShapeDtypeStruct((n, D_XBC), BF16),
        scratch_shapes=[pltpu.VMEM((TM + 2 * CONV_HALO, D_XBC), BF16)],
        compiler_params=_cparams(("arbitrary",)),
        name="conv",
    )(p, p, p, conv_w, conv_b)


def _bwd_chunk(t, ncc, n):
    return jnp.where(t < ncc, ncc - 1 - t, n + ncc - 1 - t)


def _ret_init(lg_ref, s_ref, dm_ref, rs_ref, ks_ref, cd_ref):
    c = CHUNK_RET
    s_ref[...] = jnp.zeros(s_ref.shape, F32)
    la = _log_sigmoid(lg_ref[...], precise=True)
    ii = lax.broadcasted_iota(jnp.int32, (c, c), 0)
    jj = lax.broadcasted_iota(jnp.int32, (c, c), 1)
    ir = lax.broadcasted_iota(jnp.int32, (c, 1), 0)
    for d in range(2):
        for h in range(H_RET):
            a = la[d * H_RET + h:d * H_RET + h + 1, :]
            a2 = jnp.concatenate([a, a], axis=1)
            if d == 0:
                dist, keep = ii - jj, ii >= jj
                rpow = ir + 1
                kpow = c - 1 - ir
            else:
                dist, keep = jj - ii, jj >= ii
                rpow = c - ir
                kpow = ir
            dm_ref[d, h] = jnp.where(keep, jnp.exp(dist.astype(F32) * a), 0.0)
            rs_ref[d, h] = jnp.exp(rpow.astype(F32) * a2)
            ks_ref[d, h] = jnp.exp(kpow.astype(F32) * a)
            cd_ref[d, h] = jnp.exp(float(c) * jnp.broadcast_to(a2, (8, DV_RET)))


def _ret_phases(qkf_ref, vf_ref, yf_ref, qkb_ref, vb_ref, yb_ref, s_ref, dm_ref, rs_ref, ks_ref, cd_ref):
    dirs = ((qkf_ref, vf_ref, yf_ref), (qkb_ref, vb_ref, yb_ref))

    def q_of(qk_ref, h):
        return qk_ref[:, h * DK_RET:(h + 1) * DK_RET]

    def k_of(qk_ref, h):
        return qk_ref[:, H_RET * DK_RET + h * DK_RET:H_RET * DK_RET + (h + 1) * DK_RET]

    qk = [[_dot_nt(q_of(qk_ref, h), k_of(qk_ref, h)) for h in range(H_RET)] for qk_ref, _, _ in dirs]
    qs = [[_dot(q_of(qk_ref, h), s_ref[d, h].astype(BF16)) for h in range(H_RET)]
          for d, (qk_ref, _, _) in enumerate(dirs)]
    def outputs():
        for d, (qk_ref, v_ref, y_ref) in enumerate(dirs):
            for h in range(H_RET):
                v = v_ref[:, h * DV_RET:(h + 1) * DV_RET]
                att = (qk[d][h] * dm_ref[d, h]).astype(BF16)
                y = rs_ref[d, h] * qs[d][h] + _dot(att, v)
                y_ref[:, h * DV_RET:(h + 1) * DV_RET] = y.astype(y_ref.dtype)

    def states():
        for d, (qk_ref, v_ref, y_ref) in enumerate(dirs):
            for h in range(H_RET):
                v = v_ref[:, h * DV_RET:(h + 1) * DV_RET]
                kt = (k_of(qk_ref, h).astype(F32) * ks_ref[d, h]).astype(BF16)
                s_ref[d, h] = cd_ref[d, h][0:1, :] * s_ref[d, h] + _dot_tn(kt, v)

    return outputs, states


GLA_BLOCK = 128


def _gla_dir(d, qk_ref, v_ref, ps_ref, wah_ref, wal_ref, ba_ref, y_ref, st_ref):
    c = CHUNK_GLA
    nb = GLA_BLOCK
    nsub = c // GLA_SUB
    bi = lax.broadcasted_iota(jnp.int32, (nb, nb), 0)
    bj = lax.broadcasted_iota(jnp.int32, (nb, nb), 1)
    same = (bi // c) == (bj // c)
    sub_start = jnp.bitwise_and(bi, -GLA_SUB)
    if d == 0:
        cum = jnp.logical_and(same, bj <= bi)
        ref_m = jnp.logical_and(same, bj < sub_start)
    else:
        cum = jnp.logical_and(same, bj >= bi)
        ref_m = jnp.logical_and(same, bj >= sub_start + GLA_SUB)
    tri = jnp.concatenate([cum, ref_m], axis=0).astype(F32).astype(BF16)
    z = _dot_split(ps_ref[...], wah_ref[d], wal_ref[d]) + ba_ref[d]
    a = _log_sigmoid(z) * (1.0 / GLA_NORMALIZER)
    gr = _dot_sel(tri, a)
    ii = lax.broadcasted_iota(jnp.int32, (c, c), 0)
    jj = lax.broadcasted_iota(jnp.int32, (c, c), 1)
    causal = (jj <= ii) if d == 0 else (jj >= ii)
    end_row = c - 1 if d == 0 else 0
    col_blk = jnp.right_shift(jj, int(math.log2(GLA_SUB)))

    def sub_rows(jb):
        return slice(jb * GLA_SUB, c) if d == 0 else slice(0, (jb + 1) * GLA_SUB)

    order = list(range(nb // c)) if d == 0 else list(range(nb // c - 1, -1, -1))
    pre = {}
    for ci in order:
        rows = slice(ci * c, (ci + 1) * c)
        g_all = gr[ci * c:(ci + 1) * c]
        r_all = gr[nb + ci * c:nb + (ci + 1) * c]
        for h in range(H_GLA):
            ks = slice(h * DK_GLA, (h + 1) * DK_GLA)
            q = qk_ref[rows, h * DK_GLA:(h + 1) * DK_GLA].astype(F32)
            k = qk_ref[rows, H_GLA * DK_GLA + h * DK_GLA:H_GLA * DK_GLA + (h + 1) * DK_GLA].astype(F32)
            g = g_all[:, ks]
            r = r_all[:, ks]
            g_end = g[end_row:end_row + 1, :]
            kt = (k * jnp.exp(r - g)).astype(BF16)
            qs = []
            for jb in range(nsub):
                rj = r[jb * GLA_SUB:jb * GLA_SUB + 1, :]
                need = sub_rows(jb)
                qs.append((q[need] * jnp.exp(jnp.minimum(g[need] - rj, 0.0))).astype(BF16))
            pm = _dot_nt(jnp.concatenate(qs, axis=0), kt)
            qg = (q * jnp.exp(g)).astype(BF16)
            ke = (k * jnp.exp(g_end - g)).astype(BF16)
            pre[(ci, h)] = (pm, qg, ke, jnp.exp(g_end))

    def finish(ci, h):
        rows = slice(ci * c, (ci + 1) * c)
        pm, qg, ke, e_end = pre[(ci, h)]
        v = v_ref[rows, h * DV_GLA:(h + 1) * DV_GLA]
        att = jnp.zeros((c, c), F32)
        off = 0
        for jb in range(nsub):
            need = sub_rows(jb)
            nr = need.stop - need.start
            pads = [jnp.zeros((need.start, c), F32), pm[off:off + nr], jnp.zeros((c - need.stop, c), F32)]
            piece = jnp.concatenate([t for t in pads if t.shape[0]], axis=0)
            att = jnp.where(col_blk == jb, piece, att)
            off += nr
        att = jnp.where(causal, att, 0.0).astype(BF16)
        st = st_ref[d, h]
        y = _dot_nt(qg, st.astype(BF16)) + _dot(att, v)
        y_ref[rows, h * DV_GLA:(h + 1) * DV_GLA] = y.astype(y_ref.dtype)
        st_ref[d, h] = e_end * st + _dot_tn(v, ke)

    return [[functools.partial(finish, ci, h) for h in range(H_GLA)] for ci in order]


def _ssd_dir(d, xbc_ref, ps_ref, dtb_ref, alog_ref, y_ref, s_ref, e_ref):
    c = CHUNK_SSD
    base = SM_DTF if d == 0 else SM_DTB
    ii = lax.broadcasted_iota(jnp.int32, (c, c), 0)
    jj = lax.broadcasted_iota(jnp.int32, (c, c), 1)
    keep = (jj <= ii) if d == 0 else (jj >= ii)
    end_row = c - 1 if d == 0 else 0
    lane = lax.broadcasted_iota(jnp.int32, (1, D_SMALL), 1)
    in_dir = jnp.logical_and(lane >= base, lane < base + H_SSD)
    dt = jnp.where(in_dir, _softplus(ps_ref[...] + dtb_ref[...]), 0.0)
    a = dt * jnp.where(in_dir, -jnp.exp(alog_ref[...]), 0.0)
    g = _dot_sel(keep.astype(F32).astype(BF16), a)
    g2 = g * LOG2E
    gt = ((g - jnp.where(in_dir, jnp.log(dt), 0.0)) * LOG2E).T
    g_end = g[end_row:end_row + 1, :]
    f_state = dt * jnp.exp(g_end - g)
    f_y = jnp.exp(g)
    f_end = jnp.broadcast_to(jnp.exp(g_end), (8, D_SMALL))
    fs_hi, fs_lo = _split2(jnp.concatenate([f_state, f_y, f_end], axis=0))
    fx = _dot(jnp.concatenate([fs_hi, fs_lo], axis=1), e_ref[d])
    wv = (xbc_ref[:, 0:D_INNER].astype(F32) * fx[0:c]).astype(BF16)
    egx = fx[c:2 * c]
    eex = fx[2 * c:2 * c + 1]
    lane2 = lax.broadcasted_iota(jnp.int32, (c, 2 * SSD_HEADDIM), 1)
    hpg = H_SSD // SSD_GROUPS
    gw = hpg * SSD_HEADDIM
    def b_of(grp):
        return xbc_ref[:, D_INNER + grp * D_STATE:D_INNER + (grp + 1) * D_STATE]

    def c_of(grp):
        return xbc_ref[:, D_INNER + N_BC + grp * D_STATE:D_INNER + N_BC + (grp + 1) * D_STATE]

    cbs = [_dot_nt(c_of(grp), b_of(grp)).astype(BF16) for grp in range(SSD_GROUPS)]
    yis = [_dot(c_of(grp), s_ref[d, grp].astype(BF16)) for grp in range(SSD_GROUPS)]

    def group(grp):
        bg = b_of(grp)
        cb = cbs[grp]
        sg = s_ref[d, grp]
        yi = yis[grp]
        for pr in range(hpg // 2):
            h0 = grp * hpg + 2 * pr
            cols = slice(h0 * SSD_HEADDIM, (h0 + 2) * SSD_HEADDIM)
            atts = []
            for hh in (h0, h0 + 1):
                col = base + hh
                diff = g2[:, col:col + 1] - gt[col:col + 1, :]
                dec = jnp.exp2(jnp.where(keep, diff, -jnp.inf))
                atts.append(cb * dec.astype(BF16))
            v2 = xbc_ref[:, cols]
            vv = jnp.concatenate([jnp.where(lane2 < SSD_HEADDIM, v2, jnp.zeros_like(v2)),
                                  jnp.where(lane2 >= SSD_HEADDIM, v2, jnp.zeros_like(v2))], axis=0)
            y2 = _dot(jnp.concatenate(atts, axis=1), vv)
            yo = yi[:, pr * 2 * SSD_HEADDIM:(pr + 1) * 2 * SSD_HEADDIM] * egx[:, cols] + y2
            y_ref[:, cols] = yo.astype(y_ref.dtype)
        gcols = slice(grp * gw, (grp + 1) * gw)
        s_ref[d, grp] = eex[:, gcols] * sg + _dot_tn(bg, wv[:, gcols])

    return [functools.partial(group, grp) for grp in range(SSD_GROUPS)]


def _ssd_init(s_ref, e_ref):
    s_ref[...] = jnp.zeros(s_ref.shape, F32)
    row = lax.broadcasted_iota(jnp.int32, (2 * D_SMALL, D_INNER), 0) % D_SMALL
    head = lax.broadcasted_iota(jnp.int32, (2 * D_SMALL, D_INNER), 1) // SSD_HEADDIM
    e_ref[0] = (row - SM_DTF == head).astype(F32).astype(BF16)
    e_ref[1] = (row - SM_DTB == head).astype(F32).astype(BF16)


def _mix_kernel(lg_ref, wah_ref, wal_ref, ba_ref, dtb_ref, alog_ref,
                qkv_f, ps_f, xbc_f, qkv_b, ps_b, xbc_b, y_f, y_b,
                rs_s, r_dm, r_rs, r_ks, r_cd, g_st, s_st, s_e):
    @pl.when(pl.program_id(0) == 0)
    def _():
        _ret_init(lg_ref, rs_s, r_dm, r_rs, r_ks, r_cd)
        g_st[...] = jnp.zeros(g_st.shape, F32)
        _ssd_init(s_st, s_e)

    w = H_RET * DV_RET

    def split_in(ref):
        return (ref.at[:, COL_RET:COL_RET + w], ref.at[:, COL_RET + w:COL_RET + 2 * w],
                ref.at[:, COL_GLA:COL_GLA + w], ref.at[:, COL_GLA + w:COL_GLA + 2 * w])

    def split_out(ref):
        return ref.at[:, 0:w], ref.at[:, w:2 * w], ref.at[:, 2 * w:2 * w + D_INNER]

    rqk_f, rv_f, gqk_f, gv_f = split_in(qkv_f)
    rqk_b, rv_b, gqk_b, gv_b = split_in(qkv_b)
    yr_f, yg_f, ys_f = split_out(y_f)
    yr_b, yg_b, ys_b = split_out(y_b)

    ret_out, ret_state = _ret_phases(rqk_f, rv_f, yr_f, rqk_b, rv_b, yr_b, rs_s, r_dm, r_rs, r_ks, r_cd)
    gla_f = _gla_dir(0, gqk_f, gv_f, ps_f, wah_ref, wal_ref, ba_ref, yg_f, g_st)
    gla_b = _gla_dir(1, gqk_b, gv_b, ps_b, wah_ref, wal_ref, ba_ref, yg_b, g_st)
    ssd_f = _ssd_dir(0, xbc_f, ps_f, dtb_ref, alog_ref, ys_f, s_st, s_e)
    ssd_b = _ssd_dir(1, xbc_b, ps_b, dtb_ref, alog_ref, ys_b, s_st, s_e)
    ret_out()
    for fn in gla_f[0] + gla_b[0] + ssd_f:
        fn()
    ret_state()
    for fn in gla_f[1] + gla_b[1] + ssd_b:
        fn()


def _mix_scan(p, psm, xbc, logit8, wa_hi, wa_lo, ba, dtb_full, alog_full, ncc_rows):
    n = p.shape[0]
    c = CHUNK_RET
    assert c == GLA_BLOCK == CHUNK_SSD
    nch = n // c
    ncc = ncc_rows // c
    bwd = lambda t: _bwd_chunk(t, ncc, nch)
    assert COL_RET == 0 and COL_GLA == D_QKV
    const = lambda shape: pl.BlockSpec(shape, lambda t: (0,) * len(shape))

    def blocks(idx):
        return [
            pl.BlockSpec((c, 2 * D_QKV), lambda t: (idx(t), 0)),
            pl.BlockSpec((c, D_SMALL), lambda t: (idx(t), 0)),
            pl.BlockSpec((c, D_XBC), lambda t: (idx(t), 0)),
        ]

    fwd = lambda t: t
    out_w = (D_Y, D_Y)
    out_idx = (fwd, bwd)
    return pl.pallas_call(
        _mix_kernel,
        grid=(nch,),
        in_specs=[
            const((8, 128)),
            const((2, D_SMALL, H_GLA * DK_GLA)),
            const((2, D_SMALL, H_GLA * DK_GLA)),
            const((2, 1, H_GLA * DK_GLA)),
            const((1, D_SMALL)),
            const((1, D_SMALL)),
        ] + blocks(fwd) + blocks(bwd),
        out_specs=[pl.BlockSpec((c, ow), functools.partial(lambda t, f: (f(t), 0), f=f))
                   for ow, f in zip(out_w, out_idx)],
        out_shape=[jax.ShapeDtypeStruct((n, ow), BF16) for ow in out_w],
        scratch_shapes=[
            pltpu.VMEM((2, H_RET, DK_RET, DV_RET), F32),
            pltpu.VMEM((2, H_RET, c, c), F32),
            pltpu.VMEM((2, H_RET, c, DV_RET), F32),
            pltpu.VMEM((2, H_RET, c, DK_RET), F32),
            pltpu.VMEM((2, H_RET, 8, DV_RET), F32),
            pltpu.VMEM((2, H_GLA, DV_GLA, DK_GLA), F32),
            pltpu.VMEM((2, SSD_GROUPS, D_STATE, D_INNER // SSD_GROUPS), F32),
            pltpu.VMEM((2, 2 * D_SMALL, D_INNER), BF16),
        ],
        compiler_params=_cparams(("arbitrary",)),
        name="mix_scan",
    )(logit8, wa_hi, wa_lo, ba, dtb_full, alog_full, p, psm, xbc, p, psm, xbc)


def _segnorm(y, width):
    outs = [_rms(y[:, s * width:(s + 1) * width]) for s in range(y.shape[1] // width)]
    return jnp.concatenate(outs, axis=1)


def _merge_kernel(x_ref, mod_ref, rg_ref, gr_ref, mg_ref, z_ref, xs_ref, yf_ref, yb_ref,
                  gng_ref, dx_ref, sng_ref, wbr_ref, wbg_ref, wbs_ref, wo_ref,
                  ng_ref, wa_ref, wu_ref, wf_ref, fg_ref, o_ref, *, final):
    c_r, c_g, c_s = slice(0, D_MODEL), slice(D_MODEL, 2 * D_MODEL), slice(2 * D_MODEL, D_Y)
    yr = (yf_ref[:, c_r] + yb_ref[:, c_r]).astype(F32)
    br = _segnorm(yr, DV_RET) * _silu_of_half(rg_ref[...].astype(F32))
    b_ret = _dot(br.astype(BF16), wbr_ref[...])
    yg = (yf_ref[:, c_g] + yb_ref[:, c_g]).astype(F32)
    bg = _segnorm(yg, DV_GLA) * gng_ref[...] * _silu_of_half(gr_ref[...].astype(F32))
    b_gla = _dot(bg.astype(BF16), wbg_ref[...])
    ys = (yf_ref[:, c_s] + yb_ref[:, c_s]).astype(F32)
    ys = (ys + dx_ref[...] * xs_ref[...].astype(F32)) * _silu_of_half(z_ref[...].astype(F32))
    bs = _segnorm(ys, D_INNER // SSD_GROUPS) * sng_ref[...]
    b_ssd = _dot(bs.astype(BF16), wbs_ref[...])
    gates = _sigmoid_of_half(mg_ref[...].astype(F32))
    mix = (gates[:, 0:D_MODEL] * b_ret + gates[:, D_MODEL:2 * D_MODEL] * b_gla
           + gates[:, 2 * D_MODEL:3 * D_MODEL] * b_ssd)
    out = _dot(mix.astype(BF16), wo_ref[...])
    x_mid = x_ref[...] + mod_ref[0, 5:6, :] * out
    _ffn_body(x_mid, mod_ref, ng_ref, wa_ref, wu_ref, wf_ref, fg_ref, o_ref, 6, 2, final)


def _merge(xa, mod, p, xbc, yf, yb, gng, dx, sng, wbr, wbg, wbs, wout,
           norm_g, wi, wf, final_g, l, nct, final):
    n = xa.shape[0]
    skip = nct if final else 0
    ntiles = n // TM - skip
    row = lambda w, cb: pl.BlockSpec((TM, w), lambda i: (i + skip, cb))
    return pl.pallas_call(
        functools.partial(_merge_kernel, final=final),
        grid=(ntiles,),
        in_specs=[
            row(D_MODEL, 0),
            pl.BlockSpec((None, 1, N_MOD, D_MODEL),
                         lambda i: (l, (i + skip >= nct).astype(jnp.int32), 0, 0)),
            row(D_MODEL, COL_RETG // D_MODEL),
            row(D_MODEL, COL_GLAR // D_MODEL),
            row(3 * D_MODEL, COL_MERGE // (3 * D_MODEL)),
            row(D_INNER, COL_Z // D_INNER),
            row(D_INNER, 0),
            row(D_Y, 0), row(D_Y, 0),
            _resident((None, 1, D_MODEL), lambda i: (l, 0, 0)),
            _resident((None, 1, D_INNER), lambda i: (l, 0, 0)),
            _resident((None, 1, D_INNER), lambda i: (l, 0, 0)),
            _resident((None, D_MODEL, D_MODEL), lambda i: (l, 0, 0)),
            _resident((None, D_MODEL, D_MODEL), lambda i: (l, 0, 0)),
            _resident((None, D_INNER, D_MODEL), lambda i: (l, 0, 0)),
            _resident((None, D_MODEL, D_MODEL), lambda i: (l, 0, 0)),
            _resident((None, 3, D_MODEL), lambda i: (l, 0, 0)),
            _resident((None, D_MODEL, D_FF), lambda i: (l, 0, 0)),
            _resident((None, D_MODEL, D_FF), lambda i: (l, 0, 1)),
            _resident((None, D_FF, D_MODEL), lambda i: (l, 0, 0)),
            _resident((1, D_MODEL), lambda i: (0, 0)),
        ],
        out_specs=pl.BlockSpec((TM, D_MODEL), lambda i: (i, 0)),
        out_shape=jax.ShapeDtypeStruct((ntiles * TM, D_MODEL), F32),
        compiler_params=_cparams(("arbitrary",)),
        name="merge",
    )(xa, mod, p, p, p, p, xbc, yf, yb, gng, dx, sng, wbr, wbg, wbs, wout,
      norm_g, wi, wi, wf, final_g)


def _rope_tables(n_ctx, n_lat):
    rows = n_lat // GRID_W
    nf = DK_RET // 4
    freq = ROPE_BASE ** (-jnp.arange(nf, dtype=F32) / nf)
    ang_r = jnp.arange(rows, dtype=F32)[:, None] * freq
    ang_c = jnp.arange(GRID_W, dtype=F32)[:, None] * freq

    def table(fn):
        tr = jnp.broadcast_to(fn(ang_r)[:, None, :], (rows, GRID_W, nf))
        tc = jnp.broadcast_to(fn(ang_c)[None, :, :], (rows, GRID_W, nf))
        return jnp.concatenate([tr, tc], axis=-1).reshape(n_lat, 2 * nf)

    cos, sin = table(jnp.cos), table(jnp.sin)
    cos_t = jnp.concatenate([cos, cos], axis=1)
    sin_t = jnp.concatenate([-sin, sin], axis=1)
    cos_t = jnp.concatenate([jnp.ones((n_ctx, DK_RET), F32), cos_t], axis=0)
    sin_t = jnp.concatenate([jnp.zeros((n_ctx, DK_RET), F32), sin_t], axis=0)
    return cos_t, sin_t


WP_COLS = 1024
W_IN_SRC = ((COL_RET, 0, D_QKV), (COL_GLA, 3072, D_QKV), (COL_RETG, 2048, D_MODEL),
            (COL_GLAR, 5120, D_MODEL), (COL_XBC, 8224, D_XBC),
            (COL_MERGE, 11360, 3 * D_MODEL), (COL_Z, 6176, D_INNER))
W_IN_SMALL_SRC = ((6144, 2 * GLA_RANK), (11296, 2 * H_SSD))


def _wprep_kernel(src_ref, wt_ref, o_ref):
    o_ref[...] = wt_ref[0].T.astype(BF16)


def _wsmall_kernel(a_ref, d_ref, o_ref):
    used = a_ref.shape[1] + d_ref.shape[1]
    rows = jnp.concatenate([a_ref[0], d_ref[0], jnp.zeros((D_SMALL - used, D_MODEL), F32)], axis=0)
    o_ref[...] = rows.T


def _wprep(w_in):
    depth = w_in.shape[0]
    wt = jnp.swapaxes(w_in, 1, 2)
    src = []
    for dst, start, width in W_IN_SRC:
        assert dst == len(src) * WP_COLS and width % WP_COLS == 0 and start % 8 == 0
        src.extend(r // 8 for r in range(start, start + width, WP_COLS))
    big = pl.pallas_call(
        _wprep_kernel,
        grid_spec=pltpu.PrefetchScalarGridSpec(
            num_scalar_prefetch=1,
            grid=(depth, D_BIG // WP_COLS),
            in_specs=[pl.BlockSpec((pl.Element(1), pl.Element(WP_COLS), pl.Element(D_MODEL)),
                                   lambda l, j, tab: (l, tab[j] * 8, 0))],
            out_specs=pl.BlockSpec((None, D_MODEL, WP_COLS), lambda l, j, tab: (l, 0, j)),
        ),
        out_shape=jax.ShapeDtypeStruct((depth, D_MODEL, D_BIG), BF16),
        compiler_params=_cparams(("arbitrary", "arbitrary")),
        name="wprep",
    )(jnp.asarray(src, jnp.int32), wt)
    small = pl.pallas_call(
        _wsmall_kernel,
        grid=(depth,),
        in_specs=[pl.BlockSpec((pl.Element(1), pl.Element(n), pl.Element(D_MODEL)),
                               functools.partial(lambda l, s: (l, s, 0), s=s))
                  for s, n in W_IN_SMALL_SRC],
        out_specs=pl.BlockSpec((None, D_MODEL, D_SMALL), lambda l: (l, 0, 0)),
        out_shape=jax.ShapeDtypeStruct((depth, D_MODEL, D_SMALL), F32),
        compiler_params=_cparams(("arbitrary",)),
        name="wsmall",
    )(wt, wt)
    return big, small


def kernel(x, c, ctx, c_ctx, ada_w, ada_b, norm_g, final_norm_g, ffn1_wi, ffn1_wo, ffn2_wi, ffn2_wo,
           w_in, ret_logit, gla_wa2, gla_ba, gla_norm_g, conv_w, conv_b, dt_bias, a_log, ssd_d,
           ssd_norm_g, wb_ret, wb_gla, wb_ssd, w_out):
    depth = ada_w.shape[0]
    n_ctx, n_lat = ctx.shape[1], x.shape[1]
    assert x.shape[0] == 1 and n_ctx % TM == 0 and n_lat % TM == 0
    nct = n_ctx // TM
    xa = (ctx[0], x[0])

    cond_b = jnp.broadcast_to(jnp.stack([c_ctx, c[0]])[:, :, None], (N_COND, D_MODEL, LANES))
    mod = _adaln(cond_b, ada_w, ada_b).reshape(depth, 8, N_MOD, D_MODEL)

    cos_t, sin_t = _rope_tables(n_ctx, n_lat)
    w_big, w_small = _wprep(w_in)
    colscale = jnp.ones((1, D_BIG), F32)
    colscale = colscale.at[:, COL_RET + 512:COL_RET + 1024].set(DK_RET ** -0.5)
    colscale = colscale.at[:, COL_GLA:COL_GLA + 512].set(DK_GLA ** -0.5)
    colscale = colscale.at[:, COL_RETG:COL_RETG + D_MODEL].set(0.5)
    colscale = colscale.at[:, COL_GLAR:COL_GLAR + D_MODEL].set(0.5)
    colscale = colscale.at[:, COL_MERGE:COL_MERGE + 3 * D_MODEL].set(0.5)
    colscale = colscale.at[:, COL_Z:COL_Z + D_INNER].set(0.5)

    bf = lambda t: t.astype(BF16)
    ffn1_wi, ffn1_wo, ffn2_wi, ffn2_wo = bf(ffn1_wi), bf(ffn1_wo), bf(ffn2_wi), bf(ffn2_wo)
    wb_ret, wb_gla, wb_ssd, w_out = bf(wb_ret), bf(wb_gla), bf(wb_ssd), bf(w_out)

    logit8 = jnp.broadcast_to(ret_logit.reshape(depth, 2 * H_RET, 1), (depth, 2 * H_RET, 128))
    wa_pad = jnp.zeros((depth, 2, D_SMALL, H_GLA * DK_GLA), F32)
    wa_pad = wa_pad.at[:, 0, SM_AF:SM_AF + GLA_RANK].set(gla_wa2[:, 0])
    wa_pad = wa_pad.at[:, 1, SM_AB:SM_AB + GLA_RANK].set(gla_wa2[:, 1])
    wa_hi = wa_pad.astype(BF16)
    wa_lo = (wa_pad - wa_hi.astype(F32)).astype(BF16)
    ws_hi = w_small.astype(BF16)
    ws_lo = (w_small - ws_hi.astype(F32)).astype(BF16)
    ba = gla_ba.reshape(depth, 2, 1, H_GLA * DK_GLA)
    zpad = jnp.zeros((depth, SM_DTF), F32)
    dtb_full = jnp.concatenate([zpad, dt_bias[:, 0], dt_bias[:, 1], zpad], axis=1).reshape(depth, 1, D_SMALL)
    alog_full = jnp.concatenate([zpad, a_log[:, 0], a_log[:, 1], zpad], axis=1).reshape(depth, 1, D_SMALL)
    conv_w8 = jnp.concatenate([conv_w, jnp.zeros((depth, 8 - CONV_K, D_XBC), F32)], axis=1)
    conv_b3 = conv_b.reshape(depth, 1, D_XBC)
    gng = jnp.tile(gla_norm_g, (1, H_GLA)).reshape(depth, 1, H_GLA * DV_GLA)
    dx = jnp.repeat(ssd_d, SSD_HEADDIM, axis=1).reshape(depth, 1, D_INNER)
    sng = ssd_norm_g.reshape(depth, 1, D_INNER)

    fg = final_norm_g.reshape(1, D_MODEL)
    for l in range(depth):
        xa = _ffn(xa, mod, norm_g, ffn1_wi, ffn1_wo, fg, l, 0, 0, nct)
        p, psm = _inproj(xa, mod, norm_g, cos_t, sin_t, w_big, colscale, ws_hi, ws_lo, l, nct)
        xbc = _conv(p, conv_w8, conv_b3, l, nct)
        yf, yb = _mix_scan(p, psm, xbc, logit8[l], wa_hi[l], wa_lo[l], ba[l],
                           dtb_full[l], alog_full[l], n_ctx)
        xa = _merge(xa, mod, p, xbc, yf, yb, gng, dx, sng,
                    wb_ret, wb_gla, wb_ssd, w_out, norm_g, ffn2_wi, ffn2_wo, fg, l, nct,
                    final=(l == depth - 1))

    return xa[None]
```

```python
import functools
import math

import jax
import jax.numpy as jnp
from jax import lax
from jax.experimental import pallas as pl
from jax.experimental.pallas import tpu as pltpu

F32 = jnp.float32
BF16 = jnp.bfloat16
HIGHEST = lax.Precision.HIGHEST
LOG2E = 1.4426950408889634

D_MODEL = 1024
GRID_W = 64
EPS = 1e-6
N_MOD = 9
D_FF = 2816
H_RET, DK_RET, DV_RET, CHUNK_RET = 4, 128, 256, 128
ROPE_BASE = 10000.0
H_GLA, DK_GLA, DV_GLA, GLA_RANK, CHUNK_GLA = 4, 128, 256, 16, 64
GLA_NORMALIZER = 16.0
GLA_SUB = 16
D_INNER = 2 * D_MODEL
SSD_HEADDIM = 64
H_SSD = D_INNER // SSD_HEADDIM
SSD_GROUPS = 4
D_STATE = 128
CONV_K = 5
CHUNK_SSD = 128
N_BC = SSD_GROUPS * D_STATE
D_XBC = D_INNER + 2 * N_BC

COL_RET = 0
COL_GLA = 2048
D_QKV = 2048
D_Y = 4096
COL_RETG = 4096
COL_GLAR = 5120
COL_XBC = 6144
COL_MERGE = 9216
COL_Z = 12288
D_BIG = 14336
SM_AF, SM_AB, SM_DTF, SM_DTB, D_SMALL = 0, 16, 32, 64, 128

TM = 256
CW = 1024
FF_SPLITS = (0, 1536, D_FF)
VMEM_LIMIT = 56 * 1024 * 1024


def _cparams(sem):
    return pltpu.CompilerParams(dimension_semantics=sem, vmem_limit_bytes=VMEM_LIMIT)


def _resident(shape, index_map):
    return pl.BlockSpec(shape, index_map, pipeline_mode=pl.Buffered(1))


def _sigmoid(x):
    return 0.5 * jnp.tanh(0.5 * x) + 0.5


def _silu(x):
    return x * _sigmoid(x)


def _silu_of_half(h):
    return h * jnp.tanh(h) + h


def _sigmoid_of_half(h):
    return 0.5 * jnp.tanh(h) + 0.5


def _softplus(x):
    return jnp.maximum(x, 0.0) + jnp.log(1.0 + jnp.exp(-jnp.abs(x)))


def _log_sigmoid(x, precise=False):
    e = jnp.exp(-jnp.abs(x))
    return jnp.minimum(x, 0.0) - (jnp.log1p(e) if precise else jnp.log(1.0 + e))


def _rms(x):
    return x * lax.rsqrt(jnp.mean(x * x, axis=-1, keepdims=True) + EPS)


def _pre(x, g, shift, scale):
    return _rms(x) * g * (1.0 + scale) + shift


def _dot(a, b):
    return jnp.dot(a, b, preferred_element_type=F32)


def _dot_exact(a, b):
    return jnp.dot(a, b, precision=HIGHEST, preferred_element_type=F32)


def _split2(x):
    hi = x.astype(BF16)
    return hi, (x - hi.astype(F32)).astype(BF16)


def _split3(x):
    hi = x.astype(BF16)
    r = x - hi.astype(F32)
    mid = r.astype(BF16)
    return hi, mid, (r - mid.astype(F32)).astype(BF16)


def _dot_sel(m, x):
    hi, mid, lo = _split3(x)
    return _dot(m, hi) + _dot(m, mid) + _dot(m, lo)


def _dot_split(x, w_hi, w_lo):
    x_hi, x_lo = _split2(x)
    return _dot(x_hi, w_hi) + _dot(x_lo, w_hi) + _dot(x_hi, w_lo)


def _dot_nt(a, b):
    return lax.dot_general(a, b, (((1,), (1,)), ((), ())), preferred_element_type=F32)


def _dot_tn(a, b):
    return lax.dot_general(a, b, (((0,), (0,)), ((), ())), preferred_element_type=F32)


N_COND = 2
LANES = 128
ADA_COLS = 2304


def _adaln_kernel(cond_ref, w_ref, b_ref, o_ref, s_ref):
    s_ref[...] = _silu(cond_ref[...])
    width = w_ref.shape[1]
    reps = width // LANES

    def body(k8, accs):
        r0 = pl.multiple_of(k8 * 8, 8)
        wk = w_ref[pl.ds(r0, 8), :]
        return tuple(acc + wk * jnp.concatenate([s_ref[r, pl.ds(r0, 8), :]] * reps, axis=1)
                     for r, acc in enumerate(accs))

    zero = jnp.zeros((8, width), F32)
    accs = lax.fori_loop(0, D_MODEL // 8, body, (zero,) * N_COND)
    rows = [jnp.sum(a, axis=0, keepdims=True) + b_ref[...] for a in accs]
    o_ref[...] = jnp.concatenate(rows + [jnp.zeros((8 - N_COND, width), F32)], axis=0)


def _adaln(cond_b, ada_w, ada_b):
    depth = ada_w.shape[0]
    nblk = N_MOD * D_MODEL // ADA_COLS
    return pl.pallas_call(
        _adaln_kernel,
        grid=(depth, nblk),
        in_specs=[
            pl.BlockSpec((N_COND, D_MODEL, LANES), lambda l, j: (0, 0, 0)),
            pl.BlockSpec((None, D_MODEL, ADA_COLS), lambda l, j: (l, 0, j)),
            pl.BlockSpec((None, 1, ADA_COLS), lambda l, j: (l, 0, j)),
        ],
        out_specs=pl.BlockSpec((None, 8, ADA_COLS), lambda l, j: (l, 0, j)),
        out_shape=jax.ShapeDtypeStruct((depth, 8, N_MOD * D_MODEL), F32),
        scratch_shapes=[pltpu.VMEM((N_COND, D_MODEL, LANES), F32)],
        compiler_params=_cparams(("arbitrary", "arbitrary")),
        name="adaln",
    )(cond_b, ada_w, ada_b.reshape(depth, 1, N_MOD * D_MODEL))


def _ffn_kernel(x_ref, mod_ref, g_ref, wa_ref, wu_ref, wo_ref, fg_ref, o_ref, *, k0, gi, final):
    _ffn_body(x_ref[...], mod_ref, g_ref, wa_ref, wu_ref, wo_ref, fg_ref, o_ref, k0, gi, final)


def _ffn_body(x, mod_ref, g_ref, wa_ref, wu_ref, wo_ref, fg_ref, o_ref, k0, gi, final):
    h = _pre(x, g_ref[gi:gi + 1, :], mod_ref[0, k0:k0 + 1, :], mod_ref[0, k0 + 1:k0 + 2, :])
    hb = h.astype(BF16)
    out = None
    for lo, hi in zip(FF_SPLITS[:-1], FF_SPLITS[1:]):
        sl = slice(lo, hi)
        a = _dot(hb, wa_ref[:, sl])
        u = _dot(hb, wu_ref[:, sl])
        t = (_silu(a) * u).astype(BF16)
        part = _dot(t, wo_ref[sl, :])
        out = part if out is None else out + part
    y = x + 0.5 * mod_ref[0, k0 + 2:k0 + 3, :] * out
    o_ref[...] = _rms(y) * fg_ref[...] if final else y


def _ffn_split_kernel(ctx_ref, lat_ref, mod_ref, g_ref, wa_ref, wu_ref, wo_ref, fg_ref, o_ref,
                      *, k0, gi, nct):
    x = jnp.where(pl.program_id(0) >= nct, lat_ref[...], ctx_ref[...])
    _ffn_body(x, mod_ref, g_ref, wa_ref, wu_ref, wo_ref, fg_ref, o_ref, k0, gi, False)


def _ffn(xs, mod, norm_g, wi, wo, final_g, l, k0, gi, nct):
    split = isinstance(xs, tuple)
    n = xs[0].shape[0] + xs[1].shape[0] if split else xs.shape[0]
    if split:
        kern = functools.partial(_ffn_split_kernel, k0=k0, gi=gi, nct=nct)
        x_specs = [pl.BlockSpec((TM, D_MODEL), lambda i: (jnp.minimum(i, nct - 1), 0)),
                   pl.BlockSpec((TM, D_MODEL), lambda i: (jnp.maximum(i - nct, 0), 0))]
        x_args = list(xs)
    else:
        kern = functools.partial(_ffn_kernel, k0=k0, gi=gi, final=False)
        x_specs = [pl.BlockSpec((TM, D_MODEL), lambda i: (i, 0))]
        x_args = [xs]
    return pl.pallas_call(
        kern,
        grid=(n // TM,),
        in_specs=x_specs + [
            pl.BlockSpec((None, 1, N_MOD, D_MODEL),
                         lambda i: (l, (i >= nct).astype(jnp.int32), 0, 0)),
            _resident((None, 3, D_MODEL), lambda i: (l, 0, 0)),
            _resident((None, D_MODEL, D_FF), lambda i: (l, 0, 0)),
            _resident((None, D_MODEL, D_FF), lambda i: (l, 0, 1)),
            _resident((None, D_FF, D_MODEL), lambda i: (l, 0, 0)),
            _resident((1, D_MODEL), lambda i: (0, 0)),
        ],
        out_specs=pl.BlockSpec((TM, D_MODEL), lambda i: (i, 0)),
        out_shape=jax.ShapeDtypeStruct((n, D_MODEL), F32),
        compiler_params=_cparams(("arbitrary",)),
        name="ffn",
    )(*x_args, mod, norm_g, wi, wi, wo, final_g)


def _inproj_kernel(x_ref, mod_ref, g_ref, cos_ref, sin_ref, w_ref, cs_ref, wsh_ref, wsl_ref,
                   p_ref, ps_ref):
    h = _pre(x_ref[...], g_ref[1:2, :], mod_ref[0, 3:4, :], mod_ref[0, 4:5, :])
    hb = h.astype(BF16)
    hl = _dot(hb, wsh_ref[...])
    ps_ref[...] = (hl[:, 0:D_SMALL] + hl[:, D_SMALL:2 * D_SMALL]
                   + _dot((h - hb.astype(F32)).astype(BF16), wsl_ref[...]))
    for j in range(D_BIG // CW):
        cols = slice(j * CW, (j + 1) * CW)
        acc = _dot(hb, w_ref[:, cols]) * cs_ref[:, cols]
        if j == 0:
            cos = cos_ref[...]
            sin = sin_ref[...]
            for b in range(CW // DK_RET):
                t = acc[:, b * DK_RET:(b + 1) * DK_RET]
                p_ref[:, b * DK_RET:(b + 1) * DK_RET] = (
                    t * cos + pltpu.roll(t, DK_RET // 2, 1) * sin).astype(BF16)
        else:
            p_ref[:, cols] = acc.astype(BF16)


def _inproj(xa, mod, norm_g, cos_t, sin_t, w_big, colscale, ws_hi, ws_lo, l, nct):
    n = xa.shape[0]
    return pl.pallas_call(
        _inproj_kernel,
        grid=(n // TM,),
        in_specs=[
            pl.BlockSpec((TM, D_MODEL), lambda i: (i, 0)),
            pl.BlockSpec((None, 1, N_MOD, D_MODEL), lambda i: (l, (i >= nct).astype(jnp.int32), 0, 0)),
            _resident((None, 3, D_MODEL), lambda i: (l, 0, 0)),
            pl.BlockSpec((TM, DK_RET), lambda i: (i, 0)),
            pl.BlockSpec((TM, DK_RET), lambda i: (i, 0)),
            _resident((None, D_MODEL, D_BIG), lambda i: (l, 0, 0)),
            _resident((1, D_BIG), lambda i: (0, 0)),
            _resident((None, D_MODEL, 2 * D_SMALL), lambda i: (l, 0, 0)),
            _resident((None, D_MODEL, D_SMALL), lambda i: (l, 0, 0)),
        ],
        out_specs=[
            pl.BlockSpec((TM, D_BIG), lambda i: (i, 0)),
            pl.BlockSpec((TM, D_SMALL), lambda i: (i, 0)),
        ],
        out_shape=[
            jax.ShapeDtypeStruct((n, D_BIG), BF16),
            jax.ShapeDtypeStruct((n, D_SMALL), F32),
        ],
        compiler_params=_cparams(("arbitrary",)),
        name="inproj",
    )(xa, mod, norm_g, cos_t, sin_t, w_big, colscale, jnp.concatenate([ws_hi, ws_lo], axis=-1), ws_hi)


CONV_STRIP = 256
CONV_HALO = 16
CONV_BLK = 128


def _conv_kernel(cur_ref, prev_ref, next_ref, w_ref, b_ref, o_ref, ext_ref, *, nct, ntiles):
    i = pl.program_id(0)
    has_prev = jnp.logical_and(i != 0, i != nct)
    has_next = jnp.logical_and(i != nct - 1, i != ntiles - 1)
    halo0 = jnp.zeros((CONV_HALO, D_XBC), BF16)
    ext_ref[0:CONV_HALO, :] = jnp.where(has_prev, prev_ref[...], halo0)
    ext_ref[CONV_HALO:CONV_HALO + TM, :] = cur_ref[...]
    ext_ref[CONV_HALO + TM:2 * CONV_HALO + TM, :] = jnp.where(has_next, next_ref[...], halo0)
    kwin = CONV_BLK + 2 * CONV_HALO
    rr = lax.broadcasted_iota(jnp.int32, (CONV_BLK, kwin), 0)
    cc = lax.broadcasted_iota(jnp.int32, (CONV_BLK, kwin), 1)
    mid = CONV_K // 2
    shift = {k: (cc == rr + CONV_HALO + k - mid).astype(F32).astype(BF16)
             for k in range(CONV_K) if k != mid}
    for blk in range(TM // CONV_BLK):
        rows = slice(blk * CONV_BLK, (blk + 1) * CONV_BLK)
        for s in range(D_XBC // CONV_STRIP):
            cs = slice(s * CONV_STRIP, (s + 1) * CONV_STRIP)
            win = ext_ref[blk * CONV_BLK:blk * CONV_BLK + kwin, cs]
            acc = b_ref[:, cs] + w_ref[mid:mid + 1, cs] * cur_ref[rows, cs].astype(F32)
            for k in range(CONV_K):
                if k != mid:
                    acc = acc + w_ref[k:k + 1, cs] * _dot(shift[k], win)
            o_ref[rows, cs] = _silu(acc).astype(BF16)


def _conv(p, conv_w, conv_b, l, nct):
    n = p.shape[0]
    ntiles = n // TM
    rh = TM // CONV_HALO
    nblkh = n // CONV_HALO
    cb = COL_XBC // D_XBC
    return pl.pallas_call(
        functools.partial(_conv_kernel, nct=nct, ntiles=ntiles),
        grid=(ntiles,),
        in_specs=[
            pl.BlockSpec((TM, D_XBC), lambda i: (i, cb)),
            pl.BlockSpec((CONV_HALO, D_XBC), lambda i: (jnp.maximum(i * rh - 1, 0), cb)),
            pl.BlockSpec((CONV_HALO, D_XBC), lambda i: (jnp.minimum((i + 1) * rh, nblkh - 1), cb)),
            pl.BlockSpec((None, 8, D_XBC), lambda i: (l, 0, 0)),
            pl.BlockSpec((None, 1, D_XBC), lambda i: (l, 0, 0)),
        ],
        out_specs=pl.BlockSpec((TM, D_XBC), lambda i: (i, 0)),
        out_shape=jax.ShapeDtypeStruct((n, D_XBC), BF16),
        scratch_shapes=[pltpu.VMEM((TM + 2 * CONV_HALO, D_XBC), BF16)],
        compiler_params=_cparams(("arbitrary",)),
        name="conv",
    )(p, p, p, conv_w, conv_b)


def _bwd_chunk(t, ncc, n):
    return jnp.where(t < ncc, ncc - 1 - t, n + ncc - 1 - t)


def _ret_init(lg_ref, s_ref, dm_ref, rs_ref, ks_ref, cd_ref):
    c = CHUNK_RET
    s_ref[...] = jnp.zeros(s_ref.shape, F32)
    la = _log_sigmoid(lg_ref[...], precise=True)
    ii = lax.broadcasted_iota(jnp.int32, (c, c), 0)
    jj = lax.broadcasted_iota(jnp.int32, (c, c), 1)
    ir = lax.broadcasted_iota(jnp.int32, (c, 1), 0)
    for d in range(2):
        for h in range(H_RET):
            a = la[d * H_RET + h:d * H_RET + h + 1, :]
            a2 = jnp.concatenate([a, a], axis=1)
            if d == 0:
                dist, keep = ii - jj, ii >= jj
                rpow = ir + 1
                kpow = c - 1 - ir
            else:
                dist, keep = jj - ii, jj >= ii
                rpow = c - ir
                kpow = ir
            dm_ref[d, h] = jnp.where(keep, jnp.exp(dist.astype(F32) * a), 0.0)
            rs_ref[d, h] = jnp.exp(rpow.astype(F32) * a2)
            ks_ref[d, h] = jnp.exp(kpow.astype(F32) * a)
            cd_ref[d, h] = jnp.exp(float(c) * jnp.broadcast_to(a2, (8, DV_RET)))


def _ret_phases(qkf_ref, vf_ref, yf_ref, qkb_ref, vb_ref, yb_ref, s_ref, dm_ref, rs_ref, ks_ref, cd_ref):
    dirs = ((qkf_ref, vf_ref, yf_ref), (qkb_ref, vb_ref, yb_ref))

    def q_of(qk_ref, h):
        return qk_ref[:, h * DK_RET:(h + 1) * DK_RET]

    def k_of(qk_ref, h):
        return qk_ref[:, H_RET * DK_RET + h * DK_RET:H_RET * DK_RET + (h + 1) * DK_RET]

    qk = [[_dot_nt(q_of(qk_ref, h), k_of(qk_ref, h)) for h in range(H_RET)] for qk_ref, _, _ in dirs]
    qs = [[_dot(q_of(qk_ref, h), s_ref[d, h].astype(BF16)) for h in range(H_RET)]
          for d, (qk_ref, _, _) in enumerate(dirs)]
    def outputs():
        for d, (qk_ref, v_ref, y_ref) in enumerate(dirs):
            for h in range(H_RET):
                v = v_ref[:, h * DV_RET:(h + 1) * DV_RET]
                att = (qk[d][h] * dm_ref[d, h]).astype(BF16)
                y = rs_ref[d, h] * qs[d][h] + _dot(att, v)
                y_ref[:, h * DV_RET:(h + 1) * DV_RET] = y.astype(y_ref.dtype)

    def states():
        for d, (qk_ref, v_ref, y_ref) in enumerate(dirs):
            for h in range(H_RET):
                v = v_ref[:, h * DV_RET:(h + 1) * DV_RET]
                kt = (k_of(qk_ref, h).astype(F32) * ks_ref[d, h]).astype(BF16)
                s_ref[d, h] = cd_ref[d, h][0:1, :] * s_ref[d, h] + _dot_tn(kt, v)

    return outputs, states


GLA_BLOCK = 128


def _gla_dir(d, qk_ref, v_ref, ps_ref, wah_ref, wal_ref, ba_ref, y_ref, st_ref):
    c = CHUNK_GLA
    nb = GLA_BLOCK
    nsub = c // GLA_SUB
    bi = lax.broadcasted_iota(jnp.int32, (nb, nb), 0)
    bj = lax.broadcasted_iota(jnp.int32, (nb, nb), 1)
    same = (bi // c) == (bj // c)
    sub_start = jnp.bitwise_and(bi, -GLA_SUB)
    if d == 0:
        cum = jnp.logical_and(same, bj <= bi)
        ref_m = jnp.logical_and(same, bj < sub_start)
    else:
        cum = jnp.logical_and(same, bj >= bi)
        ref_m = jnp.logical_and(same, bj >= sub_start + GLA_SUB)
    tri = jnp.concatenate([cum, ref_m], axis=0).astype(F32).astype(BF16)
    z = _dot_split(ps_ref[...], wah_ref[d], wal_ref[d]) + ba_ref[d]
    a = _log_sigmoid(z) * (1.0 / GLA_NORMALIZER)
    gr = _dot_sel(tri, a)
    ii = lax.broadcasted_iota(jnp.int32, (c, c), 0)
    jj = lax.broadcasted_iota(jnp.int32, (c, c), 1)
    causal = (jj <= ii) if d == 0 else (jj >= ii)
    end_row = c - 1 if d == 0 else 0
    col_blk = jnp.right_shift(jj, int(math.log2(GLA_SUB)))

    def sub_rows(jb):
        return slice(jb * GLA_SUB, c) if d == 0 else slice(0, (jb + 1) * GLA_SUB)

    order = list(range(nb // c)) if d == 0 else list(range(nb // c - 1, -1, -1))
    pre = {}
    for ci in order:
        rows = slice(ci * c, (ci + 1) * c)
        g_all = gr[ci * c:(ci + 1) * c]
        r_all = gr[nb + ci * c:nb + (ci + 1) * c]
        for h in range(H_GLA):
            ks = slice(h * DK_GLA, (h + 1) * DK_GLA)
            q = qk_ref[rows, h * DK_GLA:(h + 1) * DK_GLA].astype(F32)
            k = qk_ref[rows, H_GLA * DK_GLA + h * DK_GLA:H_GLA * DK_GLA + (h + 1) * DK_GLA].astype(F32)
            g = g_all[:, ks]
            r = r_all[:, ks]
            g_end = g[end_row:end_row + 1, :]
            kt = (k * jnp.exp(r - g)).astype(BF16)
            qs = []
            for jb in range(nsub):
                rj = r[jb * GLA_SUB:jb * GLA_SUB + 1, :]
                need = sub_rows(jb)
                qs.append((q[need] * jnp.exp(jnp.minimum(g[need] - rj, 0.0))).astype(BF16))
            pm = _dot_nt(jnp.concatenate(qs, axis=0), kt)
            qg = (q * jnp.exp(g)).astype(BF16)
            ke = (k * jnp.exp(g_end - g)).astype(BF16)
            pre[(ci, h)] = (pm, qg, ke, jnp.exp(g_end))

    def finish(ci, h):
        rows = slice(ci * c, (ci + 1) * c)
        pm, qg, ke, e_end = pre[(ci, h)]
        v = v_ref[rows, h * DV_GLA:(h + 1) * DV_GLA]
        att = jnp.zeros((c, c), F32)
        off = 0
        for jb in range(nsub):
            need = sub_rows(jb)
            nr = need.stop - need.start
            pads = [jnp.zeros((need.start, c), F32), pm[off:off + nr], jnp.zeros((c - need.stop, c), F32)]
            piece = jnp.concatenate([t for t in pads if t.shape[0]], axis=0)
            att = jnp.where(col_blk == jb, piece, att)
            off += nr
        att = jnp.where(causal, att, 0.0).astype(BF16)
        st = st_ref[d, h]
        y = _dot_nt(qg, st.astype(BF16)) + _dot(att, v)
        y_ref[rows, h * DV_GLA:(h + 1) * DV_GLA] = y.astype(y_ref.dtype)
        st_ref[d, h] = e_end * st + _dot_tn(v, ke)

    return [[functools.partial(finish, ci, h) for h in range(H_GLA)] for ci in order]


def _ssd_dir(d, xbc_ref, ps_ref, dtb_ref, alog_ref, y_ref, s_ref, e_ref):
    c = CHUNK_SSD
    base = SM_DTF if d == 0 else SM_DTB
    ii = lax.broadcasted_iota(jnp.int32, (c, c), 0)
    jj = lax.broadcasted_iota(jnp.int32, (c, c), 1)
    keep = (jj <= ii) if d == 0 else (jj >= ii)
    end_row = c - 1 if d == 0 else 0
    lane = lax.broadcasted_iota(jnp.int32, (1, D_SMALL), 1)
    in_dir = jnp.logical_and(lane >= base, lane < base + H_SSD)
    dt = jnp.where(in_dir, _softplus(ps_ref[...] + dtb_ref[...]), 0.0)
    a = dt * jnp.where(in_dir, -jnp.exp(alog_ref[...]), 0.0)
    g = _dot_sel(keep.astype(F32).astype(BF16), a)
    g2 = g * LOG2E
    gt = ((g - jnp.where(in_dir, jnp.log(dt), 0.0)) * LOG2E).T
    g_end = g[end_row:end_row + 1, :]
    f_state = dt * jnp.exp(g_end - g)
    f_y = jnp.exp(g)
    f_end = jnp.broadcast_to(jnp.exp(g_end), (8, D_SMALL))
    fs_hi, fs_lo = _split2(jnp.concatenate([f_state, f_y, f_end], axis=0))
    fx = _dot(jnp.concatenate([fs_hi, fs_lo], axis=1), e_ref[d])
    wv = (xbc_ref[:, 0:D_INNER].astype(F32) * fx[0:c]).astype(BF16)
    egx = fx[c:2 * c]
    eex = fx[2 * c:2 * c + 1]
    lane2 = lax.broadcasted_iota(jnp.int32, (c, 2 * SSD_HEADDIM), 1)
    hpg = H_SSD // SSD_GROUPS
    gw = hpg * SSD_HEADDIM
    def b_of(grp):
        return xbc_ref[:, D_INNER + grp * D_STATE:D_INNER + (grp + 1) * D_STATE]

    def c_of(grp):
        return xbc_ref[:, D_INNER + N_BC + grp * D_STATE:D_INNER + N_BC + (grp + 1) * D_STATE]

    cbs = [_dot_nt(c_of(grp), b_of(grp)).astype(BF16) for grp in range(SSD_GROUPS)]
    yis = [_dot(c_of(grp), s_ref[d, grp].astype(BF16)) for grp in range(SSD_GROUPS)]

    def group(grp):
        bg = b_of(grp)
        cb = cbs[grp]
        sg = s_ref[d, grp]
        yi = yis[grp]
        for pr in range(hpg // 2):
            h0 = grp * hpg + 2 * pr
            cols = slice(h0 * SSD_HEADDIM, (h0 + 2) * SSD_HEADDIM)
            atts = []
            for hh in (h0, h0 + 1):
                col = base + hh
                diff = g2[:, col:col + 1] - gt[col:col + 1, :]
                dec = jnp.exp2(jnp.where(keep, diff, -jnp.inf))
                atts.append(cb * dec.astype(BF16))
            v2 = xbc_ref[:, cols]
            vv = jnp.concatenate([jnp.where(lane2 < SSD_HEADDIM, v2, jnp.zeros_like(v2)),
                                  jnp.where(lane2 >= SSD_HEADDIM, v2, jnp.zeros_like(v2))], axis=0)
            y2 = _dot(jnp.concatenate(atts, axis=1), vv)
            yo = yi[:, pr * 2 * SSD_HEADDIM:(pr + 1) * 2 * SSD_HEADDIM] * egx[:, cols] + y2
            y_ref[:, cols] = yo.astype(y_ref.dtype)
        gcols = slice(grp * gw, (grp + 1) * gw)
        s_ref[d, grp] = eex[:, gcols] * sg + _dot_tn(bg, wv[:, gcols])

    return [functools.partial(group, grp) for grp in range(SSD_GROUPS)]


def _ssd_init(s_ref, e_ref):
    s_ref[...] = jnp.zeros(s_ref.shape, F32)
    row = lax.broadcasted_iota(jnp.int32, (2 * D_SMALL, D_INNER), 0) % D_SMALL
    head = lax.broadcasted_iota(jnp.int32, (2 * D_SMALL, D_INNER), 1) // SSD_HEADDIM
    e_ref[0] = (row - SM_DTF == head).astype(F32).astype(BF16)
    e_ref[1] = (row - SM_DTB == head).astype(F32).astype(BF16)


def _mix_kernel(lg_ref, wah_ref, wal_ref, ba_ref, dtb_ref, alog_ref,
                qkv_f, ps_f, xbc_f, qkv_b, ps_b, xbc_b, y_f, y_b,
                rs_s, r_dm, r_rs, r_ks, r_cd, g_st, s_st, s_e):
    @pl.when(pl.program_id(0) == 0)
    def _():
        _ret_init(lg_ref, rs_s, r_dm, r_rs, r_ks, r_cd)
        g_st[...] = jnp.zeros(g_st.shape, F32)
        _ssd_init(s_st, s_e)

    w = H_RET * DV_RET

    def split_in(ref):
        return (ref.at[:, COL_RET:COL_RET + w], ref.at[:, COL_RET + w:COL_RET + 2 * w],
                ref.at[:, COL_GLA:COL_GLA + w], ref.at[:, COL_GLA + w:COL_GLA + 2 * w])

    def split_out(ref):
        return ref.at[:, 0:w], ref.at[:, w:2 * w], ref.at[:, 2 * w:2 * w + D_INNER]

    rqk_f, rv_f, gqk_f, gv_f = split_in(qkv_f)
    rqk_b, rv_b, gqk_b, gv_b = split_in(qkv_b)
    yr_f, yg_f, ys_f = split_out(y_f)
    yr_b, yg_b, ys_b = split_out(y_b)

    ret_out, ret_state = _ret_phases(rqk_f, rv_f, yr_f, rqk_b, rv_b, yr_b, rs_s, r_dm, r_rs, r_ks, r_cd)
    gla_f = _gla_dir(0, gqk_f, gv_f, ps_f, wah_ref, wal_ref, ba_ref, yg_f, g_st)
    gla_b = _gla_dir(1, gqk_b, gv_b, ps_b, wah_ref, wal_ref, ba_ref, yg_b, g_st)
    ssd_f = _ssd_dir(0, xbc_f, ps_f, dtb_ref, alog_ref, ys_f, s_st, s_e)
    ssd_b = _ssd_dir(1, xbc_b, ps_b, dtb_ref, alog_ref, ys_b, s_st, s_e)
    ret_out()
    for fn in gla_f[0] + gla_b[0] + ssd_f:
        fn()
    ret_state()
    for fn in gla_f[1] + gla_b[1] + ssd_b:
        fn()


def _mix_scan(p, psm, xbc, logit8, wa_hi, wa_lo, ba, dtb_full, alog_full, ncc_rows):
    n = p.shape[0]
    c = CHUNK_RET
    assert c == GLA_BLOCK == CHUNK_SSD
    nch = n // c
    ncc = ncc_rows // c
    bwd = lambda t: _bwd_chunk(t, ncc, nch)
    assert COL_RET == 0 and COL_GLA == D_QKV
    const = lambda shape: pl.BlockSpec(shape, lambda t: (0,) * len(shape))

    def blocks(idx):
        return [
            pl.BlockSpec((c, 2 * D_QKV), lambda t: (idx(t), 0)),
            pl.BlockSpec((c, D_SMALL), lambda t: (idx(t), 0)),
            pl.BlockSpec((c, D_XBC), lambda t: (idx(t), 0)),
        ]

    fwd = lambda t: t
    out_w = (D_Y, D_Y)
    out_idx = (fwd, bwd)
    return pl.pallas_call(
        _mix_kernel,
        grid=(nch,),
        in_specs=[
            const((8, 128)),
            const((2, D_SMALL, H_GLA * DK_GLA)),
            const((2, D_SMALL, H_GLA * DK_GLA)),
            const((2, 1, H_GLA * DK_GLA)),
            const((1, D_SMALL)),
            const((1, D_SMALL)),
        ] + blocks(fwd) + blocks(bwd),
        out_specs=[pl.BlockSpec((c, ow), functools.partial(lambda t, f: (f(t), 0), f=f))
                   for ow, f in zip(out_w, out_idx)],
        out_shape=[jax.ShapeDtypeStruct((n, ow), BF16) for ow in out_w],
        scratch_shapes=[
            pltpu.VMEM((2, H_RET, DK_RET, DV_RET), F32),
            pltpu.VMEM((2, H_RET, c, c), F32),
            pltpu.VMEM((2, H_RET, c, DV_RET), F32),
            pltpu.VMEM((2, H_RET, c, DK_RET), F32),
            pltpu.VMEM((2, H_RET, 8, DV_RET), F32),
            pltpu.VMEM((2, H_GLA, DV_GLA, DK_GLA), F32),
            pltpu.VMEM((2, SSD_GROUPS, D_STATE, D_INNER // SSD_GROUPS), F32),
            pltpu.VMEM((2, 2 * D_SMALL, D_INNER), BF16),
        ],
        compiler_params=_cparams(("arbitrary",)),
        name="mix_scan",
    )(logit8, wa_hi, wa_lo, ba, dtb_full, alog_full, p, psm, xbc, p, psm, xbc)


def _segnorm(y, width):
    outs = [_rms(y[:, s * width:(s + 1) * width]) for s in range(y.shape[1] // width)]
    return jnp.concatenate(outs, axis=1)


def _merge_kernel(x_ref, mod_ref, rg_ref, gr_ref, mg_ref, z_ref, xs_ref, yf_ref, yb_ref,
                  gng_ref, dx_ref, sng_ref, wbr_ref, wbg_ref, wbs_ref, wo_ref,
                  ng_ref, wa_ref, wu_ref, wf_ref, fg_ref, o_ref, *, final):
    c_r, c_g, c_s = slice(0, D_MODEL), slice(D_MODEL, 2 * D_MODEL), slice(2 * D_MODEL, D_Y)
    yr = (yf_ref[:, c_r] + yb_ref[:, c_r]).astype(F32)
    br = _segnorm(yr, DV_RET) * _silu_of_half(rg_ref[...].astype(F32))
    b_ret = _dot(br.astype(BF16), wbr_ref[...])
    yg = (yf_ref[:, c_g] + yb_ref[:, c_g]).astype(F32)
    bg = _segnorm(yg, DV_GLA) * gng_ref[...] * _silu_of_half(gr_ref[...].astype(F32))
    b_gla = _dot(bg.astype(BF16), wbg_ref[...])
    ys = (yf_ref[:, c_s] + yb_ref[:, c_s]).astype(F32)
    ys = (ys + dx_ref[...] * xs_ref[...].astype(F32)) * _silu_of_half(z_ref[...].astype(F32))
    bs = _segnorm(ys, D_INNER // SSD_GROUPS) * sng_ref[...]
    b_ssd = _dot(bs.astype(BF16), wbs_ref[...])
    gates = _sigmoid_of_half(mg_ref[...].astype(F32))
    mix = (gates[:, 0:D_MODEL] * b_ret + gates[:, D_MODEL:2 * D_MODEL] * b_gla
           + gates[:, 2 * D_MODEL:3 * D_MODEL] * b_ssd)
    out = _dot(mix.astype(BF16), wo_ref[...])
    x_mid = x_ref[...] + mod_ref[0, 5:6, :] * out
    _ffn_body(x_mid, mod_ref, ng_ref, wa_ref, wu_ref, wf_ref, fg_ref, o_ref, 6, 2, final)


def _merge(xa, mod, p, xbc, yf, yb, gng, dx, sng, wbr, wbg, wbs, wout,
           norm_g, wi, wf, final_g, l, nct, final):
    n = xa.shape[0]
    skip = nct if final else 0
    ntiles = n // TM - skip
    row = lambda w, cb: pl.BlockSpec((TM, w), lambda i: (i + skip, cb))
    return pl.pallas_call(
        functools.partial(_merge_kernel, final=final),
        grid=(ntiles,),
        in_specs=[
            row(D_MODEL, 0),
            pl.BlockSpec((None, 1, N_MOD, D_MODEL),
                         lambda i: (l, (i + skip >= nct).astype(jnp.int32), 0, 0)),
            row(D_MODEL, COL_RETG // D_MODEL),
            row(D_MODEL, COL_GLAR // D_MODEL),
            row(3 * D_MODEL, COL_MERGE // (3 * D_MODEL)),
            row(D_INNER, COL_Z // D_INNER),
            row(D_INNER, 0),
            row(D_Y, 0), row(D_Y, 0),
            _resident((None, 1, D_MODEL), lambda i: (l, 0, 0)),
            _resident((None, 1, D_INNER), lambda i: (l, 0, 0)),
            _resident((None, 1, D_INNER), lambda i: (l, 0, 0)),
            _resident((None, D_MODEL, D_MODEL), lambda i: (l, 0, 0)),
            _resident((None, D_MODEL, D_MODEL), lambda i: (l, 0, 0)),
            _resident((None, D_INNER, D_MODEL), lambda i: (l, 0, 0)),
            _resident((None, D_MODEL, D_MODEL), lambda i: (l, 0, 0)),
            _resident((None, 3, D_MODEL), lambda i: (l, 0, 0)),
            _resident((None, D_MODEL, D_FF), lambda i: (l, 0, 0)),
            _resident((None, D_MODEL, D_FF), lambda i: (l, 0, 1)),
            _resident((None, D_FF, D_MODEL), lambda i: (l, 0, 0)),
            _resident((1, D_MODEL), lambda i: (0, 0)),
        ],
        out_specs=pl.BlockSpec((TM, D_MODEL), lambda i: (i, 0)),
        out_shape=jax.ShapeDtypeStruct((ntiles * TM, D_MODEL), F32),
        compiler_params=_cparams(("arbitrary",)),
        name="merge",
    )(xa, mod, p, p, p, p, xbc, yf, yb, gng, dx, sng, wbr, wbg, wbs, wout,
      norm_g, wi, wi, wf, final_g)


def _rope_tables(n_ctx, n_lat):
    rows = n_lat // GRID_W
    nf = DK_RET // 4
    freq = ROPE_BASE ** (-jnp.arange(nf, dtype=F32) / nf)
    ang_r = jnp.arange(rows, dtype=F32)[:, None] * freq
    ang_c = jnp.arange(GRID_W, dtype=F32)[:, None] * freq

    def table(fn):
        tr = jnp.broadcast_to(fn(ang_r)[:, None, :], (rows, GRID_W, nf))
        tc = jnp.broadcast_to(fn(ang_c)[None, :, :], (rows, GRID_W, nf))
        return jnp.concatenate([tr, tc], axis=-1).reshape(n_lat, 2 * nf)

    cos, sin = table(jnp.cos), table(jnp.sin)
    cos_t = jnp.concatenate([cos, cos], axis=1)
    sin_t = jnp.concatenate([-sin, sin], axis=1)
    cos_t = jnp.concatenate([jnp.ones((n_ctx, DK_RET), F32), cos_t], axis=0)
    sin_t = jnp.concatenate([jnp.zeros((n_ctx, DK_RET), F32), sin_t], axis=0)
    return cos_t, sin_t


WP_COLS = 1024
W_IN_SRC = ((COL_RET, 0, D_QKV), (COL_GLA, 3072, D_QKV), (COL_RETG, 2048, D_MODEL),
            (COL_GLAR, 5120, D_MODEL), (COL_XBC, 8224, D_XBC),
            (COL_MERGE, 11360, 3 * D_MODEL), (COL_Z, 6176, D_INNER))
W_IN_SMALL_SRC = ((6144, 2 * GLA_RANK), (11296, 2 * H_SSD))


def _wprep_kernel(src_ref, wt_ref, o_ref):
    o_ref[...] = wt_ref[0].T.astype(BF16)


def _wsmall_kernel(a_ref, d_ref, o_ref):
    used = a_ref.shape[1] + d_ref.shape[1]
    rows = jnp.concatenate([a_ref[0], d_ref[0], jnp.zeros((D_SMALL - used, D_MODEL), F32)], axis=0)
    o_ref[...] = rows.T


def _wprep(w_in):
    depth = w_in.shape[0]
    wt = jnp.swapaxes(w_in, 1, 2)
    src = []
    for dst, start, width in W_IN_SRC:
        assert dst == len(src) * WP_COLS and width % WP_COLS == 0 and start % 8 == 0
        src.extend(r // 8 for r in range(start, start + width, WP_COLS))
    big = pl.pallas_call(
        _wprep_kernel,
        grid_spec=pltpu.PrefetchScalarGridSpec(
            num_scalar_prefetch=1,
            grid=(depth, D_BIG // WP_COLS),
            in_specs=[pl.BlockSpec((pl.Element(1), pl.Element(WP_COLS), pl.Element(D_MODEL)),
                                   lambda l, j, tab: (l, tab[j] * 8, 0))],
            out_specs=pl.BlockSpec((None, D_MODEL, WP_COLS), lambda l, j, tab: (l, 0, j)),
        ),
        out_shape=jax.ShapeDtypeStruct((depth, D_MODEL, D_BIG), BF16),
        compiler_params=_cparams(("arbitrary", "arbitrary")),
        name="wprep",
    )(jnp.asarray(src, jnp.int32), wt)
    small = pl.pallas_call(
        _wsmall_kernel,
        grid=(depth,),
        in_specs=[pl.BlockSpec((pl.Element(1), pl.Element(n), pl.Element(D_MODEL)),
                               functools.partial(lambda l, s: (l, s, 0), s=s))
                  for s, n in W_IN_SMALL_SRC],
        out_specs=pl.BlockSpec((None, D_MODEL, D_SMALL), lambda l: (l, 0, 0)),
        out_shape=jax.ShapeDtypeStruct((depth, D_MODEL, D_SMALL), F32),
        compiler_params=_cparams(("arbitrary",)),
        name="wsmall",
    )(wt, wt)
    return big, small


def kernel(x, c, ctx, c_ctx, ada_w, ada_b, norm_g, final_norm_g, ffn1_wi, ffn1_wo, ffn2_wi, ffn2_wo,
           w_in, ret_logit, gla_wa2, gla_ba, gla_norm_g, conv_w, conv_b, dt_bias, a_log, ssd_d,
           ssd_norm_g, wb_ret, wb_gla, wb_ssd, w_out):
    depth = ada_w.shape[0]
    n_ctx, n_lat = ctx.shape[1], x.shape[1]
    assert x.shape[0] == 1 and n_ctx % TM == 0 and n_lat % TM == 0
    nct = n_ctx // TM
    xa = (ctx[0], x[0])

    cond_b = jnp.broadcast_to(jnp.stack([c_ctx, c[0]])[:, :, None], (N_COND, D_MODEL, LANES))
    mod = _adaln(cond_b, ada_w, ada_b).reshape(depth, 8, N_MOD, D_MODEL)

    cos_t, sin_t = _rope_tables(n_ctx, n_lat)
    w_big, w_small = _wprep(w_in)
    colscale = jnp.ones((1, D_BIG), F32)
    colscale = colscale.at[:, COL_RET + 512:COL_RET + 1024].set(DK_RET ** -0.5)
    colscale = colscale.at[:, COL_GLA:COL_GLA + 512].set(DK_GLA ** -0.5)
    colscale = colscale.at[:, COL_RETG:COL_RETG + D_MODEL].set(0.5)
    colscale = colscale.at[:, COL_GLAR:COL_GLAR + D_MODEL].set(0.5)
    colscale = colscale.at[:, COL_MERGE:COL_MERGE + 3 * D_MODEL].set(0.5)
    colscale = colscale.at[:, COL_Z:COL_Z + D_INNER].set(0.5)

    bf = lambda t: t.astype(BF16)
    ffn1_wi, ffn1_wo, ffn2_wi, ffn2_wo = bf(ffn1_wi), bf(ffn1_wo), bf(ffn2_wi), bf(ffn2_wo)
    wb_ret, wb_gla, wb_ssd, w_out = bf(wb_ret), bf(wb_gla), bf(wb_ssd), bf(w_out)

    logit8 = jnp.broadcast_to(ret_logit.reshape(depth, 2 * H_RET, 1), (depth, 2 * H_RET, 128))
    wa_pad = jnp.zeros((depth, 2, D_SMALL, H_GLA * DK_GLA), F32)
    wa_pad = wa_pad.at[:, 0, SM_AF:SM_AF + GLA_RANK].set(gla_wa2[:, 0])
    wa_pad = wa_pad.at[:, 1, SM_AB:SM_AB + GLA_RANK].set(gla_wa2[:, 1])
    wa_hi = wa_pad.astype(BF16)
    wa_lo = (wa_pad - wa_hi.astype(F32)).astype(BF16)
    ws_hi = w_small.astype(BF16)
    ws_lo = (w_small - ws_hi.astype(F32)).astype(BF16)
    ba = gla_ba.reshape(depth, 2, 1, H_GLA * DK_GLA)
    zpad = jnp.zeros((depth, SM_DTF), F32)
    dtb_full = jnp.concatenate([zpad, dt_bias[:, 0], dt_bias[:, 1], zpad], axis=1).reshape(depth, 1, D_SMALL)
    alog_full = jnp.concatenate([zpad, a_log[:, 0], a_log[:, 1], zpad], axis=1).reshape(depth, 1, D_SMALL)
    conv_w8 = jnp.concatenate([conv_w, jnp.zeros((depth, 8 - CONV_K, D_XBC), F32)], axis=1)
    conv_b3 = conv_b.reshape(depth, 1, D_XBC)
    gng = jnp.tile(gla_norm_g, (1, H_GLA)).reshape(depth, 1, H_GLA * DV_GLA)
    dx = jnp.repeat(ssd_d, SSD_HEADDIM, axis=1).reshape(depth, 1, D_INNER)
    sng = ssd_norm_g.reshape(depth, 1, D_INNER)

    fg = final_norm_g.reshape(1, D_MODEL)
    for l in range(depth):
        xa = _ffn(xa, mod, norm_g, ffn1_wi, ffn1_wo, fg, l, 0, 0, nct)
        p, psm = _inproj(xa, mod, norm_g, cos_t, sin_t, w_big, colscale, ws_hi, ws_lo, l, nct)
        xbc = _conv(p, conv_w8, conv_b3, l, nct)
        yf, yb = _mix_scan(p, psm, xbc, logit8[l], wa_hi[l], wa_lo[l], ba[l],
                           dtb_full[l], alog_full[l], n_ctx)
        xa = _merge(xa, mod, p, xbc, yf, yb, gng, dx, sng,
                    wb_ret, wb_gla, wb_ssd, w_out, norm_g, ffn2_wi, ffn2_wo, fg, l, nct,
                    final=(l == depth - 1))

    return xa[None]
```
